```python
import functools
import jax, jax.numpy as jnp
from jax import lax
import numpy as np

D_MODEL = 1024
BATCH = 2
SEQ = 8192
DEPTH = 1
DEC_BATCH = 32
DEC_SEQ = 8
PAST_LEN = 8192
PAGE_SIZE = 128

A_HEADS = 8
A_HEAD_DIM = 64
A_WIDTH = A_HEADS * A_HEAD_DIM
Q_BLOCK = 128
B_HEADS = 4
B_KEY_DIM = 128
B_VAL_DIM = 256
B_KEY_WIDTH = B_HEADS * B_KEY_DIM
B_VAL_WIDTH = B_HEADS * B_VAL_DIM
GATE_RANK = 16
GATE_TEMP = 16.0
GLA_CHUNK = 64
D_FF = 2816
CONV_WIDTH = 3
EPS = 1e-6

IN_SPLITS = (A_WIDTH, A_WIDTH, A_WIDTH, A_HEADS, B_KEY_WIDTH, B_KEY_WIDTH, B_VAL_WIDTH, GATE_RANK, B_VAL_WIDTH, D_MODEL, D_MODEL)
D_IN = 3 * A_WIDTH + A_HEADS + 2 * B_KEY_WIDTH + 2 * B_VAL_WIDTH + GATE_RANK + 2 * D_MODEL

kernel_name = 'fox_gla_parallel_convffn_step'


def rmsnorm(x, g):
    xf = x.astype(jnp.float32)
    xf = xf * lax.rsqrt(jnp.mean(xf * xf, axis=-1, keepdims=True) + EPS)
    return xf.astype(x.dtype) * g


def project_mixer_inputs(hn, w_in, b_f, w_alpha2, b_alpha):
    b, t, _ = hn.shape
    u = hn @ w_in
    offs = np.cumsum(IN_SPLITS)[:-1].tolist()
    q_a, k_a, v_a, f_a, q_b, k_b, v_b, a_low, r_b, gate_a, gate_b = jnp.split(u, offs, axis=-1)
    logf = jax.nn.log_sigmoid((f_a + b_f).astype(jnp.float32))
    log_alpha = jax.nn.log_sigmoid((a_low @ w_alpha2 + b_alpha).astype(jnp.float32)) / GATE_TEMP
    return (q_a.reshape(b, t, A_HEADS, A_HEAD_DIM) * A_HEAD_DIM ** -0.5,
            k_a.reshape(b, t, A_HEADS, A_HEAD_DIM),
            v_a.reshape(b, t, A_HEADS, A_HEAD_DIM),
            logf,
            q_b.reshape(b, t, B_HEADS, B_KEY_DIM) * B_KEY_DIM ** -0.5,
            k_b.reshape(b, t, B_HEADS, B_KEY_DIM),
            v_b.reshape(b, t, B_HEADS, B_VAL_DIM),
            log_alpha.reshape(b, t, B_HEADS, B_KEY_DIM),
            r_b, gate_a, gate_b)


def fox_prompt(q, k, v, logf):
    b, t, h, d = q.shape
    n_blocks = t // Q_BLOCK
    c = jnp.cumsum(logf, axis=1).transpose(0, 2, 1)
    kpos = jnp.arange(t)

    def block(i):
        start = i * Q_BLOCK
        qb = lax.dynamic_slice_in_dim(q, start, Q_BLOCK, axis=1)
        cq = lax.dynamic_slice_in_dim(c, start, Q_BLOCK, axis=2)
        qpos = start + jnp.arange(Q_BLOCK)
        logits = (jnp.einsum('bqhd,bkhd->bhqk', qb, k, preferred_element_type=jnp.float32)
                  + cq[..., :, None] - c[..., None, :])
        logits = jnp.where(kpos[None, :] <= qpos[:, None], logits, -jnp.inf)
        p = jax.nn.softmax(logits, axis=-1).astype(v.dtype)
        return jnp.einsum('bhqk,bkhd->bqhd', p, v)

    out = lax.map(block, jnp.arange(n_blocks))
    return out.transpose(1, 0, 2, 3, 4).reshape(b, t, h * d)


def fox_sample(q, k, v, logf, k_past, v_past, logf_past):
    db, t, h, d = q.shape
    past_len = k_past.shape[1]
    c_new = jnp.cumsum(logf, axis=1).transpose(0, 2, 1)
    lp = logf_past.astype(jnp.float32)
    suffix = (lax.cumsum(lp, axis=1, reverse=True) - lp).transpose(0, 2, 1)
    s_past = (jnp.einsum('bqhd,bkhd->bhqk', q, k_past, preferred_element_type=jnp.float32)
              + c_new[..., :, None] + suffix[..., None, :])
    s_new = (jnp.einsum('bqhd,bkhd->bhqk', q, k, preferred_element_type=jnp.float32)
             + c_new[..., :, None] - c_new[..., None, :])
    causal = jnp.tril(jnp.ones((t, t), dtype=bool))
    s_new = jnp.where(causal, s_new, -jnp.inf)
    p = jax.nn.softmax(jnp.concatenate([s_past, s_new], axis=-1), axis=-1).astype(v.dtype)
    out = (jnp.einsum('bhqk,bkhd->bqhd', p[..., :past_len], v_past)
           + jnp.einsum('bhqk,bkhd->bqhd', p[..., past_len:], v))
    return out.reshape(db, t, h * d)


def gla_chunk(state, xs):
    q, k, v, log_alpha = xs
    qf, kf, vf = q.astype(jnp.float32), k.astype(jnp.float32), v.astype(jnp.float32)
    c = q.shape[2]
    bcum = jnp.cumsum(log_alpha.astype(jnp.float32), axis=2)
    o_inter = jnp.einsum('bhtk,bhkv->bhtv', qf * jnp.exp(bcum), state)
    causal = jnp.tril(jnp.ones((c, c), dtype=bool))
    diff = bcum[:, :, :, None, :] - bcum[:, :, None, :, :]
    decay = jnp.exp(jnp.where(causal[:, :, None], diff, -jnp.inf))
    attn = jnp.einsum('bhtk,bhsk,bhtsk->bhts', qf, kf, decay)
    o = o_inter + jnp.einsum('bhts,bhsv->bhtv', attn, vf)
    b_last = bcum[:, :, -1:, :]
    new_state = (jnp.exp(b_last[:, :, 0, :])[..., None] * state
                 + jnp.einsum('bhsk,bhsv->bhkv', kf * jnp.exp(b_last - bcum), vf))
    return new_state, o


def gla_prompt(q, k, v, log_alpha):
    b, t, h, dk = q.shape
    dv = v.shape[-1]
    n_chunks = t // GLA_CHUNK

    def chunks(a):
        return a.reshape(b, n_chunks, GLA_CHUNK, h, a.shape[-1]).transpose(1, 0, 3, 2, 4)

    s0 = jnp.zeros((b, h, dk, dv), jnp.float32)
    s_fin, o = lax.scan(gla_chunk, s0, (chunks(q), chunks(k), chunks(v), chunks(log_alpha)))
    return o.transpose(1, 0, 3, 2, 4).reshape(b, t, h, dv).astype(v.dtype), s_fin


def gla_sample(state, q, k, v, log_alpha):
    tr = lambda a: a.transpose(0, 2, 1, 3)
    s_new, o = gla_chunk(state.astype(jnp.float32), (tr(q), tr(k), tr(v), tr(log_alpha)))
    return tr(o).astype(v.dtype), s_new


def conv_ffn(hn, conv_buf, w_up, w_conv, b_conv, w_down):
    t = hn.shape[1]
    u = hn @ w_up
    full = jnp.concatenate([conv_buf.astype(u.dtype), u], axis=1)
    conv = b_conv + sum(full[:, i:i + t] * w_conv[i] for i in range(CONV_WIDTH))
    a, g = jnp.split(conv, 2, axis=-1)
    return (jax.nn.gelu(g, approximate=True) * a) @ w_down, full[:, t:]


def layer_forward(x, attn_fn, gla_fn, conv_buf, g_pre_mix, w_in, b_f, w_alpha2, b_alpha, g_gla,
                  w_proj_a, w_proj_b, w_out, g_post_mix, g_pre_ffn, w_up, w_conv, b_conv, w_down, g_post_ffn):
    b, t, _ = x.shape
    hn = rmsnorm(x, g_pre_mix)
    q_a, k_a, v_a, logf, q_b, k_b, v_b, log_alpha, r_b, gate_a, gate_b = project_mixer_inputs(
        hn, w_in, b_f, w_alpha2, b_alpha)
    o_a = attn_fn(q_a, k_a, v_a, logf)
    o_b, gla_state = gla_fn(q_b, k_b, v_b, log_alpha)
    o_b = rmsnorm(o_b, g_gla).reshape(b, t, B_VAL_WIDTH) * jax.nn.silu(r_b)
    merged = jax.nn.sigmoid(gate_a) * (o_a @ w_proj_a) + jax.nn.sigmoid(gate_b) * (o_b @ w_proj_b)
    x = x + rmsnorm(merged @ w_out, g_post_mix)
    f, new_conv = conv_ffn(rmsnorm(x, g_pre_ffn), conv_buf, w_up, w_conv, b_conv, w_down)
    y = x + rmsnorm(f, g_post_ffn)
    return y, k_a, v_a, logf, gla_state, new_conv


def setup_inputs(seed: int = 0) -> dict:
    key = jax.random.key(seed)
    ks = jax.random.split(key, 24)
    n_pages = PAST_LEN // PAGE_SIZE
    n_pool = (DEC_BATCH * n_pages * 5) // 4

    def nrm(k, shape, scale=1.0):
        return scale * jax.random.normal(k, shape, jnp.float32)

    page_table = jax.random.permutation(ks[7], n_pool)[: DEC_BATCH * n_pages].reshape(DEC_BATCH, n_pages).astype(jnp.int32)
    return {
        'x_prompt': nrm(ks[0], (BATCH, SEQ, D_MODEL)),
        'x_sample': nrm(ks[1], (DEC_BATCH, DEC_SEQ, D_MODEL)),
        'cache_k': nrm(ks[2], (DEPTH, n_pool, PAGE_SIZE, A_HEADS, A_HEAD_DIM)),
        'cache_v': nrm(ks[3], (DEPTH, n_pool, PAGE_SIZE, A_HEADS, A_HEAD_DIM)),
        'cache_logf': jax.nn.log_sigmoid(8.0 + nrm(ks[4], (DEPTH, n_pool, PAGE_SIZE, A_HEADS), 0.5)),
        'state_gla': nrm(ks[5], (DEPTH, DEC_BATCH, B_HEADS, B_KEY_DIM, B_VAL_DIM)),
        'state_conv': nrm(ks[6], (DEPTH, DEC_BATCH, CONV_WIDTH - 1, 2 * D_FF)),
        'page_table': page_table,
        'g_pre_mix': 1.0 + nrm(ks[8], (DEPTH, D_MODEL), 0.01),
        'w_in': nrm(ks[9], (DEPTH, D_MODEL, D_IN), D_MODEL ** -0.5),
        'b_f': jnp.linspace(3.0, 8.0, A_HEADS, dtype=jnp.float32) + nrm(ks[10], (DEPTH, A_HEADS), 0.1),
        'w_alpha2': nrm(ks[11], (DEPTH, GATE_RANK, B_KEY_WIDTH), GATE_RANK ** -0.5),
        'b_alpha': nrm(ks[12], (DEPTH, B_KEY_WIDTH), 0.1),
        'g_gla': 1.0 + nrm(ks[13], (DEPTH, B_VAL_DIM), 0.01),
        'w_proj_a': nrm(ks[14], (DEPTH, A_WIDTH, D_MODEL), A_WIDTH ** -0.5),
        'w_proj_b': nrm(ks[15], (DEPTH, B_VAL_WIDTH, D_MODEL), B_VAL_WIDTH ** -0.5),
        'w_out': nrm(ks[16], (DEPTH, D_MODEL, D_MODEL), D_MODEL ** -0.5),
        'g_post_mix': 1.0 + nrm(ks[17], (DEPTH, D_MODEL), 0.01),
        'g_pre_ffn': 1.0 + nrm(ks[18], (DEPTH, D_MODEL), 0.01),
        'w_up': nrm(ks[19], (DEPTH, D_MODEL, 2 * D_FF), D_MODEL ** -0.5),
        'w_conv': nrm(ks[20], (DEPTH, CONV_WIDTH, 2 * D_FF), CONV_WIDTH ** -0.5),
        'b_conv': nrm(ks[21], (DEPTH, 2 * D_FF), 0.01),
        'w_down': nrm(ks[22], (DEPTH, D_FF, D_MODEL), D_FF ** -0.5),
        'g_post_ffn': 1.0 + nrm(ks[23], (DEPTH, D_MODEL), 0.01),
    }


def reference(x_prompt, x_sample, cache_k, cache_v, cache_logf, state_gla, state_conv, page_table,
              g_pre_mix, w_in, b_f, w_alpha2, b_alpha, g_gla, w_proj_a, w_proj_b, w_out, g_post_mix,
              g_pre_ffn, w_up, w_conv, b_conv, w_down, g_post_ffn):
    dec_batch, n_pages = page_table.shape
    past_len = n_pages * PAGE_SIZE
    conv0 = jnp.zeros((x_prompt.shape[0], CONV_WIDTH - 1, 2 * D_FF), x_prompt.dtype)
    yp, ys = x_prompt, x_sample
    kp, vp, fp, gp, cp = [], [], [], [], []
    ksm, vsm, fsm, gsm, csm = [], [], [], [], []
    for l in range(DEPTH):
        w = (g_pre_mix[l], w_in[l], b_f[l], w_alpha2[l], b_alpha[l], g_gla[l], w_proj_a[l], w_proj_b[l],
             w_out[l], g_post_mix[l], g_pre_ffn[l], w_up[l], w_conv[l], b_conv[l], w_down[l], g_post_ffn[l])
        yp, k_a, v_a, logf, st, cb = layer_forward(yp, fox_prompt, gla_prompt, conv0, *w)
        kp.append(k_a); vp.append(v_a); fp.append(logf); gp.append(st); cp.append(cb)
        k_past = cache_k[l][page_table].reshape(dec_batch, past_len, A_HEADS, A_HEAD_DIM)
        v_past = cache_v[l][page_table].reshape(dec_batch, past_len, A_HEADS, A_HEAD_DIM)
        f_past = cache_logf[l][page_table].reshape(dec_batch, past_len, A_HEADS)
        attn_s = functools.partial(fox_sample, k_past=k_past, v_past=v_past, logf_past=f_past)
        gla_s = functools.partial(gla_sample, state_gla[l])
        ys, k_a, v_a, logf, st, cb = layer_forward(ys, attn_s, gla_s, state_conv[l], *w)
        ksm.append(k_a); vsm.append(v_a); fsm.append(logf); gsm.append(st); csm.append(cb)
    new_k_prompt = jnp.stack(kp)
    new_v_prompt = jnp.stack(vp)
    new_logf_prompt = jnp.stack(fp)
    new_gla_prompt = jnp.stack(gp)
    new_conv_prompt = jnp.stack(cp)
    new_k_sample = jnp.stack(ksm)
    new_v_sample = jnp.stack(vsm)
    new_logf_sample = jnp.stack(fsm)
    new_gla_sample = jnp.stack(gsm)
    new_conv_sample = jnp.stack(csm)
    return (yp, ys, new_k_prompt, new_v_prompt, new_logf_prompt, new_gla_prompt, new_conv_prompt,
            new_k_sample, new_v_sample, new_logf_sample, new_gla_sample, new_conv_sample)
```

```python
import functools

import jax
import jax.numpy as jnp
import numpy as np
from jax import lax
from jax.experimental import pallas as pl
from jax.experimental.pallas import tpu as pltpu

F32 = jnp.float32
BF16 = jnp.bfloat16
EPS = 1e-6
GATE_TEMP = 16.0
PAGE = 128
A_HEADS, A_DIM = 8, 64
B_HEADS, B_DK, B_DV = 4, 128, 256
GATE_RANK = 16
LANES = 128
VMEM_LIMIT = 56 * 1024 * 1024
HIGHEST = lax.Precision.HIGHEST
NT = (((1,), (1,)), ((), ()))
TN = (((0,), (0,)), ((), ()))


def _cparams(*sem):
    return pltpu.CompilerParams(dimension_semantics=sem, vmem_limit_bytes=VMEM_LIMIT)


def _resident(shape):
    nd = len(shape)
    return pl.BlockSpec(shape, lambda *_: (0,) * nd, pipeline_mode=pl.Buffered(1))


def _rms(x, g):
    return x * lax.rsqrt(jnp.mean(x * x, axis=-1, keepdims=True) + EPS) * g


def _log_sigmoid(x):
    return jnp.minimum(x, 0.0) - jnp.log1p(jnp.exp(-jnp.abs(x)))


def _dot(a, b):
    return jnp.dot(a, b, preferred_element_type=F32)


def _inproj_kernel(x_ref, g_ref, wbig_ref, wsm_ref, bf_ref, wa2_ref, ba_ref,
                   qa_ref, ka_ref, va_ref, kaf_ref, vaf_ref, logf_ref,
                   qb_ref, kb_ref, vb_ref, la_ref):
    hn = _rms(x_ref[...], g_ref[...]).astype(BF16)
    aw = A_HEADS * A_DIM
    kw = B_HEADS * B_DK

    def mm(lo, hi):
        return _dot(hn, wbig_ref[:, lo:hi])

    qa_ref[...] = (mm(0, aw) * A_DIM ** -0.5).astype(BF16)
    k_a = mm(aw, 2 * aw)
    kaf_ref[...] = k_a
    ka_ref[...] = k_a.astype(BF16)
    v_a = mm(2 * aw, 3 * aw)
    vaf_ref[...] = v_a
    va_ref[...] = v_a.astype(BF16)
    o = 3 * aw
    qb_ref[...] = mm(o, o + kw) * B_DK ** -0.5
    kb_ref[...] = mm(o + kw, o + 2 * kw)
    vb_ref[...] = mm(o + 2 * kw, o + 2 * kw + B_HEADS * B_DV)
    us = _dot(hn, wsm_ref[...])
    logf_ref[...] = _log_sigmoid(us[:, :A_HEADS] + bf_ref[...])
    z = _dot(us[:, LANES:].astype(BF16), wa2_ref[...]) + ba_ref[...]
    la_ref[...] = _log_sigmoid(z) / GATE_TEMP


def _inproj(x, g_pre, wbig, wsm, b_f, wa2, b_alpha, tm):
    t, d = x.shape
    aw, kw, vw = A_HEADS * A_DIM, B_HEADS * B_DK, B_HEADS * B_DV
    row = lambda n: pl.BlockSpec((tm, n), lambda i: (i, 0))
    outs = [(aw, BF16)] * 3 + [(aw, F32)] * 2 + [(A_HEADS, F32), (kw, F32), (kw, F32), (vw, F32), (kw, F32)]
    return pl.pallas_call(
        _inproj_kernel,
        grid=(t // tm,),
        in_specs=[row(d), _resident(g_pre.shape), _resident(wbig.shape), _resident(wsm.shape),
                  _resident(b_f.shape), _resident(wa2.shape), _resident(b_alpha.shape)],
        out_specs=[row(n) for n, _ in outs],
        out_shape=[jax.ShapeDtypeStruct((t, n), dt) for n, dt in outs],
        compiler_params=_cparams("arbitrary"),
        name="inproj",
    )(x, g_pre, wbig, wsm, b_f, wa2, b_alpha)


def _cumsum_kernel(x_ref, tri_ref, o_ref, *, seg):
    n = x_ref.shape[-1]
    carry = jnp.zeros((x_ref.shape[1], 1), F32)
    for s in range(n // seg):
        cs = jnp.dot(x_ref[0, :, s * seg:(s + 1) * seg], tri_ref[...],
                     precision=HIGHEST, preferred_element_type=F32) + carry
        o_ref[0, :, s * seg:(s + 1) * seg] = cs
        carry = cs[:, seg - 1:seg]


def _cumsum_lanes(x, seg=256):
    b, h, n = x.shape
    tri = jnp.triu(jnp.ones((seg, seg), F32))
    blk = pl.BlockSpec((1, h, n), lambda i: (i, 0, 0))
    return pl.pallas_call(
        functools.partial(_cumsum_kernel, seg=seg),
        grid=(b,),
        in_specs=[blk, _resident(tri.shape)],
        out_specs=blk,
        out_shape=jax.ShapeDtypeStruct(x.shape, F32),
        compiler_params=_cparams("arbitrary"),
        name="cumsum_logf",
    )(x, tri)


def _fox_prompt_kernel(q_ref, k_ref, v_ref, c_ref, o_ref, m_sc, l_sc, acc_sc, *, tq):
    i = pl.program_id(2)
    q = q_ref[...]
    lane = lax.broadcasted_iota(jnp.int32, q.shape, 1)
    zero = jnp.zeros_like(q)
    qh = (jnp.where(lane < A_DIM, q, zero), jnp.where(lane >= A_DIM, q, zero))
    m_sc[...] = jnp.full(m_sc.shape, -jnp.inf, F32)
    l_sc[...] = jnp.zeros(l_sc.shape, F32)
    acc_sc[...] = jnp.zeros(acc_sc.shape, F32)

    def block(j, masked):
        start = pl.multiple_of(j * tq, tq)
        kb = k_ref[pl.ds(start, tq), :]
        vb = v_ref[pl.ds(start, tq), :]
        for e in range(2):
            s = lax.dot_general(qh[e], kb, NT, preferred_element_type=F32)
            s = s - c_ref[0, 0, e:e + 1, pl.ds(start, tq)]
            if masked:
                r = lax.broadcasted_iota(jnp.int32, s.shape, 0)
                c = lax.broadcasted_iota(jnp.int32, s.shape, 1)
                s = jnp.where(c <= r, s, -jnp.inf)
            m_old = m_sc[e]
            m_new = jnp.maximum(m_old, jnp.max(s, axis=-1, keepdims=True))
            alpha = jnp.exp(m_old - m_new)
            p = jnp.exp(s - m_new)
            l_sc[e] = alpha * l_sc[e] + jnp.sum(p, axis=-1, keepdims=True)
            acc_sc[e] = alpha * acc_sc[e] + _dot(p.astype(BF16), vb)
            m_sc[e] = m_new

    def body(j, carry):
        block(j, False)
        return carry

    lax.fori_loop(0, i, body, 0)
    block(i, True)
    o_e = acc_sc[0] / l_sc[0]
    o_o = acc_sc[1] / l_sc[1]
    o_ref[...] = jnp.where(lane < A_DIM, o_e, o_o).astype(o_ref.dtype)


def _fox_prompt(qa, ka, va, c, batch, seq, tq):
    nq = seq // tq
    pairs = A_HEADS // 2
    kv_spec = pl.BlockSpec((seq, LANES), lambda b, p, i: (b, p))
    q_spec = pl.BlockSpec((tq, LANES), lambda b, p, i: (b * nq + i, p))
    return pl.pallas_call(
        functools.partial(_fox_prompt_kernel, tq=tq),
        grid=(batch, pairs, nq),
        in_specs=[q_spec, kv_spec, kv_spec,
                  pl.BlockSpec((1, 1, 2, seq), lambda b, p, i: (b, p, 0, 0))],
        out_specs=q_spec,
        out_shape=jax.ShapeDtypeStruct(qa.shape, BF16),
        scratch_shapes=[pltpu.VMEM((2, tq, 1), F32), pltpu.VMEM((2, tq, 1), F32),
                        pltpu.VMEM((2, tq, LANES), F32)],
        compiler_params=_cparams("arbitrary", "arbitrary", "arbitrary"),
        name="fox_prompt",
    )(qa, ka, va, c)


def _fox_sample_kernel(pt_ref, q_ref, kn_ref, vn_ref, lfn_ref, rep_ref, tri_ref, sfx_ref, *rest, pages):
    page_refs = rest[:3 * pages]
    o_ref, m_sc, l_sc, acc_sc, carry_sc = rest[3 * pages:]
    g = pl.program_id(1)
    q = q_ref[0]
    rep = rep_ref[...]
    nq = q.shape[0] // A_HEADS

    def update(s, v):
        m_old = m_sc[...]
        m_new = jnp.maximum(m_old, jnp.max(s, axis=-1, keepdims=True))
        alpha = jnp.exp(m_old - m_new)
        p = jnp.exp(s - m_new)
        l_sc[...] = alpha * l_sc[...] + jnp.sum(p, axis=-1, keepdims=True)
        acc_sc[...] = alpha * acc_sc[...] + _dot(p.astype(BF16), v)
        m_sc[...] = m_new

    @pl.when(g == 0)
    def _():
        m_sc[...] = jnp.full(m_sc.shape, -jnp.inf, F32)
        l_sc[...] = jnp.zeros(l_sc.shape, F32)
        acc_sc[...] = jnp.zeros(acc_sc.shape, F32)
        carry_sc[...] = jnp.zeros(carry_sc.shape, F32)
        cn = jnp.dot(lfn_ref[0], tri_ref[...], precision=HIGHEST, preferred_element_type=F32)
        cn = jnp.dot(rep, cn, precision=HIGHEST, preferred_element_type=F32)
        s = lax.dot_general(q, kn_ref[0], NT, preferred_element_type=F32) - cn
        r = lax.broadcasted_iota(jnp.int32, s.shape, 0)
        c = lax.broadcasted_iota(jnp.int32, s.shape, 1)
        s = jnp.where(c <= (r & (nq - 1)), s, -jnp.inf)
        update(s, vn_ref[0])

    for p in reversed(range(pages)):
        k_ref, v_ref, lf_ref = page_refs[p], page_refs[pages + p], page_refs[2 * pages + p]
        lf = jnp.dot(rep, lf_ref[0], precision=HIGHEST, preferred_element_type=F32)
        sfx = jnp.dot(lf, sfx_ref[...], precision=HIGHEST, preferred_element_type=F32)
        bias = sfx[:, :PAGE] + carry_sc[...]
        carry_sc[...] = carry_sc[...] + sfx[:, PAGE:]
        s = lax.dot_general(q, k_ref[0].astype(BF16), NT, preferred_element_type=F32) + bias
        update(s, v_ref[0].astype(BF16))

    @pl.when(g == pl.num_programs(1) - 1)
    def _():
        o_ref[0] = acc_sc[...] / l_sc[...]


def _fox_sample(page_table, qbd, kn, vn, lfn, cache_k, cache_v, cache_lft, pages=8):
    db, n_pages = page_table.shape
    groups = n_pages // pages
    rows = qbd.shape[1]
    aw = A_HEADS * A_DIM
    rep = jnp.repeat(jnp.eye(A_HEADS, dtype=F32), rows // A_HEADS, axis=0)
    tri = jnp.triu(jnp.ones((PAGE, PAGE), F32))
    idx = np.arange(PAGE)
    sfx = jnp.asarray(np.concatenate([(idx[:, None] > idx[None, :]), np.ones((PAGE, PAGE), bool)], 1), F32)

    def per_b(shape):
        nd = len(shape)
        return pl.BlockSpec((1,) + shape, lambda b, g, pt: (b,) + (0,) * nd)

    def const(shape):
        nd = len(shape)
        return pl.BlockSpec(shape, lambda b, g, pt: (0,) * nd)

    def page_spec(shape, p):
        return pl.BlockSpec((1,) + shape,
                            lambda b, g, pt: (pt[b * n_pages + (groups - 1 - g) * pages + p], 0, 0))

    in_specs = [per_b((rows, aw)), per_b((PAGE, aw)), per_b((PAGE, aw)), per_b((A_HEADS, PAGE)),
                const(rep.shape), const(tri.shape), const(sfx.shape)]
    in_specs += [page_spec((PAGE, aw), p) for p in range(pages)]
    in_specs += [page_spec((PAGE, aw), p) for p in range(pages)]
    in_specs += [page_spec((A_HEADS, PAGE), p) for p in range(pages)]
    grid_spec = pltpu.PrefetchScalarGridSpec(
        num_scalar_prefetch=1,
        grid=(db, groups),
        in_specs=in_specs,
        out_specs=pl.BlockSpec((1, rows, aw), lambda b, g, pt: (b, 0, 0)),
        scratch_shapes=[pltpu.VMEM((rows, 1), F32), pltpu.VMEM((rows, 1), F32),
                        pltpu.VMEM((rows, aw), F32), pltpu.VMEM((rows, PAGE), F32)],
    )
    return pl.pallas_call(
        functools.partial(_fox_sample_kernel, pages=pages),
        grid_spec=grid_spec,
        out_shape=jax.ShapeDtypeStruct((db, rows, aw), F32),
        compiler_params=_cparams("arbitrary", "arbitrary"),
        name="fox_sample",
    )(page_table.reshape(-1), qbd, kn, vn, lfn, rep, tri, sfx,
      *([cache_k] * pages), *([cache_v] * pages), *([cache_lft] * pages))


def _gla_chunk(q, k, v, la, s_old, tril, row0, ones, cast):
    c = q.shape[0]
    bc = jnp.dot(tril, la, precision=HIGHEST, preferred_element_type=F32)
    ref = bc[c // 2 - 1:c // 2, :]
    last = bc[c - 1:c, :]
    qt = q * jnp.exp(bc - ref)
    kt = k * jnp.exp(ref - bc)
    o = _dot(cast(qt * jnp.exp(ref)), cast(s_old))
    a = lax.dot_general(cast(qt), cast(kt), NT, preferred_element_type=F32)
    a = jnp.where(tril > 0, a, 0.0)
    vc = cast(v)
    o = o + _dot(cast(a), vc)
    kd = cast(kt * jnp.exp(last - ref))
    scale = lax.dot_general(row0 * jnp.exp(last), ones, TN, precision=HIGHEST, preferred_element_type=F32)
    s_new = scale * s_old + lax.dot_general(kd, vc, TN, preferred_element_type=F32)
    return o, s_new


def _gla_prompt_kernel(q_ref, k_ref, v_ref, la_ref, tril_ref, row0_ref, ones_ref, o_ref, s_ref, *, chunk):
    @pl.when(pl.program_id(1) == 0)
    def _():
        s_ref[...] = jnp.zeros(s_ref.shape, F32)

    cast = lambda a: a.astype(BF16)
    tril, row0, ones = tril_ref[...], row0_ref[...], ones_ref[...]

    def body(ci, carry):
        rows = pl.ds(pl.multiple_of(ci * chunk, chunk), chunk)
        for h in range(B_HEADS):
            kc = slice(h * B_DK, (h + 1) * B_DK)
            vc = slice(h * B_DV, (h + 1) * B_DV)
            o, s_new = _gla_chunk(q_ref[rows, kc], k_ref[rows, kc], v_ref[rows, vc], la_ref[rows, kc],
                                  s_ref[0, h], tril, row0, ones, cast)
            o_ref[rows, vc] = o
            s_ref[0, h] = s_new
        return carry

    lax.fori_loop(0, q_ref.shape[0] // chunk, body, 0)


def _gla_consts(chunk):
    tril = jnp.tril(jnp.ones((chunk, chunk), F32))
    row0 = jnp.zeros((8, B_DK), F32).at[0].set(1.0)
    ones = jnp.ones((8, B_DV), F32)
    return tril, row0, ones


def _gla_prompt(qb, kb, vb, la, batch, seq, rows, chunk=64):
    steps = seq // rows
    kw, vw = B_HEADS * B_DK, B_HEADS * B_DV
    consts = _gla_consts(chunk)
    rk = pl.BlockSpec((rows, kw), lambda b, i: (b * steps + i, 0))
    rv = pl.BlockSpec((rows, vw), lambda b, i: (b * steps + i, 0))
    return pl.pallas_call(
        functools.partial(_gla_prompt_kernel, chunk=chunk),
        grid=(batch, steps),
        in_specs=[rk, rk, rv, rk] + [_resident(c.shape) for c in consts],
        out_specs=[rv, pl.BlockSpec((1, B_HEADS, B_DK, B_DV), lambda b, i: (b, 0, 0, 0))],
        out_shape=[jax.ShapeDtypeStruct(vb.shape, F32),
                   jax.ShapeDtypeStruct((batch, B_HEADS, B_DK, B_DV), F32)],
        compiler_params=_cparams("arbitrary", "arbitrary"),
        name="gla_prompt",
    )(qb, kb, vb, la, *consts)


def _gla_sample_kernel(q_ref, k_ref, v_ref, la_ref, s_in_ref, tril_ref, row0_ref, ones_ref, o_ref, s_ref):
    cast = lambda a: a
    tril, row0, ones = tril_ref[...], row0_ref[...], ones_ref[...]
    for h in range(B_HEADS):
        kc = slice(h * B_DK, (h + 1) * B_DK)
        vc = slice(h * B_DV, (h + 1) * B_DV)
        o, s_new = _gla_chunk(q_ref[:, kc], k_ref[:, kc], v_ref[:, vc], la_ref[:, kc],
                              s_in_ref[0, h], tril, row0, ones, cast)
        o_ref[:, vc] = o
        s_ref[0, h] = s_new


def _gla_sample(qb, kb, vb, la, state, db, t):
    kw, vw = B_HEADS * B_DK, B_HEADS * B_DV
    consts = _gla_consts(t)
    rk = pl.BlockSpec((t, kw), lambda b: (b, 0))
    rv = pl.BlockSpec((t, vw), lambda b: (b, 0))
    st = pl.BlockSpec((1, B_HEADS, B_DK, B_DV), lambda b: (b, 0, 0, 0))
    return pl.pallas_call(
        _gla_sample_kernel,
        grid=(db,),
        in_specs=[rk, rk, rv, rk, st] + [_resident(c.shape) for c in consts],
        out_specs=[rv, st],
        out_shape=[jax.ShapeDtypeStruct(vb.shape, F32), jax.ShapeDtypeStruct(state.shape, F32)],
        compiler_params=_cparams("arbitrary"),
        name="gla_sample",
    )(qb, kb, vb, la, state, *consts)


def _mix_kernel(x_ref, oa_ref, ob_ref, gpre_ref, wr_ref, wga_ref, wgb_ref, ggla_ref,
                wpa_ref, wpb_ref, wout_ref, gpost_ref, y_ref):
    x = x_ref[...]
    hn = _rms(x, gpre_ref[...]).astype(BF16)
    r_b = _dot(hn, wr_ref[...])
    ob = ob_ref[...]
    parts = []
    for h in range(B_HEADS):
        seg = ob[:, h * B_DV:(h + 1) * B_DV]
        parts.append(_rms(seg, ggla_ref[...]))
    obn = jnp.concatenate(parts, axis=-1) * (r_b * jax.nn.sigmoid(r_b))
    pa = _dot(oa_ref[...], wpa_ref[...])
    pb = _dot(obn.astype(BF16), wpb_ref[...])
    merged = jax.nn.sigmoid(_dot(hn, wga_ref[...])) * pa + jax.nn.sigmoid(_dot(hn, wgb_ref[...])) * pb
    z = _dot(merged.astype(BF16), wout_ref[...])
    y_ref[...] = x + _rms(z, gpost_ref[...])


def _mix(x, oa, ob, g_pre, w_r, w_ga, w_gb, g_gla, w_pa, w_pb, w_out, g_post, tm):
    t, d = x.shape
    row = lambda n: pl.BlockSpec((tm, n), lambda i: (i, 0))
    consts = (g_pre, w_r, w_ga, w_gb, g_gla, w_pa, w_pb, w_out, g_post)
    return pl.pallas_call(
        _mix_kernel,
        grid=(t // tm,),
        in_specs=[row(d), row(oa.shape[1]), row(ob.shape[1])] + [_resident(c.shape) for c in consts],
        out_specs=row(d),
        out_shape=jax.ShapeDtypeStruct(x.shape, F32),
        compiler_params=_cparams("arbitrary"),
        name="mix",
    )(x, oa, ob, *consts)


def _gelu_tanh(g):
    return 0.5 * g * (1.0 + jnp.tanh(np.sqrt(2.0 / np.pi) * (g + 0.044715 * (g * g * g))))


def _ffn_body(x_ref, gpre_ref, wup_ref, wconv_ref, bconv_ref, wdown_ref, gpost_ref, y_ref, tail_ref,
              acc_sc, shifted):
    x = x_ref[...]
    hn = _rms(x, gpre_ref[...]).astype(BF16)
    n_chunks = wdown_ref.shape[0]
    for j in range(n_chunks):
        halves = []
        for half in range(2):
            idx = half * n_chunks + j
            u = _dot(hn, wup_ref[idx])
            u1, u2 = shifted(idx, u)
            w = wconv_ref[idx]
            halves.append(bconv_ref[idx] + u2 * w[0:1] + u1 * w[1:2] + u * w[2:3])
            tail_ref[idx] = u[u.shape[0] - 8:, :]
        h = (_gelu_tanh(halves[1]) * halves[0]).astype(BF16)
        d = _dot(h, wdown_ref[j])
        if j == 0:
            acc_sc[...] = d
        else:
            acc_sc[...] += d
    y_ref[...] = x + _rms(acc_sc[...], gpost_ref[...])


def _ffn_prompt_kernel(x_ref, gpre_ref, wup_ref, wconv_ref, bconv_ref, wdown_ref, gpost_ref,
                       y_ref, tail_ref, acc_sc, hist_sc, *, tiles_per_seq):
    first = pl.program_id(0) % tiles_per_seq == 0

    @pl.when(first)
    def _():
        hist_sc[...] = jnp.zeros(hist_sc.shape, F32)

    def shifted(idx, u):
        rows = lax.broadcasted_iota(jnp.int32, u.shape, 0)
        p1 = hist_sc[idx, 7:8, :]
        p2 = hist_sc[idx, 6:7, :]
        u1 = jnp.where(rows == 0, p1, pltpu.roll(u, 1, axis=0))
        u2 = jnp.where(rows == 0, p2, jnp.where(rows == 1, p1, pltpu.roll(u, 2, axis=0)))
        hist_sc[idx] = u[u.shape[0] - 8:, :]
        return u1, u2

    _ffn_body(x_ref, gpre_ref, wup_ref, wconv_ref, bconv_ref, wdown_ref, gpost_ref, y_ref,
              tail_ref.at[0], acc_sc, shifted)


def _ffn_sample_kernel(x_ref, h1_ref, h2_ref, gpre_ref, wup_ref, wconv_ref, bconv_ref, wdown_ref,
                       gpost_ref, y_ref, u_ref, acc_sc, *, t):
    x = x_ref[...]
    hn = _rms(x, gpre_ref[...]).astype(BF16)
    n_chunks = wdown_ref.shape[0]
    for j in range(n_chunks):
        halves = []
        for half in range(2):
            idx = half * n_chunks + j
            u = _dot(hn, wup_ref[idx])
            pos = lax.broadcasted_iota(jnp.int32, u.shape, 0) & (t - 1)
            u1 = jnp.where(pos < 1, h1_ref[idx], pltpu.roll(u, 1, axis=0))
            u2 = jnp.where(pos < 2, h2_ref[idx], pltpu.roll(u, 2, axis=0))
            w = wconv_ref[idx]
            halves.append(bconv_ref[idx] + u2 * w[0:1] + u1 * w[1:2] + u * w[2:3])
            u_ref[idx] = u
        h = (_gelu_tanh(halves[1]) * halves[0]).astype(BF16)
        d = _dot(h, wdown_ref[j])
        if j == 0:
            acc_sc[...] = d
        else:
            acc_sc[...] += d
    y_ref[...] = x + _rms(acc_sc[...], gpost_ref[...])


def _ffn_prompt(x, g_pre, wup, wconv, bconv, wdown, g_post, seq, tm):
    t, d = x.shape
    nblk, _, nc = wup.shape
    tiles = t // tm
    row = pl.BlockSpec((tm, d), lambda i: (i, 0))
    consts = (g_pre, wup, wconv, bconv, wdown, g_post)
    return pl.pallas_call(
        functools.partial(_ffn_prompt_kernel, tiles_per_seq=seq // tm),
        grid=(tiles,),
        in_specs=[row] + [_resident(c.shape) for c in consts],
        out_specs=[row, pl.BlockSpec((1, nblk, 8, nc), lambda i: (i, 0, 0, 0))],
        out_shape=[jax.ShapeDtypeStruct(x.shape, F32), jax.ShapeDtypeStruct((tiles, nblk, 8, nc), F32)],
        scratch_shapes=[pltpu.VMEM((tm, d), F32), pltpu.VMEM((nblk, 8, nc), F32)],
        compiler_params=_cparams("arbitrary"),
        name="ffn_prompt",
    )(x, *consts)


def _ffn_sample(x, h1, h2, g_pre, wup, wconv, bconv, wdown, g_post, t):
    rows, d = x.shape
    nblk, _, nc = wup.shape
    consts = (g_pre, wup, wconv, bconv, wdown, g_post)
    return pl.pallas_call(
        functools.partial(_ffn_sample_kernel, t=t),
        grid=(1,),
        in_specs=[_resident(a.shape) for a in (x, h1, h2) + consts],
        out_specs=[pl.BlockSpec(x.shape, lambda i: (0, 0)),
                   pl.BlockSpec((nblk, rows, nc), lambda i: (0, 0, 0))],
        out_shape=[jax.ShapeDtypeStruct(x.shape, F32), jax.ShapeDtypeStruct((nblk, rows, nc), F32)],
        scratch_shapes=[pltpu.VMEM((rows, d), F32)],
        compiler_params=_cparams("arbitrary"),
        name="ffn_sample",
    )(x, h1, h2, *consts)


def _pack_layer(w_in, w_alpha2, w_up, w_conv, b_conv, w_down, nc):
    aw, kw, vw = A_HEADS * A_DIM, B_HEADS * B_DK, B_HEADS * B_DV
    d = w_in.shape[0]
    offs = np.cumsum([aw, aw, aw, A_HEADS, kw, kw, vw, GATE_RANK, vw, d, d])[:-1].tolist()
    q_a, k_a, v_a, f_a, q_b, k_b, v_b, a_low, r_b, g_a, g_b = jnp.split(w_in, offs, axis=1)
    wbig = jnp.concatenate([q_a, k_a, v_a, q_b, k_b, v_b], axis=1).astype(BF16)
    pad = lambda a: jnp.pad(a, ((0, 0), (0, LANES - a.shape[1])))
    wsm = jnp.concatenate([pad(f_a), pad(a_low)], axis=1).astype(BF16)
    wa2 = jnp.pad(w_alpha2, ((0, LANES - GATE_RANK), (0, 0))).astype(BF16)
    dff = w_down.shape[0]
    n_chunks = dff // nc
    wup = w_up.reshape(d, 2 * n_chunks, nc).transpose(1, 0, 2).astype(BF16)
    wconv = w_conv.reshape(w_conv.shape[0], 2 * n_chunks, nc).transpose(1, 0, 2)
    bconv = b_conv.reshape(2 * n_chunks, 1, nc)
    wdown = w_down.reshape(n_chunks, nc, d).astype(BF16)
    return dict(wbig=wbig, wsm=wsm, wa2=wa2, w_r=r_b.astype(BF16), w_ga=g_a.astype(BF16),
                w_gb=g_b.astype(BF16), wup=wup, wconv=wconv, bconv=bconv, wdown=wdown)


def kernel(x_prompt, x_sample, cache_k, cache_v, cache_logf, state_gla, state_conv, page_table,
           g_pre_mix, w_in, b_f, w_alpha2, b_alpha, g_gla, w_proj_a, w_proj_b, w_out, g_post_mix,
           g_pre_ffn, w_up, w_conv, b_conv, w_down, g_post_ffn):
    batch, seq, d = x_prompt.shape
    db, dt, _ = x_sample.shape
    depth = w_in.shape[0]
    n_pool = cache_k.shape[1]
    aw = A_HEADS * A_DIM
    dff2 = w_up.shape[2]
    nc = 256
    assert dt == 8 and depth >= 1

    yp = x_prompt.reshape(batch * seq, d)
    ys = x_sample.reshape(db * dt, d)
    outs = [[] for _ in range(10)]
    for l in range(depth):
        w = _pack_layer(w_in[l], w_alpha2[l], w_up[l], w_conv[l], b_conv[l], w_down[l], nc)
        row = lambda a: a[l].reshape(1, -1)
        inproj_w = (row(g_pre_mix), w["wbig"], w["wsm"], row(b_f), w["wa2"], row(b_alpha))
        mix_w = (row(g_pre_mix), w["w_r"], w["w_ga"], w["w_gb"], row(g_gla), w_proj_a[l].astype(BF16),
                 w_proj_b[l].astype(BF16), w_out[l].astype(BF16), row(g_post_mix))
        ffn_w = (row(g_pre_ffn), w["wup"], w["wconv"], w["bconv"], w["wdown"], row(g_post_ffn))

        qa, ka, va, kaf, vaf, logf, qb, kb, vb, la = _inproj(yp, *inproj_w, tm=256)
        lft = logf.reshape(batch, seq, A_HEADS).transpose(0, 2, 1)
        c = _cumsum_lanes(lft).reshape(batch, A_HEADS // 2, 2, seq)
        oa = _fox_prompt(qa, ka, va, c, batch, seq, tq=512)
        ob, gla_state = _gla_prompt(qb, kb, vb, la, batch, seq, rows=512)
        x1 = _mix(yp, oa, ob, *mix_w, tm=512)
        yp, tails = _ffn_prompt(x1, *ffn_w, seq=seq, tm=512)
        tiles_per_seq = seq // 512
        tails = tails.reshape(batch, tiles_per_seq, 2 * (dff2 // 2 // nc), 8, nc)[:, -1, :, 6:, :]
        conv_p = tails.transpose(0, 2, 1, 3).reshape(batch, 2, dff2)
        for lst, val in zip(outs[:5], (kaf.reshape(batch, seq, A_HEADS, A_DIM),
                                       vaf.reshape(batch, seq, A_HEADS, A_DIM),
                                       logf.reshape(batch, seq, A_HEADS), gla_state, conv_p)):
            lst.append(val)

        qa, ka, va, kaf, vaf, logf, qb, kb, vb, la = _inproj(ys, *inproj_w, tm=db * dt)
        q4 = qa.reshape(db, dt, A_HEADS, A_DIM)
        qbd = jnp.einsum("bthd,hg->bhtgd", q4, jnp.eye(A_HEADS, dtype=BF16)).reshape(db, A_HEADS * dt, aw)
        padk = lambda a: jnp.pad(a.reshape(db, dt, aw), ((0, 0), (0, PAGE - dt), (0, 0)))
        lfn = jnp.pad(logf.reshape(db, dt, A_HEADS).transpose(0, 2, 1), ((0, 0), (0, 0), (0, PAGE - dt)))
        o_full = _fox_sample(page_table, qbd, padk(ka), padk(va), lfn,
                             cache_k[l].reshape(n_pool, PAGE, aw), cache_v[l].reshape(n_pool, PAGE, aw),
                             cache_logf[l].transpose(0, 2, 1))
        o5 = o_full.reshape(db, A_HEADS, dt, A_HEADS, A_DIM)
        oa = jnp.einsum("bhtgd,hg->bthd", o5, jnp.eye(A_HEADS, dtype=F32)).reshape(db * dt, aw).astype(BF16)
        ob, gla_state = _gla_sample(qb, kb, vb, la, state_gla[l], db, dt)
        x1 = _mix(ys, oa, ob, *mix_w, tm=db * dt)
        st = state_conv[l].astype(F32)
        zeros = jnp.zeros((db, dt, dff2), F32)
        h1 = zeros.at[:, 0].set(st[:, 1])
        h2 = zeros.at[:, 0].set(st[:, 0]).at[:, 1].set(st[:, 1])
        chunked = lambda a: a.reshape(db * dt, dff2 // nc, nc).transpose(1, 0, 2)
        ys, u = _ffn_sample(x1, chunked(h1), chunked(h2), *ffn_w, t=dt)
        conv_s = u.transpose(1, 0, 2).reshape(db, dt, dff2)[:, dt - 2:, :]
        for lst, val in zip(outs[5:], (kaf.reshape(db, dt, A_HEADS, A_DIM),
                                       vaf.reshape(db, dt, A_HEADS, A_DIM),
                                       logf.reshape(db, dt, A_HEADS), gla_state, conv_s)):
            lst.append(val)

    stacked = [jnp.stack(o) for o in outs]
    return (yp.reshape(batch, seq, d), ys.reshape(db, dt, d), *stacked)
```

```python
import functools

import jax
import jax.numpy as jnp
import numpy as np
from jax import lax
from jax.experimental import pallas as pl
from jax.experimental.pallas import tpu as pltpu

F32 = jnp.float32
BF16 = jnp.bfloat16
EPS = 1e-6
GATE_TEMP = 16.0
PAGE = 128
A_HEADS, A_DIM = 8, 64
B_HEADS, B_DK, B_DV = 4, 128, 256
GATE_RANK = 16
LANES = 128
VMEM_LIMIT = 56 * 1024 * 1024
HIGHEST = lax.Precision.HIGHEST
LOG2E = float(np.log2(np.e))
NT = (((1,), (1,)), ((), ()))
TN = (((0,), (0,)), ((), ()))


def _cparams(*sem):
    return pltpu.CompilerParams(dimension_semantics=sem, vmem_limit_bytes=VMEM_LIMIT)


def _resident(shape):
    nd = len(shape)
    return pl.BlockSpec(shape, lambda *_: (0,) * nd, pipeline_mode=pl.Buffered(1))


def _rms(x, g):
    return x * lax.rsqrt(jnp.mean(x * x, axis=-1, keepdims=True) + EPS) * g


def _log_sigmoid(x):
    return jnp.minimum(x, 0.0) - jnp.log1p(jnp.exp(-jnp.abs(x)))


def _dot(a, b):
    return jnp.dot(a, b, preferred_element_type=F32)


def _trunc_to_bf16_bits(x):
    bits = lax.bitcast_convert_type(x, jnp.int32) & jnp.int32(-65536)
    return lax.bitcast_convert_type(bits, F32)


def _split3(x):
    hi = _trunc_to_bf16_bits(x)
    r1 = x - hi
    mid = _trunc_to_bf16_bits(r1)
    lo = r1 - mid
    return hi.astype(BF16), mid.astype(BF16), lo.astype(BF16)


def _dot_sel(x, sel):
    return sum(_dot(part, sel) for part in _split3(x))


def _sel_dot(sel, x):
    return sum(_dot(sel, part) for part in _split3(x))


def _inproj_kernel(x_ref, g_ref, wbig_ref, wsm_ref, bf_ref, wa2_ref, ba_ref,
                   qa_ref, ka_ref, va_ref, kaf_ref, vaf_ref, logf_ref,
                   qb_ref, kb_ref, vb_ref, la_ref):
    hn = _rms(x_ref[...], g_ref[...]).astype(BF16)
    aw = A_HEADS * A_DIM
    kw = B_HEADS * B_DK

    def mm(lo, hi):
        return _dot(hn, wbig_ref[:, lo:hi])

    qa_ref[...] = (mm(0, aw) * (A_DIM ** -0.5 * LOG2E)).astype(BF16)
    k_a = mm(aw, 2 * aw)
    kaf_ref[...] = k_a
    ka_ref[...] = k_a.astype(BF16)
    v_a = mm(2 * aw, 3 * aw)
    vaf_ref[...] = v_a
    va_ref[...] = v_a.astype(BF16)
    o = 3 * aw
    qb_ref[...] = mm(o, o + kw) * B_DK ** -0.5
    kb_ref[...] = mm(o + kw, o + 2 * kw)
    vb_ref[...] = mm(o + 2 * kw, o + 2 * kw + B_HEADS * B_DV)
    us = _dot(hn, wsm_ref[...])
    logf_ref[...] = _log_sigmoid(us[:, :A_HEADS] + bf_ref[...])
    z = _dot(us[:, LANES:].astype(BF16), wa2_ref[...]) + ba_ref[...]
    la_ref[...] = _log_sigmoid(z) / GATE_TEMP


def _inproj(x, g_pre, wbig, wsm, b_f, wa2, b_alpha, tm):
    t, d = x.shape
    aw, kw, vw = A_HEADS * A_DIM, B_HEADS * B_DK, B_HEADS * B_DV
    row = lambda n: pl.BlockSpec((tm, n), lambda i: (i, 0))
    outs = [(aw, BF16)] * 3 + [(aw, F32)] * 2 + [(A_HEADS, F32), (kw, F32), (kw, F32), (vw, F32), (kw, F32)]
    return pl.pallas_call(
        _inproj_kernel,
        grid=(t // tm,),
        in_specs=[row(d), _resident(g_pre.shape), _resident(wbig.shape), _resident(wsm.shape),
                  _resident(b_f.shape), _resident(wa2.shape), _resident(b_alpha.shape)],
        out_specs=[row(n) for n, _ in outs],
        out_shape=[jax.ShapeDtypeStruct((t, n), dt) for n, dt in outs],
        compiler_params=_cparams("arbitrary"),
        name="inproj",
    )(x, g_pre, wbig, wsm, b_f, wa2, b_alpha)


def _cumsum_kernel(x_ref, tri_ref, o_ref, *, seg):
    n = x_ref.shape[-1]
    carry = jnp.zeros((x_ref.shape[1], 1), F32)
    for s in range(n // seg):
        cs = jnp.dot(x_ref[0, :, s * seg:(s + 1) * seg], tri_ref[...],
                     precision=HIGHEST, preferred_element_type=F32) + carry
        o_ref[0, :, s * seg:(s + 1) * seg] = cs
        carry = cs[:, seg - 1:seg]


def _cumsum_lanes(x, seg=256):
    b, h, n = x.shape
    tri = jnp.triu(jnp.ones((seg, seg), F32))
    blk = pl.BlockSpec((1, h, n), lambda i: (i, 0, 0))
    return pl.pallas_call(
        functools.partial(_cumsum_kernel, seg=seg),
        grid=(b,),
        in_specs=[blk, _resident(tri.shape)],
        out_specs=blk,
        out_shape=jax.ShapeDtypeStruct(x.shape, F32),
        compiler_params=_cparams("arbitrary"),
        name="cumsum_logf",
    )(x, tri)


def _fox_prompt_kernel(q_ref, k_ref, v_ref, kb_ref, o_ref, m_sc, l_sc, acc_sc, *, tq, tk):
    i = pl.program_id(2)
    q = q_ref[...]
    lane = lax.broadcasted_iota(jnp.int32, q.shape, 1)
    zero = jnp.zeros_like(q)
    sel_even = jnp.where(lane < 3, 1.0, 0.0).astype(BF16)
    sel_odd = jnp.where(lane < 3, 0.0, jnp.where(lane < 6, 1.0, 0.0)).astype(BF16)
    q_even = jnp.concatenate([jnp.where(lane < A_DIM, q, zero), sel_even], axis=1)
    q_odd = jnp.concatenate([jnp.where(lane >= A_DIM, q, zero), sel_odd], axis=1)
    qs = jnp.concatenate([q_even, q_odd], axis=0)
    m_sc[...] = jnp.full(m_sc.shape, -jnp.inf, F32)
    l_sc[...] = jnp.zeros(l_sc.shape, F32)
    acc_sc[...] = jnp.zeros(acc_sc.shape, F32)
    reps = tk // LANES

    def block(j, masked):
        start = pl.multiple_of(j * tk, tk)
        ka = jnp.concatenate([k_ref[pl.ds(start, tk), :], kb_ref[pl.ds(start, tk), :]], axis=1)
        s = lax.dot_general(qs, ka, NT, preferred_element_type=F32)
        if masked:
            r = lax.broadcasted_iota(jnp.int32, s.shape, 0)
            c = lax.broadcasted_iota(jnp.int32, s.shape, 1)
            qpos = i * tq + jnp.where(r >= tq, r - tq, r)
            s = jnp.where(start + c <= qpos, s, -jnp.inf)
        m_old = m_sc[...]
        m_new = jnp.maximum(m_old, jnp.max(s, axis=-1, keepdims=True))
        alpha = jnp.exp2(m_old - m_new)
        p = jnp.exp2(s - pltpu.repeat(m_new, reps, axis=1))
        l_sc[...] = alpha * l_sc[...] + jnp.sum(p, axis=-1, keepdims=True)
        acc_sc[...] = alpha * acc_sc[...] + _dot(p.astype(BF16), v_ref[pl.ds(start, tk), :])
        m_sc[...] = m_new

    n_full = i * (tq // tk)

    def body(jj, carry):
        block(2 * jj, False)
        block(2 * jj + 1, False)
        return carry

    lax.fori_loop(0, n_full // 2, body, 0)

    @pl.when(n_full % 2 == 1)
    def _():
        block(n_full - 1, False)

    for d in range(tq // tk):
        block(n_full + d, True)
    o = acc_sc[...] / l_sc[...]
    o_ref[...] = jnp.where(lane < A_DIM, o[:tq], o[tq:]).astype(o_ref.dtype)


def _key_bias_lanes(c):
    batch, heads, seq = c.shape
    parts = jnp.stack(_split3(-c * LOG2E), axis=-1).reshape(batch, heads // 2, 2, seq, 3)
    parts = parts.transpose(0, 3, 1, 2, 4).reshape(batch, seq, heads // 2, 6)
    parts = jnp.pad(parts, ((0, 0), (0, 0), (0, 0), (0, LANES - 6)))
    return parts.reshape(batch * seq, heads // 2 * LANES)


def _fox_prompt(qa, ka, va, kbias, batch, seq, tq, tk):
    assert tq % tk == 0
    nq = seq // tq
    pairs = A_HEADS // 2
    kv_spec = pl.BlockSpec((seq, LANES), lambda b, p, i: (b, p))
    q_spec = pl.BlockSpec((tq, LANES), lambda b, p, i: (b * nq + i, p))
    return pl.pallas_call(
        functools.partial(_fox_prompt_kernel, tq=tq, tk=tk),
        grid=(batch, pairs, nq),
        in_specs=[q_spec, kv_spec, kv_spec, kv_spec],
        out_specs=q_spec,
        out_shape=jax.ShapeDtypeStruct(qa.shape, BF16),
        scratch_shapes=[pltpu.VMEM((2 * tq, LANES), F32), pltpu.VMEM((2 * tq, LANES), F32),
                        pltpu.VMEM((2 * tq, LANES), F32)],
        compiler_params=_cparams("arbitrary", "arbitrary", "arbitrary"),
        name="fox_prompt",
    )(qa, ka, va, kbias)


def _fox_sample_kernel(pt_ref, q_ref, kn_ref, vn_ref, lfn_ref, exp_ref, tri_ref, sfx_ref, *rest, pages):
    page_refs = rest[:3 * pages]
    o_ref, m_sc, l_sc, acc_sc, carry_sc = rest[3 * pages:]
    g = pl.program_id(1)

    def head_rows(ref, h):
        return ref[0, pl.ds(h, PAGE, stride=A_HEADS), :].astype(BF16)

    def scores(k_ref):
        s = _dot(head_rows(k_ref, 0), q_ref[0, 0])
        for h in range(1, A_HEADS):
            s = s + _dot(head_rows(k_ref, h), q_ref[0, h])
        return s

    def update(s_list, v_refs):
        m_old = m_sc[...]
        m_tile = s_list[0]
        for s in s_list[1:]:
            m_tile = jnp.maximum(m_tile, s)
        m_new = jnp.maximum(m_old, jnp.max(m_tile, axis=0, keepdims=True))
        alpha = jnp.exp2(m_old - m_new)
        ps = [jnp.exp2(s - m_new) for s in s_list]
        l_tile = ps[0]
        for p in ps[1:]:
            l_tile = l_tile + p
        l_sc[...] = alpha * l_sc[...] + jnp.sum(l_tile, axis=0, keepdims=True)
        pb = [p.astype(BF16) for p in ps]
        for h in range(A_HEADS):
            pv = None
            for p, v_ref in zip(pb, v_refs):
                d = lax.dot_general(head_rows(v_ref, h), p, TN, preferred_element_type=F32)
                pv = d if pv is None else pv + d
            acc_sc[h] = alpha * acc_sc[h] + pv
        m_sc[...] = m_new

    @pl.when(g == 0)
    def _():
        m_sc[...] = jnp.full(m_sc.shape, -jnp.inf, F32)
        l_sc[...] = jnp.zeros(l_sc.shape, F32)
        acc_sc[...] = jnp.zeros(acc_sc.shape, F32)
        carry_sc[...] = jnp.zeros(carry_sc.shape, F32)
        cn = _sel_dot(tri_ref[...], _dot_sel(lfn_ref[0], exp_ref[...]))
        s = scores(kn_ref) - cn * LOG2E
        r = lax.broadcasted_iota(jnp.int32, s.shape, 0)
        c = lax.broadcasted_iota(jnp.int32, s.shape, 1)
        s = jnp.where(r <= (c & 7), s, -jnp.inf)
        update([s], [vn_ref])

    carry = carry_sc[...]
    order = list(reversed(range(pages)))
    half = pages // 2
    for group in (order[:half], order[half:]):
        s_list = []
        for p in group:
            sfx = _sel_dot(sfx_ref[...], _dot_sel(page_refs[2 * pages + p][0], exp_ref[...]))
            bias = sfx[:PAGE] + carry
            carry = carry + sfx[PAGE:PAGE + 1]
            s_list.append(scores(page_refs[p]) + bias * LOG2E)
        update(s_list, [page_refs[pages + p] for p in group])
    carry_sc[...] = carry

    @pl.when(g == pl.num_programs(1) - 1)
    def _():
        o_ref[0] = acc_sc[...] / l_sc[...]


def _fox_sample(page_table, qbd, kn, vn, lfn, cache_k, cache_v, cache_lf, pages=8):
    db, n_pages = page_table.shape
    groups = n_pages // pages
    cols = qbd.shape[-1]
    expand = jnp.repeat(jnp.eye(A_HEADS, dtype=BF16), cols // A_HEADS, axis=1)
    idx = np.arange(PAGE)
    tri = jnp.asarray(idx[:, None] >= idx[None, :], BF16)
    sfx = jnp.asarray(np.concatenate([idx[:, None] < idx[None, :], np.ones((PAGE, PAGE), bool)], 0), BF16)
    rows = PAGE * A_HEADS

    def per_b(shape):
        nd = len(shape)
        return pl.BlockSpec((1,) + shape, lambda b, g, pt: (b,) + (0,) * nd)

    def const(shape):
        nd = len(shape)
        return pl.BlockSpec(shape, lambda b, g, pt: (0,) * nd)

    def page_spec(shape, p):
        return pl.BlockSpec((1,) + shape,
                            lambda b, g, pt: (pt[b * n_pages + (groups - 1 - g) * pages + p], 0, 0))

    in_specs = [per_b(qbd.shape[1:]), per_b((rows, A_DIM)), per_b((rows, A_DIM)), per_b((PAGE, A_HEADS)),
                const(expand.shape), const(tri.shape), const(sfx.shape)]
    in_specs += [page_spec((rows, A_DIM), p) for p in range(pages)]
    in_specs += [page_spec((rows, A_DIM), p) for p in range(pages)]
    in_specs += [page_spec((PAGE, A_HEADS), p) for p in range(pages)]
    grid_spec = pltpu.PrefetchScalarGridSpec(
        num_scalar_prefetch=1,
        grid=(db, groups),
        in_specs=in_specs,
        out_specs=pl.BlockSpec((1, A_HEADS, A_DIM, cols), lambda b, g, pt: (b, 0, 0, 0)),
        scratch_shapes=[pltpu.VMEM((1, cols), F32), pltpu.VMEM((1, cols), F32),
                        pltpu.VMEM((A_HEADS, A_DIM, cols), F32), pltpu.VMEM((1, cols), F32)],
    )
    return pl.pallas_call(
        functools.partial(_fox_sample_kernel, pages=pages),
        grid_spec=grid_spec,
        out_shape=jax.ShapeDtypeStruct((db, A_HEADS, A_DIM, cols), F32),
        compiler_params=_cparams("arbitrary", "arbitrary"),
        name="fox_sample",
    )(page_table.reshape(-1), qbd, kn, vn, lfn, expand, tri, sfx,
      *([cache_k] * pages), *([cache_v] * pages), *([cache_lf] * pages))


def _gla_chunk(q, k, v, la, s_old, tril, row0, ones, cast):
    c = q.shape[0]
    bc = jnp.dot(tril, la, precision=HIGHEST, preferred_element_type=F32)
    ref = bc[c // 2 - 1:c // 2, :]
    last = bc[c - 1:c, :]
    qt = q * jnp.exp(bc - ref)
    kt = k * jnp.exp(ref - bc)
    o = _dot(cast(qt * jnp.exp(ref)), cast(s_old))
    a = lax.dot_general(cast(qt), cast(kt), NT, preferred_element_type=F32)
    a = jnp.where(tril > 0, a, 0.0)
    vc = cast(v)
    o = o + _dot(cast(a), vc)
    kd = cast(kt * jnp.exp(last - ref))
    scale = lax.dot_general(row0 * jnp.exp(last), ones, TN, precision=HIGHEST, preferred_element_type=F32)
    s_new = scale * s_old + lax.dot_general(kd, vc, TN, preferred_element_type=F32)
    return o, s_new


def _gla_prompt_kernel(q_ref, k_ref, v_ref, la_ref, tril_ref, o_ref, st_ref, *, chunk):
    @pl.when(pl.program_id(1) == 0)
    def _():
        st_ref[...] = jnp.zeros(st_ref.shape, F32)

    tril = tril_ref[...]
    r = lax.broadcasted_iota(jnp.int32, tril.shape, 0)
    c = lax.broadcasted_iota(jnp.int32, tril.shape, 1)
    causal = c <= r

    def body(ci, carry):
        rows = pl.ds(pl.multiple_of(ci * chunk, chunk), chunk)
        bc = _sel_dot(tril, la_ref[rows, :])
        ref = bc[chunk // 2 - 1:chunk // 2, :]
        last = bc[chunk - 1:chunk, :]
        qt = q_ref[rows, :] * jnp.exp(bc - ref)
        kt = k_ref[rows, :] * jnp.exp(ref - bc)
        qs = (qt * jnp.exp(ref)).astype(BF16)
        kd = (kt * jnp.exp(last - ref)).astype(BF16)
        decay = jnp.exp(last)
        qtb = qt.astype(BF16)
        ktb = kt.astype(BF16)
        for h in range(B_HEADS):
            kc = slice(h * B_DK, (h + 1) * B_DK)
            vc = slice(h * B_DV, (h + 1) * B_DV)
            st = st_ref[0, h]
            vb = v_ref[rows, vc].astype(BF16)
            o = lax.dot_general(qs[:, kc], st.astype(BF16), NT, preferred_element_type=F32)
            a = lax.dot_general(qtb[:, kc], ktb[:, kc], NT, preferred_element_type=F32)
            a = jnp.where(causal, a, 0.0).astype(BF16)
            o_ref[rows, vc] = o + _dot(a, vb)
            st_ref[0, h] = st * decay[:, kc] + lax.dot_general(vb, kd[:, kc], TN, preferred_element_type=F32)
        return carry

    lax.fori_loop(0, q_ref.shape[0] // chunk, body, 0, unroll=2)


def _gla_consts(chunk):
    tril = jnp.tril(jnp.ones((chunk, chunk), F32))
    row0 = jnp.zeros((8, B_DK), F32).at[0].set(1.0)
    ones = jnp.ones((8, B_DV), F32)
    return tril, row0, ones


def _gla_prompt(qb, kb, vb, la, batch, seq, rows, chunk=64):
    steps = seq // rows
    kw, vw = B_HEADS * B_DK, B_HEADS * B_DV
    tril = jnp.tril(jnp.ones((chunk, chunk), BF16))
    rk = pl.BlockSpec((rows, kw), lambda b, i: (b * steps + i, 0))
    rv = pl.BlockSpec((rows, vw), lambda b, i: (b * steps + i, 0))
    return pl.pallas_call(
        functools.partial(_gla_prompt_kernel, chunk=chunk),
        grid=(batch, steps),
        in_specs=[rk, rk, rv, rk, _resident(tril.shape)],
        out_specs=[rv, pl.BlockSpec((1, B_HEADS, B_DV, B_DK), lambda b, i: (b, 0, 0, 0))],
        out_shape=[jax.ShapeDtypeStruct(vb.shape, F32),
                   jax.ShapeDtypeStruct((batch, B_HEADS, B_DV, B_DK), F32)],
        compiler_params=_cparams("arbitrary", "arbitrary"),
        name="gla_prompt",
    )(qb, kb, vb, la, tril)


def _gla_sample_kernel(q_ref, k_ref, v_ref, la_ref, s_in_ref, tril_ref, row0_ref, ones_ref, o_ref, s_ref):
    cast = lambda a: a
    tril, row0, ones = tril_ref[...], row0_ref[...], ones_ref[...]
    for h in range(B_HEADS):
        kc = slice(h * B_DK, (h + 1) * B_DK)
        vc = slice(h * B_DV, (h + 1) * B_DV)
        o, s_new = _gla_chunk(q_ref[:, kc], k_ref[:, kc], v_ref[:, vc], la_ref[:, kc],
                              s_in_ref[0, h], tril, row0, ones, cast)
        o_ref[:, vc] = o
        s_ref[0, h] = s_new


def _gla_sample(qb, kb, vb, la, state, db, t):
    kw, vw = B_HEADS * B_DK, B_HEADS * B_DV
    consts = _gla_consts(t)
    rk = pl.BlockSpec((t, kw), lambda b: (b, 0))
    rv = pl.BlockSpec((t, vw), lambda b: (b, 0))
    st = pl.BlockSpec((1, B_HEADS, B_DK, B_DV), lambda b: (b, 0, 0, 0))
    return pl.pallas_call(
        _gla_sample_kernel,
        grid=(db,),
        in_specs=[rk, rk, rv, rk, st] + [_resident(c.shape) for c in consts],
        out_specs=[rv, st],
        out_shape=[jax.ShapeDtypeStruct(vb.shape, F32), jax.ShapeDtypeStruct(state.shape, F32)],
        compiler_params=_cparams("arbitrary"),
        name="gla_sample",
    )(qb, kb, vb, la, state, *consts)


def _mix_kernel(x_ref, oa_ref, ob_ref, gpre_ref, wr_ref, wga_ref, wgb_ref, ggla_ref,
                wpa_ref, wpb_ref, wout_ref, gpost_ref, y_ref):
    x = x_ref[...]
    hn = _rms(x, gpre_ref[...]).astype(BF16)
    r_b = _dot(hn, wr_ref[...])
    ob = ob_ref[...]
    parts = []
    for h in range(B_HEADS):
        seg = ob[:, h * B_DV:(h + 1) * B_DV]
        parts.append(_rms(seg, ggla_ref[...]))
    obn = jnp.concatenate(parts, axis=-1) * (r_b * jax.nn.sigmoid(r_b))
    pa = _dot(oa_ref[...], wpa_ref[...])
    pb = _dot(obn.astype(BF16), wpb_ref[...])
    merged = jax.nn.sigmoid(_dot(hn, wga_ref[...])) * pa + jax.nn.sigmoid(_dot(hn, wgb_ref[...])) * pb
    z = _dot(merged.astype(BF16), wout_ref[...])
    y_ref[...] = x + _rms(z, gpost_ref[...])


def _mix(x, oa, ob, g_pre, w_r, w_ga, w_gb, g_gla, w_pa, w_pb, w_out, g_post, tm):
    t, d = x.shape
    row = lambda n: pl.BlockSpec((tm, n), lambda i: (i, 0))
    consts = (g_pre, w_r, w_ga, w_gb, g_gla, w_pa, w_pb, w_out, g_post)
    return pl.pallas_call(
        _mix_kernel,
        grid=(t // tm,),
        in_specs=[row(d), row(oa.shape[1]), row(ob.shape[1])] + [_resident(c.shape) for c in consts],
        out_specs=row(d),
        out_shape=jax.ShapeDtypeStruct(x.shape, F32),
        compiler_params=_cparams("arbitrary"),
        name="mix",
    )(x, oa, ob, *consts)


def _gelu_tanh(g):
    return 0.5 * g * (1.0 + jnp.tanh(np.sqrt(2.0 / np.pi) * (g + 0.044715 * (g * g * g))))


def _ffn_body(x_ref, gpre_ref, wup_ref, wconv_ref, bconv_ref, wdown_ref, gpost_ref, y_ref, tail_ref,
              acc_sc, shifted):
    x = x_ref[...]
    hn = _rms(x, gpre_ref[...]).astype(BF16)
    n_chunks = wdown_ref.shape[0]
    for j in range(n_chunks):
        halves = []
        for half in range(2):
            idx = half * n_chunks + j
            u = _dot(hn, wup_ref[idx])
            u1, u2 = shifted(idx, u)
            w = wconv_ref[idx]
            halves.append(bconv_ref[idx] + u2 * w[0:1] + u1 * w[1:2] + u * w[2:3])
            tail_ref[idx] = u[u.shape[0] - 8:, :]
        h = (_gelu_tanh(halves[1]) * halves[0]).astype(BF16)
        d = _dot(h, wdown_ref[j])
        if j == 0:
            acc_sc[...] = d
        else:
            acc_sc[...] += d
    y_ref[...] = x + _rms(acc_sc[...], gpost_ref[...])


def _ffn_prompt_kernel(x_ref, gpre_ref, wup_ref, wconv_ref, bconv_ref, wdown_ref, gpost_ref,
                       y_ref, tail_ref, acc_sc, hist_sc, *, tiles_per_seq):
    first = pl.program_id(0) % tiles_per_seq == 0

    @pl.when(first)
    def _():
        hist_sc[...] = jnp.zeros(hist_sc.shape, F32)

    def shifted(idx, u):
        rows = lax.broadcasted_iota(jnp.int32, u.shape, 0)
        p1 = hist_sc[idx, 7:8, :]
        p2 = hist_sc[idx, 6:7, :]
        u1 = jnp.where(rows == 0, p1, pltpu.roll(u, 1, axis=0))
        u2 = jnp.where(rows == 0, p2, jnp.where(rows == 1, p1, pltpu.roll(u, 2, axis=0)))
        hist_sc[idx] = u[u.shape[0] - 8:, :]
        return u1, u2

    _ffn_body(x_ref, gpre_ref, wup_ref, wconv_ref, bconv_ref, wdown_ref, gpost_ref, y_ref,
              tail_ref.at[0], acc_sc, shifted)


def _ffn_sample_kernel(x_ref, h1_ref, h2_ref, gpre_ref, wup_ref, wconv_ref, bconv_ref, wdown_ref,
                       gpost_ref, y_ref, u_ref, acc_sc, *, t):
    x = x_ref[...]
    hn = _rms(x, gpre_ref[...]).astype(BF16)
    n_chunks = wdown_ref.shape[0]
    for j in range(n_chunks):
        halves = []
        for half in range(2):
            idx = half * n_chunks + j
            u = _dot(hn, wup_ref[idx])
            pos = lax.broadcasted_iota(jnp.int32, u.shape, 0) & (t - 1)
            u1 = jnp.where(pos < 1, h1_ref[idx], pltpu.roll(u, 1, axis=0))
            u2 = jnp.where(pos < 2, h2_ref[idx], pltpu.roll(u, 2, axis=0))
            w = wconv_ref[idx]
            halves.append(bconv_ref[idx] + u2 * w[0:1] + u1 * w[1:2] + u * w[2:3])
            u_ref[idx] = u
        h = (_gelu_tanh(halves[1]) * halves[0]).astype(BF16)
        d = _dot(h, wdown_ref[j])
        if j == 0:
            acc_sc[...] = d
        else:
            acc_sc[...] += d
    y_ref[...] = x + _rms(acc_sc[...], gpost_ref[...])


def _ffn_prompt(x, g_pre, wup, wconv, bconv, wdown, g_post, seq, tm):
    t, d = x.shape
    nblk, _, nc = wup.shape
    tiles = t // tm
    row = pl.BlockSpec((tm, d), lambda i: (i, 0))
    consts = (g_pre, wup, wconv, bconv, wdown, g_post)
    return pl.pallas_call(
        functools.partial(_ffn_prompt_kernel, tiles_per_seq=seq // tm),
        grid=(tiles,),
        in_specs=[row] + [_resident(c.shape) for c in consts],
        out_specs=[row, pl.BlockSpec((1, nblk, 8, nc), lambda i: (i, 0, 0, 0))],
        out_shape=[jax.ShapeDtypeStruct(x.shape, F32), jax.ShapeDtypeStruct((tiles, nblk, 8, nc), F32)],
        scratch_shapes=[pltpu.VMEM((tm, d), F32), pltpu.VMEM((nblk, 8, nc), F32)],
        compiler_params=_cparams("arbitrary"),
        name="ffn_prompt",
    )(x, *consts)


def _ffn_sample(x, h1, h2, g_pre, wup, wconv, bconv, wdown, g_post, t):
    rows, d = x.shape
    nblk, _, nc = wup.shape
    consts = (g_pre, wup, wconv, bconv, wdown, g_post)
    return pl.pallas_call(
        functools.partial(_ffn_sample_kernel, t=t),
        grid=(1,),
        in_specs=[_resident(a.shape) for a in (x, h1, h2) + consts],
        out_specs=[pl.BlockSpec(x.shape, lambda i: (0, 0)),
                   pl.BlockSpec((nblk, rows, nc), lambda i: (0, 0, 0))],
        out_shape=[jax.ShapeDtypeStruct(x.shape, F32), jax.ShapeDtypeStruct((nblk, rows, nc), F32)],
        scratch_shapes=[pltpu.VMEM((rows, d), F32)],
        compiler_params=_cparams("arbitrary"),
        name="ffn_sample",
    )(x, h1, h2, *consts)


def _pack_layer(w_in, w_alpha2, w_up, w_conv, b_conv, w_down, nc):
    aw, kw, vw = A_HEADS * A_DIM, B_HEADS * B_DK, B_HEADS * B_DV
    d = w_in.shape[0]
    offs = np.cumsum([aw, aw, aw, A_HEADS, kw, kw, vw, GATE_RANK, vw, d, d])[:-1].tolist()
    q_a, k_a, v_a, f_a, q_b, k_b, v_b, a_low, r_b, g_a, g_b = jnp.split(w_in, offs, axis=1)
    wbig = jnp.concatenate([q_a, k_a, v_a, q_b, k_b, v_b], axis=1).astype(BF16)
    pad = lambda a: jnp.pad(a, ((0, 0), (0, LANES - a.shape[1])))
    wsm = jnp.concatenate([pad(f_a), pad(a_low)], axis=1).astype(BF16)
    wa2 = jnp.pad(w_alpha2, ((0, LANES - GATE_RANK), (0, 0))).astype(BF16)
    dff = w_down.shape[0]
    n_chunks = dff // nc
    wup = w_up.reshape(d, 2 * n_chunks, nc).transpose(1, 0, 2).astype(BF16)
    wconv = w_conv.reshape(w_conv.shape[0], 2 * n_chunks, nc).transpose(1, 0, 2)
    bconv = b_conv.reshape(2 * n_chunks, 1, nc)
    wdown = w_down.reshape(n_chunks, nc, d).astype(BF16)
    return dict(wbig=wbig, wsm=wsm, wa2=wa2, w_r=r_b.astype(BF16), w_ga=g_a.astype(BF16),
                w_gb=g_b.astype(BF16), wup=wup, wconv=wconv, bconv=bconv, wdown=wdown)


def kernel(x_prompt, x_sample, cache_k, cache_v, cache_logf, state_gla, state_conv, page_table,
           g_pre_mix, w_in, b_f, w_alpha2, b_alpha, g_gla, w_proj_a, w_proj_b, w_out, g_post_mix,
           g_pre_ffn, w_up, w_conv, b_conv, w_down, g_post_ffn):
    batch, seq, d = x_prompt.shape
    db, dt, _ = x_sample.shape
    depth = w_in.shape[0]
    n_pool = cache_k.shape[1]
    aw = A_HEADS * A_DIM
    dff2 = w_up.shape[2]
    nc = 256
    assert dt == 8 and depth >= 1

    yp = x_prompt.reshape(batch * seq, d)
    ys = x_sample.reshape(db * dt, d)
    outs = [[] for _ in range(10)]
    for l in range(depth):
        w = _pack_layer(w_in[l], w_alpha2[l], w_up[l], w_conv[l], b_conv[l], w_down[l], nc)
        row = lambda a: a[l].reshape(1, -1)
        inproj_w = (row(g_pre_mix), w["wbig"], w["wsm"], row(b_f), w["wa2"], row(b_alpha))
        mix_w = (row(g_pre_mix), w["w_r"], w["w_ga"], w["w_gb"], row(g_gla), w_proj_a[l].astype(BF16),
                 w_proj_b[l].astype(BF16), w_out[l].astype(BF16), row(g_post_mix))
        ffn_w = (row(g_pre_ffn), w["wup"], w["wconv"], w["bconv"], w["wdown"], row(g_post_ffn))

        qa, ka, va, kaf, vaf, logf, qb, kb, vb, la = _inproj(yp, *inproj_w, tm=256)
        lft = logf.reshape(batch, seq, A_HEADS).transpose(0, 2, 1)
        oa = _fox_prompt(qa, ka, va, _key_bias_lanes(_cumsum_lanes(lft)), batch, seq, tq=512, tk=512)
        ob, gla_state_t = _gla_prompt(qb, kb, vb, la, batch, seq, rows=512)
        gla_state = gla_state_t.transpose(0, 1, 3, 2)
        x1 = _mix(yp, oa, ob, *mix_w, tm=512)
        yp, tails = _ffn_prompt(x1, *ffn_w, seq=seq, tm=512)
        tiles_per_seq = seq // 512
        tails = tails.reshape(batch, tiles_per_seq, 2 * (dff2 // 2 // nc), 8, nc)[:, -1, :, 6:, :]
        conv_p = tails.transpose(0, 2, 1, 3).reshape(batch, 2, dff2)
        for lst, val in zip(outs[:5], (kaf.reshape(batch, seq, A_HEADS, A_DIM),
                                       vaf.reshape(batch, seq, A_HEADS, A_DIM),
                                       logf.reshape(batch, seq, A_HEADS), gla_state, conv_p)):
            lst.append(val)

        qa, ka, va, kaf, vaf, logf, qb, kb, vb, la = _inproj(ys, *inproj_w, tm=db * dt)
        q4 = qa.reshape(db, dt, A_HEADS, A_DIM)
        qbd = jnp.einsum("bthd,hg->bhdgt", q4, jnp.eye(A_HEADS, dtype=BF16)).reshape(
            db, A_HEADS, A_DIM, A_HEADS * dt)
        padk = lambda a: jnp.pad(a.reshape(db, dt, aw), ((0, 0), (0, PAGE - dt), (0, 0))).reshape(
            db, PAGE * A_HEADS, A_DIM)
        lfn = jnp.pad(logf.reshape(db, dt, A_HEADS), ((0, 0), (0, PAGE - dt), (0, 0)))
        o_full = _fox_sample(page_table, qbd, padk(kaf), padk(vaf), lfn,
                             cache_k[l].reshape(n_pool, PAGE * A_HEADS, A_DIM),
                             cache_v[l].reshape(n_pool, PAGE * A_HEADS, A_DIM), cache_logf[l])
        o5 = o_full.reshape(db, A_HEADS, A_DIM, A_HEADS, dt)
        oa = jnp.einsum("bgdht,gh->bthd", o5, jnp.eye(A_HEADS, dtype=F32)).reshape(db * dt, aw).astype(BF16)
        ob, gla_state = _gla_sample(qb, kb, vb, la, state_gla[l], db, dt)
        x1 = _mix(ys, oa, ob, *mix_w, tm=db * dt)
        st = state_conv[l].astype(F32)
        zeros = jnp.zeros((db, dt, dff2), F32)
        h1 = zeros.at[:, 0].set(st[:, 1])
        h2 = zeros.at[:, 0].set(st[:, 0]).at[:, 1].set(st[:, 1])
        chunked = lambda a: a.reshape(db * dt, dff2 // nc, nc).transpose(1, 0, 2)
        ys, u = _ffn_sample(x1, chunked(h1), chunked(h2), *ffn_w, t=dt)
        conv_s = u.transpose(1, 0, 2).reshape(db, dt, dff2)[:, dt - 2:, :]
        for lst, val in zip(outs[5:], (kaf.reshape(db, dt, A_HEADS, A_DIM),
                                       vaf.reshape(db, dt, A_HEADS, A_DIM),
                                       logf.reshape(db, dt, A_HEADS), gla_state, conv_s)):
            lst.append(val)

    stacked = [jnp.stack(o) for o in outs]
    return (yp.reshape(batch, seq, d), ys.reshape(db, dt, d), *stacked)
```

```python
import functools

import jax
import jax.numpy as jnp
import numpy as np
from jax import lax
from jax.experimental import pallas as pl
from jax.experimental.pallas import tpu as pltpu

F32 = jnp.float32
BF16 = jnp.bfloat16
EPS = 1e-6
GATE_TEMP = 16.0
PAGE = 128
A_HEADS, A_DIM = 8, 64
B_HEADS, B_DK, B_DV = 4, 128, 256
GATE_RANK = 16
LANES = 128
VMEM_LIMIT = 56 * 1024 * 1024
HIGHEST = lax.Precision.HIGHEST
LOG2E = float(np.log2(np.e))
NT = (((1,), (1,)), ((), ()))
TN = (((0,), (0,)), ((), ()))


def _cparams(*sem):
    return pltpu.CompilerParams(dimension_semantics=sem, vmem_limit_bytes=VMEM_LIMIT)


def _resident(shape):
    nd = len(shape)
    return pl.BlockSpec(shape, lambda *_: (0,) * nd, pipeline_mode=pl.Buffered(1))


def _rms(x, g):
    return x * lax.rsqrt(jnp.mean(x * x, axis=-1, keepdims=True) + EPS) * g


def _log_sigmoid(x):
    return jnp.minimum(x, 0.0) - jnp.log1p(jnp.exp(-jnp.abs(x)))


def _dot(a, b):
    return jnp.dot(a, b, preferred_element_type=F32)


def _trunc_to_bf16_bits(x):
    bits = lax.bitcast_convert_type(x, jnp.int32) & jnp.int32(-65536)
    return lax.bitcast_convert_type(bits, F32)


def _split3(x):
    hi = _trunc_to_bf16_bits(x)
    r1 = x - hi
    mid = _trunc_to_bf16_bits(r1)
    lo = r1 - mid
    return hi.astype(BF16), mid.astype(BF16), lo.astype(BF16)


def _dot_sel(x, sel):
    return sum(_dot(part, sel) for part in _split3(x))


def _sel_dot(sel, x):
    return sum(_dot(sel, part) for part in _split3(x))


def _inproj_kernel(x_ref, g_ref, wbig_ref, wsm_ref, bf_ref, wa2_ref, ba_ref,
                   qa_ref, ka_ref, va_ref, kaf_ref, vaf_ref, logf_ref,
                   qb_ref, kb_ref, vb_ref, la_ref):
    hn = _rms(x_ref[...], g_ref[...]).astype(BF16)
    aw = A_HEADS * A_DIM
    kw = B_HEADS * B_DK

    def mm(lo, hi):
        return _dot(hn, wbig_ref[:, lo:hi])

    qa_ref[...] = (mm(0, aw) * (A_DIM ** -0.5 * LOG2E)).astype(BF16)
    k_a = mm(aw, 2 * aw)
    kaf_ref[...] = k_a
    ka_ref[...] = k_a.astype(BF16)
    v_a = mm(2 * aw, 3 * aw)
    vaf_ref[...] = v_a
    va_ref[...] = v_a.astype(BF16)
    o = 3 * aw
    qb_ref[...] = mm(o, o + kw) * B_DK ** -0.5
    kb_ref[...] = mm(o + kw, o + 2 * kw)
    vb_ref[...] = mm(o + 2 * kw, o + 2 * kw + B_HEADS * B_DV)
    us = _dot(hn, wsm_ref[...])
    logf_ref[...] = _log_sigmoid(us[:, :A_HEADS] + bf_ref[...])
    z = _dot(us[:, LANES:].astype(BF16), wa2_ref[...]) + ba_ref[...]
    la_ref[...] = _log_sigmoid(z) / GATE_TEMP


def _inproj(x, g_pre, wbig, wsm, b_f, wa2, b_alpha, tm):
    t, d = x.shape
    aw, kw, vw = A_HEADS * A_DIM, B_HEADS * B_DK, B_HEADS * B_DV
    row = lambda n: pl.BlockSpec((tm, n), lambda i: (i, 0))
    outs = [(aw, BF16)] * 3 + [(aw, F32)] * 2 + [(A_HEADS, F32), (kw, F32), (kw, F32), (vw, F32), (kw, F32)]
    return pl.pallas_call(
        _inproj_kernel,
        grid=(t // tm,),
        in_specs=[row(d), _resident(g_pre.shape), _resident(wbig.shape), _resident(wsm.shape),
                  _resident(b_f.shape), _resident(wa2.shape), _resident(b_alpha.shape)],
        out_specs=[row(n) for n, _ in outs],
        out_shape=[jax.ShapeDtypeStruct((t, n), dt) for n, dt in outs],
        compiler_params=_cparams("arbitrary"),
        name="inproj",
    )(x, g_pre, wbig, wsm, b_f, wa2, b_alpha)


def _cumsum_kernel(x_ref, tri_ref, o_ref, *, seg):
    n = x_ref.shape[-1]
    carry = jnp.zeros((x_ref.shape[1], 1), F32)
    for s in range(n // seg):
        cs = jnp.dot(x_ref[0, :, s * seg:(s + 1) * seg], tri_ref[...],
                     precision=HIGHEST, preferred_element_type=F32) + carry
        o_ref[0, :, s * seg:(s + 1) * seg] = cs
        carry = cs[:, seg - 1:seg]


def _cumsum_lanes(x, seg=256):
    b, h, n = x.shape
    tri = jnp.triu(jnp.ones((seg, seg), F32))
    blk = pl.BlockSpec((1, h, n), lambda i: (i, 0, 0))
    return pl.pallas_call(
        functools.partial(_cumsum_kernel, seg=seg),
        grid=(b,),
        in_specs=[blk, _resident(tri.shape)],
        out_specs=blk,
        out_shape=jax.ShapeDtypeStruct(x.shape, F32),
        compiler_params=_cparams("arbitrary"),
        name="cumsum_logf",
    )(x, tri)


def _fox_prompt_kernel(q_ref, k_ref, v_ref, kb_ref, o_ref, m_sc, l_sc, acc_sc, *, tq, tk):
    i = pl.program_id(2)
    q = q_ref[...]
    lane = lax.broadcasted_iota(jnp.int32, q.shape, 1)
    zero = jnp.zeros_like(q)
    sel_even = jnp.where(lane < 3, 1.0, 0.0).astype(BF16)
    sel_odd = jnp.where(lane < 3, 0.0, jnp.where(lane < 6, 1.0, 0.0)).astype(BF16)
    q_even = jnp.concatenate([jnp.where(lane < A_DIM, q, zero), sel_even], axis=1)
    q_odd = jnp.concatenate([jnp.where(lane >= A_DIM, q, zero), sel_odd], axis=1)
    qs = jnp.concatenate([q_even, q_odd], axis=0)
    m_sc[...] = jnp.full(m_sc.shape, -jnp.inf, F32)
    l_sc[...] = jnp.zeros(l_sc.shape, F32)
    acc_sc[...] = jnp.zeros(acc_sc.shape, F32)
    reps = tk // LANES

    def block(j, masked):
        start = pl.multiple_of(j * tk, tk)
        ka = jnp.concatenate([k_ref[pl.ds(start, tk), :], kb_ref[pl.ds(start, tk), :]], axis=1)
        s = lax.dot_general(qs, ka, NT, preferred_element_type=F32)
        if masked:
            r = lax.broadcasted_iota(jnp.int32, s.shape, 0)
            c = lax.broadcasted_iota(jnp.int32, s.shape, 1)
            qpos = i * tq + jnp.where(r >= tq, r - tq, r)
            s = jnp.where(start + c <= qpos, s, -jnp.inf)
        m_old = m_sc[...]
        m_new = jnp.maximum(m_old, jnp.max(s, axis=-1, keepdims=True))
        alpha = jnp.exp2(m_old - m_new)
        p = jnp.exp2(s - jnp.tile(m_new, (1, reps)))
        l_sc[...] = alpha * l_sc[...] + jnp.sum(p, axis=-1, keepdims=True)
        acc_sc[...] = alpha * acc_sc[...] + _dot(p.astype(BF16), v_ref[pl.ds(start, tk), :])
        m_sc[...] = m_new

    n_full = i * (tq // tk)

    def body(jj, carry):
        block(2 * jj, False)
        block(2 * jj + 1, False)
        return carry

    lax.fori_loop(0, n_full // 2, body, 0)

    @pl.when(n_full % 2 == 1)
    def _():
        block(n_full - 1, False)

    for d in range(tq // tk):
        block(n_full + d, True)
    o = acc_sc[...] / l_sc[...]
    o_ref[...] = jnp.where(lane < A_DIM, o[:tq], o[tq:]).astype(o_ref.dtype)


def _key_bias_lanes(c):
    batch, heads, seq = c.shape
    parts = jnp.stack(_split3(-c * LOG2E), axis=-1).reshape(batch, heads // 2, 2, seq, 3)
    parts = parts.transpose(0, 3, 1, 2, 4).reshape(batch, seq, heads // 2, 6)
    parts = jnp.pad(parts, ((0, 0), (0, 0), (0, 0), (0, LANES - 6)))
    return parts.reshape(batch * seq, heads // 2 * LANES)


def _fox_prompt(qa, ka, va, kbias, batch, seq, tq, tk):
    assert tq % tk == 0
    nq = seq // tq
    pairs = A_HEADS // 2
    kv_spec = pl.BlockSpec((seq, LANES), lambda b, p, i: (b, p))
    q_spec = pl.BlockSpec((tq, LANES), lambda b, p, i: (b * nq + i, p))
    return pl.pallas_call(
        functools.partial(_fox_prompt_kernel, tq=tq, tk=tk),
        grid=(batch, pairs, nq),
        in_specs=[q_spec, kv_spec, kv_spec, kv_spec],
        out_specs=q_spec,
        out_shape=jax.ShapeDtypeStruct(qa.shape, BF16),
        scratch_shapes=[pltpu.VMEM((2 * tq, LANES), F32), pltpu.VMEM((2 * tq, LANES), F32),
                        pltpu.VMEM((2 * tq, LANES), F32)],
        compiler_params=_cparams("arbitrary", "arbitrary", "arbitrary"),
        name="fox_prompt",
    )(qa, ka, va, kbias)


def _fox_sample_kernel(pt_ref, q_ref, kn_ref, vn_ref, lfn_ref, exp_ref, tri_ref, sfx_ref, *rest, pages):
    page_refs = rest[:3 * pages]
    o_ref, m_sc, l_sc, acc_sc, carry_sc = rest[3 * pages:]
    g = pl.program_id(1)

    def head_rows(ref, h):
        flat = ref.reshape(PAGE * A_HEADS, A_DIM)
        return flat[pl.ds(h, PAGE, stride=A_HEADS), :].astype(BF16)

    def scores(k_ref):
        s = _dot(head_rows(k_ref, 0), q_ref[0, 0])
        for h in range(1, A_HEADS):
            s = s + _dot(head_rows(k_ref, h), q_ref[0, h])
        return s

    def update(s_list, v_refs):
        m_old = m_sc[...]
        m_tile = s_list[0]
        for s in s_list[1:]:
            m_tile = jnp.maximum(m_tile, s)
        m_new = jnp.maximum(m_old, jnp.max(m_tile, axis=0, keepdims=True))
        alpha = jnp.exp2(m_old - m_new)
        ps = [jnp.exp2(s - m_new) for s in s_list]
        l_tile = ps[0]
        for p in ps[1:]:
            l_tile = l_tile + p
        l_sc[...] = alpha * l_sc[...] + jnp.sum(l_tile, axis=0, keepdims=True)
        pb = [p.astype(BF16) for p in ps]
        for h in range(A_HEADS):
            pv = None
            for p, v_ref in zip(pb, v_refs):
                d = lax.dot_general(head_rows(v_ref, h), p, TN, preferred_element_type=F32)
                pv = d if pv is None else pv + d
            acc_sc[h] = alpha * acc_sc[h] + pv
        m_sc[...] = m_new

    @pl.when(g == 0)
    def _():
        m_sc[...] = jnp.full(m_sc.shape, -jnp.inf, F32)
        l_sc[...] = jnp.zeros(l_sc.shape, F32)
        acc_sc[...] = jnp.zeros(acc_sc.shape, F32)
        carry_sc[...] = jnp.zeros(carry_sc.shape, F32)
        cn = _sel_dot(tri_ref[...], _dot_sel(lfn_ref[0, 0], exp_ref[...]))
        s = scores(kn_ref) - cn * LOG2E
        r = lax.broadcasted_iota(jnp.int32, s.shape, 0)
        c = lax.broadcasted_iota(jnp.int32, s.shape, 1)
        s = jnp.where(r <= (c & 7), s, -jnp.inf)
        update([s], [vn_ref])

    carry = carry_sc[...]
    order = list(reversed(range(pages)))
    s_list = []
    for p in order:
        sfx = _sel_dot(sfx_ref[...], _dot_sel(page_refs[2 * pages + p][0, 0], exp_ref[...]))
        bias = sfx[:PAGE] + carry
        carry = carry + sfx[PAGE:PAGE + 1]
        s_list.append(scores(page_refs[p]) + bias * LOG2E)
    update(s_list, [page_refs[pages + p] for p in order])
    carry_sc[...] = carry

    @pl.when(g == pl.num_programs(1) - 1)
    def _():
        o_ref[0] = acc_sc[...] / l_sc[...]


def _fox_sample(page_table, qbd, kn, vn, lfn, cache_k, cache_v, cache_lf, layer, pages=8):
    db, n_pages = page_table.shape
    groups = n_pages // pages
    cols = qbd.shape[-1]
    expand = jnp.repeat(jnp.eye(A_HEADS, dtype=BF16), cols // A_HEADS, axis=1)
    idx = np.arange(PAGE)
    tri = jnp.asarray(idx[:, None] >= idx[None, :], BF16)
    sfx = jnp.asarray(np.concatenate([idx[:, None] < idx[None, :], np.ones((PAGE, PAGE), bool)], 0), BF16)
    kv_page = (PAGE, A_HEADS, A_DIM)
    lf_page = (PAGE, A_HEADS)

    def per_b(shape):
        nd = len(shape)
        return pl.BlockSpec((1,) + shape, lambda b, g, pt: (b,) + (0,) * nd)

    def new_page(shape):
        nd = len(shape)
        return pl.BlockSpec((1, 1) + shape, lambda b, g, pt: (0, b) + (0,) * nd)

    def const(shape):
        nd = len(shape)
        return pl.BlockSpec(shape, lambda b, g, pt: (0,) * nd)

    def page_spec(shape, p):
        nd = len(shape)
        return pl.BlockSpec(
            (1, 1) + shape,
            lambda b, g, pt: (layer, pt[b * n_pages + (groups - 1 - g) * pages + p]) + (0,) * nd)

    in_specs = [per_b(qbd.shape[1:]), new_page(kv_page), new_page(kv_page), new_page(lf_page),
                const(expand.shape), const(tri.shape), const(sfx.shape)]
    in_specs += [page_spec(kv_page, p) for p in range(pages)]
    in_specs += [page_spec(kv_page, p) for p in range(pages)]
    in_specs += [page_spec(lf_page, p) for p in range(pages)]
    grid_spec = pltpu.PrefetchScalarGridSpec(
        num_scalar_prefetch=1,
        grid=(db, groups),
        in_specs=in_specs,
        out_specs=pl.BlockSpec((1, A_HEADS, A_DIM, cols), lambda b, g, pt: (b, 0, 0, 0)),
        scratch_shapes=[pltpu.VMEM((1, cols), F32), pltpu.VMEM((1, cols), F32),
                        pltpu.VMEM((A_HEADS, A_DIM, cols), F32), pltpu.VMEM((1, cols), F32)],
    )
    return pl.pallas_call(
        functools.partial(_fox_sample_kernel, pages=pages),
        grid_spec=grid_spec,
        out_shape=jax.ShapeDtypeStruct((db, A_HEADS, A_DIM, cols), F32),
        compiler_params=_cparams("arbitrary", "arbitrary"),
        name="fox_sample",
    )(page_table.reshape(-1), qbd, kn, vn, lfn, expand, tri, sfx,
      *([cache_k] * pages), *([cache_v] * pages), *([cache_lf] * pages))


def _gla_chunk(q, k, v, la, s_old, tril, row0, ones, cast):
    c = q.shape[0]
    bc = jnp.dot(tril, la, precision=HIGHEST, preferred_element_type=F32)
    ref = bc[c // 2 - 1:c // 2, :]
    last = bc[c - 1:c, :]
    qt = q * jnp.exp(bc - ref)
    kt = k * jnp.exp(ref - bc)
    o = _dot(cast(qt * jnp.exp(ref)), cast(s_old))
    a = lax.dot_general(cast(qt), cast(kt), NT, preferred_element_type=F32)
    a = jnp.where(tril > 0, a, 0.0)
    vc = cast(v)
    o = o + _dot(cast(a), vc)
    kd = cast(kt * jnp.exp(last - ref))
    scale = lax.dot_general(row0 * jnp.exp(last), ones, TN, precision=HIGHEST, preferred_element_type=F32)
    s_new = scale * s_old + lax.dot_general(kd, vc, TN, preferred_element_type=F32)
    return o, s_new


def _gla_prompt_kernel(q_ref, k_ref, v_ref, la_ref, tril_ref, o_ref, st_ref, *, chunk):
    @pl.when(pl.program_id(1) == 0)
    def _():
        st_ref[...] = jnp.zeros(st_ref.shape, F32)

    tril = tril_ref[...]
    r = lax.broadcasted_iota(jnp.int32, tril.shape, 0)
    c = lax.broadcasted_iota(jnp.int32, tril.shape, 1)
    causal = c <= r

    def body(ci, carry):
        rows = pl.ds(pl.multiple_of(ci * chunk, chunk), chunk)
        bc = _sel_dot(tril, la_ref[rows, :])
        ref = bc[chunk // 2 - 1:chunk // 2, :]
        last = bc[chunk - 1:chunk, :]
        qt = q_ref[rows, :] * jnp.exp(bc - ref)
        kt = k_ref[rows, :] * jnp.exp(ref - bc)
        qs = (qt * jnp.exp(ref)).astype(BF16)
        kd = (kt * jnp.exp(last - ref)).astype(BF16)
        decay = jnp.exp(last)
        qtb = qt.astype(BF16)
        ktb = kt.astype(BF16)
        for h in range(B_HEADS):
            kc = slice(h * B_DK, (h + 1) * B_DK)
            vc = slice(h * B_DV, (h + 1) * B_DV)
            st = st_ref[0, h]
            vb = v_ref[rows, vc].astype(BF16)
            o = lax.dot_general(qs[:, kc], st.astype(BF16), NT, preferred_element_type=F32)
            a = lax.dot_general(qtb[:, kc], ktb[:, kc], NT, preferred_element_type=F32)
            a = jnp.where(causal, a, 0.0).astype(BF16)
            o_ref[rows, vc] = o + _dot(a, vb)
            st_ref[0, h] = st * decay[:, kc] + lax.dot_general(vb, kd[:, kc], TN, preferred_element_type=F32)
        return carry

    lax.fori_loop(0, q_ref.shape[0] // chunk, body, 0, unroll=2)


def _gla_consts(chunk):
    tril = jnp.tril(jnp.ones((chunk, chunk), F32))
    row0 = jnp.zeros((8, B_DK), F32).at[0].set(1.0)
    ones = jnp.ones((8, B_DV), F32)
    return tril, row0, ones


def _gla_prompt(qb, kb, vb, la, batch, seq, rows, chunk=64):
    steps = seq // rows
    kw, vw = B_HEADS * B_DK, B_HEADS * B_DV
    tril = jnp.tril(jnp.ones((chunk, chunk), BF16))
    rk = pl.BlockSpec((rows, kw), lambda b, i: (b * steps + i, 0))
    rv = pl.BlockSpec((rows, vw), lambda b, i: (b * steps + i, 0))
    return pl.pallas_call(
        functools.partial(_gla_prompt_kernel, chunk=chunk),
        grid=(batch, steps),
        in_specs=[rk, rk, rv, rk, _resident(tril.shape)],
        out_specs=[rv, pl.BlockSpec((1, B_HEADS, B_DV, B_DK), lambda b, i: (b, 0, 0, 0))],
        out_shape=[jax.ShapeDtypeStruct(vb.shape, F32),
                   jax.ShapeDtypeStruct((batch, B_HEADS, B_DV, B_DK), F32)],
        compiler_params=_cparams("arbitrary", "arbitrary"),
        name="gla_prompt",
    )(qb, kb, vb, la, tril)


def _gla_sample_kernel(q_ref, k_ref, v_ref, la_ref, s_in_ref, tril_ref, row0_ref, ones_ref, o_ref, s_ref):
    cast = lambda a: a
    tril, row0, ones = tril_ref[...], row0_ref[...], ones_ref[...]
    for h in range(B_HEADS):
        kc = slice(h * B_DK, (h + 1) * B_DK)
        vc = slice(h * B_DV, (h + 1) * B_DV)
        o, s_new = _gla_chunk(q_ref[:, kc], k_ref[:, kc], v_ref[:, vc], la_ref[:, kc],
                              s_in_ref[0, h], tril, row0, ones, cast)
        o_ref[:, vc] = o
        s_ref[0, h] = s_new


def _gla_sample(qb, kb, vb, la, state, db, t):
    kw, vw = B_HEADS * B_DK, B_HEADS * B_DV
    consts = _gla_consts(t)
    rk = pl.BlockSpec((t, kw), lambda b: (b, 0))
    rv = pl.BlockSpec((t, vw), lambda b: (b, 0))
    st = pl.BlockSpec((1, B_HEADS, B_DK, B_DV), lambda b: (b, 0, 0, 0))
    return pl.pallas_call(
        _gla_sample_kernel,
        grid=(db,),
        in_specs=[rk, rk, rv, rk, st] + [_resident(c.shape) for c in consts],
        out_specs=[rv, st],
        out_shape=[jax.ShapeDtypeStruct(vb.shape, F32), jax.ShapeDtypeStruct(state.shape, F32)],
        compiler_params=_cparams("arbitrary"),
        name="gla_sample",
    )(qb, kb, vb, la, state, *consts)


def _mix_kernel(x_ref, oa_ref, ob_ref, gpre_ref, wr_ref, wga_ref, wgb_ref, ggla_ref,
                wpa_ref, wpb_ref, wout_ref, gpost_ref, y_ref):
    x = x_ref[...]
    hn = _rms(x, gpre_ref[...]).astype(BF16)
    r_b = _dot(hn, wr_ref[...])
    ob = ob_ref[...]
    parts = []
    for h in range(B_HEADS):
        seg = ob[:, h * B_DV:(h + 1) * B_DV]
        parts.append(_rms(seg, ggla_ref[...]))
    obn = jnp.concatenate(parts, axis=-1) * (r_b * jax.nn.sigmoid(r_b))
    pa = _dot(oa_ref[...], wpa_ref[...])
    pb = _dot(obn.astype(BF16), wpb_ref[...])
    merged = jax.nn.sigmoid(_dot(hn, wga_ref[...])) * pa + jax.nn.sigmoid(_dot(hn, wgb_ref[...])) * pb
    z = _dot(merged.astype(BF16), wout_ref[...])
    y_ref[...] = x + _rms(z, gpost_ref[...])


def _mix(x, oa, ob, g_pre, w_r, w_ga, w_gb, g_gla, w_pa, w_pb, w_out, g_post, tm):
    t, d = x.shape
    row = lambda n: pl.BlockSpec((tm, n), lambda i: (i, 0))
    consts = (g_pre, w_r, w_ga, w_gb, g_gla, w_pa, w_pb, w_out, g_post)
    return pl.pallas_call(
        _mix_kernel,
        grid=(t // tm,),
        in_specs=[row(d), row(oa.shape[1]), row(ob.shape[1])] + [_resident(c.shape) for c in consts],
        out_specs=row(d),
        out_shape=jax.ShapeDtypeStruct(x.shape, F32),
        compiler_params=_cparams("arbitrary"),
        name="mix",
    )(x, oa, ob, *consts)


def _gelu_tanh(g):
    return 0.5 * g * (1.0 + jnp.tanh(np.sqrt(2.0 / np.pi) * (g + 0.044715 * (g * g * g))))


def _ffn_body(x_ref, gpre_ref, wup_ref, wconv_ref, bconv_ref, wdown_ref, gpost_ref, y_ref, tail_ref,
              acc_sc, shifted):
    x = x_ref[...]
    hn = _rms(x, gpre_ref[...]).astype(BF16)
    n_chunks = wdown_ref.shape[0]
    for j in range(n_chunks):
        halves = []
        for half in range(2):
            idx = half * n_chunks + j
            u = _dot(hn, wup_ref[idx])
            u1, u2 = shifted(idx, u)
            w = wconv_ref[idx]
            halves.append(bconv_ref[idx] + u2 * w[0:1] + u1 * w[1:2] + u * w[2:3])
            tail_ref[idx] = u[u.shape[0] - 8:, :]
        h = (_gelu_tanh(halves[1]) * halves[0]).astype(BF16)
        d = _dot(h, wdown_ref[j])
        if j == 0:
            acc_sc[...] = d
        else:
            acc_sc[...] += d
    y_ref[...] = x + _rms(acc_sc[...], gpost_ref[...])


def _ffn_prompt_kernel(x_ref, gpre_ref, wup_ref, wconv_ref, bconv_ref, wdown_ref, gpost_ref,
                       y_ref, tail_ref, acc_sc, hist_sc, *, tiles_per_seq):
    first = pl.program_id(0) % tiles_per_seq == 0

    @pl.when(first)
    def _():
        hist_sc[...] = jnp.zeros(hist_sc.shape, F32)

    def shifted(idx, u):
        rows = lax.broadcasted_iota(jnp.int32, u.shape, 0)
        p1 = hist_sc[idx, 7:8, :]
        p2 = hist_sc[idx, 6:7, :]
        u1 = jnp.where(rows == 0, p1, pltpu.roll(u, 1, axis=0))
        u2 = jnp.where(rows == 0, p2, jnp.where(rows == 1, p1, pltpu.roll(u, 2, axis=0)))
        hist_sc[idx] = u[u.shape[0] - 8:, :]
        return u1, u2

    _ffn_body(x_ref, gpre_ref, wup_ref, wconv_ref, bconv_ref, wdown_ref, gpost_ref, y_ref,
              tail_ref.at[0], acc_sc, shifted)


def _ffn_sample_kernel(x_ref, h1_ref, h2_ref, gpre_ref, wup_ref, wconv_ref, bconv_ref, wdown_ref,
                       gpost_ref, y_ref, u_ref, acc_sc, *, t):
    x = x_ref[...]
    hn = _rms(x, gpre_ref[...]).astype(BF16)
    n_chunks = wdown_ref.shape[0]
    for j in range(n_chunks):
        halves = []
        for half in range(2):
            idx = half * n_chunks + j
            u = _dot(hn, wup_ref[idx])
            pos = lax.broadcasted_iota(jnp.int32, u.shape, 0) & (t - 1)
            u1 = jnp.where(pos < 1, h1_ref[idx], pltpu.roll(u, 1, axis=0))
            u2 = jnp.where(pos < 2, h2_ref[idx], pltpu.roll(u, 2, axis=0))
            w = wconv_ref[idx]
            halves.append(bconv_ref[idx] + u2 * w[0:1] + u1 * w[1:2] + u * w[2:3])
            u_ref[idx] = u
        h = (_gelu_tanh(halves[1]) * halves[0]).astype(BF16)
        d = _dot(h, wdown_ref[j])
        if j == 0:
            acc_sc[...] = d
        else:
            acc_sc[...] += d
    y_ref[...] = x + _rms(acc_sc[...], gpost_ref[...])


def _ffn_prompt(x, g_pre, wup, wconv, bconv, wdown, g_post, seq, tm):
    t, d = x.shape
    nblk, _, nc = wup.shape
    tiles = t // tm
    row = pl.BlockSpec((tm, d), lambda i: (i, 0))
    consts = (g_pre, wup, wconv, bconv, wdown, g_post)
    return pl.pallas_call(
        functools.partial(_ffn_prompt_kernel, tiles_per_seq=seq // tm),
        grid=(tiles,),
        in_specs=[row] + [_resident(c.shape) for c in consts],
        out_specs=[row, pl.BlockSpec((1, nblk, 8, nc), lambda i: (i, 0, 0, 0))],
        out_shape=[jax.ShapeDtypeStruct(x.shape, F32), jax.ShapeDtypeStruct((tiles, nblk, 8, nc), F32)],
        scratch_shapes=[pltpu.VMEM((tm, d), F32), pltpu.VMEM((nblk, 8, nc), F32)],
        compiler_params=_cparams("arbitrary"),
        name="ffn_prompt",
    )(x, *consts)


def _ffn_sample(x, h1, h2, g_pre, wup, wconv, bconv, wdown, g_post, t):
    rows, d = x.shape
    nblk, _, nc = wup.shape
    consts = (g_pre, wup, wconv, bconv, wdown, g_post)
    return pl.pallas_call(
        functools.partial(_ffn_sample_kernel, t=t),
        grid=(1,),
        in_specs=[_resident(a.shape) for a in (x, h1, h2) + consts],
        out_specs=[pl.BlockSpec(x.shape, lambda i: (0, 0)),
                   pl.BlockSpec((nblk, rows, nc), lambda i: (0, 0, 0))],
        out_shape=[jax.ShapeDtypeStruct(x.shape, F32), jax.ShapeDtypeStruct((nblk, rows, nc), F32)],
        scratch_shapes=[pltpu.VMEM((rows, d), F32)],
        compiler_params=_cparams("arbitrary"),
        name="ffn_sample",
    )(x, h1, h2, *consts)


def _pack_layer(w_in, w_alpha2, w_up, w_conv, b_conv, w_down, nc):
    aw, kw, vw = A_HEADS * A_DIM, B_HEADS * B_DK, B_HEADS * B_DV
    d = w_in.shape[0]
    offs = np.cumsum([aw, aw, aw, A_HEADS, kw, kw, vw, GATE_RANK, vw, d, d])[:-1].tolist()
    q_a, k_a, v_a, f_a, q_b, k_b, v_b, a_low, r_b, g_a, g_b = jnp.split(w_in, offs, axis=1)
    wbig = jnp.concatenate([q_a, k_a, v_a, q_b, k_b, v_b], axis=1).astype(BF16)
    pad = lambda a: jnp.pad(a, ((0, 0), (0, LANES - a.shape[1])))
    wsm = jnp.concatenate([pad(f_a), pad(a_low)], axis=1).astype(BF16)
    wa2 = jnp.pad(w_alpha2, ((0, LANES - GATE_RANK), (0, 0))).astype(BF16)
    dff = w_down.shape[0]
    n_chunks = dff // nc
    wup = w_up.reshape(d, 2 * n_chunks, nc).transpose(1, 0, 2).astype(BF16)
    wconv = w_conv.reshape(w_conv.shape[0], 2 * n_chunks, nc).transpose(1, 0, 2)
    bconv = b_conv.reshape(2 * n_chunks, 1, nc)
    wdown = w_down.reshape(n_chunks, nc, d).astype(BF16)
    return dict(wbig=wbig, wsm=wsm, wa2=wa2, w_r=r_b.astype(BF16), w_ga=g_a.astype(BF16),
                w_gb=g_b.astype(BF16), wup=wup, wconv=wconv, bconv=bconv, wdown=wdown)


def kernel(x_prompt, x_sample, cache_k, cache_v, cache_logf, state_gla, state_conv, page_table,
           g_pre_mix, w_in, b_f, w_alpha2, b_alpha, g_gla, w_proj_a, w_proj_b, w_out, g_post_mix,
           g_pre_ffn, w_up, w_conv, b_conv, w_down, g_post_ffn):
    batch, seq, d = x_prompt.shape
    db, dt, _ = x_sample.shape
    depth = w_in.shape[0]
    n_pool = cache_k.shape[1]
    aw = A_HEADS * A_DIM
    dff2 = w_up.shape[2]
    nc = 256
    assert dt == 8 and depth >= 1

    yp = x_prompt.reshape(batch * seq, d)
    ys = x_sample.reshape(db * dt, d)
    outs = [[] for _ in range(10)]
    for l in range(depth):
        w = _pack_layer(w_in[l], w_alpha2[l], w_up[l], w_conv[l], b_conv[l], w_down[l], nc)
        row = lambda a: a[l].reshape(1, -1)
        inproj_w = (row(g_pre_mix), w["wbig"], w["wsm"], row(b_f), w["wa2"], row(b_alpha))
        mix_w = (row(g_pre_mix), w["w_r"], w["w_ga"], w["w_gb"], row(g_gla), w_proj_a[l].astype(BF16),
                 w_proj_b[l].astype(BF16), w_out[l].astype(BF16), row(g_post_mix))
        ffn_w = (row(g_pre_ffn), w["wup"], w["wconv"], w["bconv"], w["wdown"], row(g_post_ffn))

        qa, ka, va, kaf, vaf, logf, qb, kb, vb, la = _inproj(yp, *inproj_w, tm=256)
        lft = logf.reshape(batch, seq, A_HEADS).transpose(0, 2, 1)
        oa = _fox_prompt(qa, ka, va, _key_bias_lanes(_cumsum_lanes(lft)), batch, seq, tq=512, tk=512)
        ob, gla_state_t = _gla_prompt(qb, kb, vb, la, batch, seq, rows=512)
        gla_state = gla_state_t.transpose(0, 1, 3, 2)
        x1 = _mix(yp, oa, ob, *mix_w, tm=512)
        yp, tails = _ffn_prompt(x1, *ffn_w, seq=seq, tm=512)
        tiles_per_seq = seq // 512
        tails = tails.reshape(batch, tiles_per_seq, 2 * (dff2 // 2 // nc), 8, nc)[:, -1, :, 6:, :]
        conv_p = tails.transpose(0, 2, 1, 3).reshape(batch, 2, dff2)
        for lst, val in zip(outs[:5], (kaf.reshape(batch, seq, A_HEADS, A_DIM),
                                       vaf.reshape(batch, seq, A_HEADS, A_DIM),
                                       logf.reshape(batch, seq, A_HEADS), gla_state, conv_p)):
            lst.append(val)

        qa, ka, va, kaf, vaf, logf, qb, kb, vb, la = _inproj(ys, *inproj_w, tm=db * dt)
        q4 = qa.reshape(db, dt, A_HEADS, A_DIM)
        qbd = jnp.einsum("bthd,hg->bhdgt", q4, jnp.eye(A_HEADS, dtype=BF16)).reshape(
            db, A_HEADS, A_DIM, A_HEADS * dt)
        padk = lambda a: jnp.pad(a.reshape(1, db, dt, A_HEADS, A_DIM),
                                 ((0, 0), (0, 0), (0, PAGE - dt), (0, 0), (0, 0)))
        lfn = jnp.pad(logf.reshape(1, db, dt, A_HEADS), ((0, 0), (0, 0), (0, PAGE - dt), (0, 0)))
        o_full = _fox_sample(page_table, qbd, padk(kaf), padk(vaf), lfn, cache_k, cache_v, cache_logf, layer=l)
        o5 = o_full.reshape(db, A_HEADS, A_DIM, A_HEADS, dt)
        oa = jnp.einsum("bgdht,gh->bthd", o5, jnp.eye(A_HEADS, dtype=F32)).reshape(db * dt, aw).astype(BF16)
        ob, gla_state = _gla_sample(qb, kb, vb, la, state_gla[l], db, dt)
        x1 = _mix(ys, oa, ob, *mix_w, tm=db * dt)
        st = state_conv[l].astype(F32)
        zeros = jnp.zeros((db, dt, dff2), F32)
        h1 = zeros.at[:, 0].set(st[:, 1])
        h2 = zeros.at[:, 0].set(st[:, 0]).at[:, 1].set(st[:, 1])
        chunked = lambda a: a.reshape(db * dt, dff2 // nc, nc).transpose(1, 0, 2)
        ys, u = _ffn_sample(x1, chunked(h1), chunked(h2), *ffn_w, t=dt)
        conv_s = u.transpose(1, 0, 2).reshape(db, dt, dff2)[:, dt - 2:, :]
        for lst, val in zip(outs[5:], (kaf.reshape(db, dt, A_HEADS, A_DIM),
                                       vaf.reshape(db, dt, A_HEADS, A_DIM),
                                       logf.reshape(db, dt, A_HEADS), gla_state, conv_s)):
            lst.append(val)

    stacked = [jnp.stack(o) for o in outs]
    return (yp.reshape(batch, seq, d), ys.reshape(db, dt, d), *stacked)
```

```python
import functools

import jax
import jax.numpy as jnp
import numpy as np
from jax import lax
from jax.experimental import pallas as pl
from jax.experimental.pallas import tpu as pltpu

F32 = jnp.float32
BF16 = jnp.bfloat16
EPS = 1e-6
GATE_TEMP = 16.0
PAGE = 128
A_HEADS, A_DIM = 8, 64
B_HEADS, B_DK, B_DV = 4, 128, 256
GATE_RANK = 16
LANES = 128
VMEM_LIMIT = 56 * 1024 * 1024
HIGHEST = lax.Precision.HIGHEST
LOG2E = float(np.log2(np.e))
NT = (((1,), (1,)), ((), ()))
TN = (((0,), (0,)), ((), ()))


def _cparams(*sem):
    return pltpu.CompilerParams(dimension_semantics=sem, vmem_limit_bytes=VMEM_LIMIT)


def _resident(shape):
    nd = len(shape)
    return pl.BlockSpec(shape, lambda *_: (0,) * nd, pipeline_mode=pl.Buffered(1))


def _rms(x, g):
    return x * lax.rsqrt(jnp.mean(x * x, axis=-1, keepdims=True) + EPS) * g


def _log_sigmoid(x):
    return jnp.minimum(x, 0.0) - jnp.log1p(jnp.exp(-jnp.abs(x)))


def _dot(a, b):
    return jnp.dot(a, b, preferred_element_type=F32)


def _trunc_to_bf16_bits(x):
    bits = lax.bitcast_convert_type(x, jnp.int32) & jnp.int32(-65536)
    return lax.bitcast_convert_type(bits, F32)


def _split3(x):
    hi = _trunc_to_bf16_bits(x)
    r1 = x - hi
    mid = _trunc_to_bf16_bits(r1)
    lo = r1 - mid
    return hi.astype(BF16), mid.astype(BF16), lo.astype(BF16)


def _dot_sel(x, sel):
    return sum(_dot(part, sel) for part in _split3(x))


def _sel_dot(sel, x):
    return sum(_dot(sel, part) for part in _split3(x))


def _inproj_kernel(x_ref, g_ref, wbig_ref, wsm_ref, bf_ref, wa2_ref, ba_ref,
                   qa_ref, ka_ref, va_ref, kaf_ref, vaf_ref, logf_ref,
                   qb_ref, kb_ref, vb_ref, la_ref):
    hn = _rms(x_ref[...], g_ref[...]).astype(BF16)
    aw = A_HEADS * A_DIM
    kw = B_HEADS * B_DK

    def mm(lo, hi):
        return _dot(hn, wbig_ref[:, lo:hi])

    qa_ref[...] = (mm(0, aw) * (A_DIM ** -0.5 * LOG2E)).astype(BF16)
    k_a = mm(aw, 2 * aw)
    kaf_ref[...] = k_a
    ka_ref[...] = k_a.astype(BF16)
    v_a = mm(2 * aw, 3 * aw)
    vaf_ref[...] = v_a
    va_ref[...] = v_a.astype(BF16)
    o = 3 * aw
    qb_ref[...] = mm(o, o + kw) * B_DK ** -0.5
    kb_ref[...] = mm(o + kw, o + 2 * kw)
    vb_ref[...] = mm(o + 2 * kw, o + 2 * kw + B_HEADS * B_DV)
    us = _dot(hn, wsm_ref[...])
    logf_ref[...] = _log_sigmoid(us[:, :A_HEADS] + bf_ref[...])
    z = _dot(us[:, LANES:].astype(BF16), wa2_ref[...]) + ba_ref[...]
    la_ref[...] = _log_sigmoid(z) / GATE_TEMP


def _inproj(x, g_pre, wbig, wsm, b_f, wa2, b_alpha, tm):
    t, d = x.shape
    aw, kw, vw = A_HEADS * A_DIM, B_HEADS * B_DK, B_HEADS * B_DV
    row = lambda n: pl.BlockSpec((tm, n), lambda i: (i, 0))
    outs = [(aw, BF16)] * 3 + [(aw, F32)] * 2 + [(A_HEADS, F32), (kw, F32), (kw, F32), (vw, F32), (kw, F32)]
    return pl.pallas_call(
        _inproj_kernel,
        grid=(t // tm,),
        in_specs=[row(d), _resident(g_pre.shape), _resident(wbig.shape), _resident(wsm.shape),
                  _resident(b_f.shape), _resident(wa2.shape), _resident(b_alpha.shape)],
        out_specs=[row(n) for n, _ in outs],
        out_shape=[jax.ShapeDtypeStruct((t, n), dt) for n, dt in outs],
        compiler_params=_cparams("arbitrary"),
        name="inproj",
    )(x, g_pre, wbig, wsm, b_f, wa2, b_alpha)


def _cumsum_kernel(x_ref, tri_ref, o_ref, *, seg):
    n = x_ref.shape[-1]
    carry = jnp.zeros((x_ref.shape[1], 1), F32)
    for s in range(n // seg):
        cs = jnp.dot(x_ref[0, :, s * seg:(s + 1) * seg], tri_ref[...],
                     precision=HIGHEST, preferred_element_type=F32) + carry
        o_ref[0, :, s * seg:(s + 1) * seg] = cs
        carry = cs[:, seg - 1:seg]


def _cumsum_lanes(x, seg=256):
    b, h, n = x.shape
    tri = jnp.triu(jnp.ones((seg, seg), F32))
    blk = pl.BlockSpec((1, h, n), lambda i: (i, 0, 0))
    return pl.pallas_call(
        functools.partial(_cumsum_kernel, seg=seg),
        grid=(b,),
        in_specs=[blk, _resident(tri.shape)],
        out_specs=blk,
        out_shape=jax.ShapeDtypeStruct(x.shape, F32),
        compiler_params=_cparams("arbitrary"),
        name="cumsum_logf",
    )(x, tri)


def _fox_prompt_kernel(q_ref, k_ref, v_ref, kb_ref, o_ref, m_sc, l_sc, acc_sc, *, tq, tk):
    i = pl.program_id(2)
    q = q_ref[...]
    lane = lax.broadcasted_iota(jnp.int32, q.shape, 1)
    zero = jnp.zeros_like(q)
    sel_even = jnp.where(lane < 3, 1.0, 0.0).astype(BF16)
    sel_odd = jnp.where(lane < 3, 0.0, jnp.where(lane < 6, 1.0, 0.0)).astype(BF16)
    q_even = jnp.concatenate([jnp.where(lane < A_DIM, q, zero), sel_even], axis=1)
    q_odd = jnp.concatenate([jnp.where(lane >= A_DIM, q, zero), sel_odd], axis=1)
    qs = jnp.concatenate([q_even, q_odd], axis=0)
    m_sc[...] = jnp.full(m_sc.shape, -jnp.inf, F32)
    l_sc[...] = jnp.zeros(l_sc.shape, F32)
    acc_sc[...] = jnp.zeros(acc_sc.shape, F32)
    reps = tk // LANES

    def block(j, masked):
        start = pl.multiple_of(j * tk, tk)
        ka = jnp.concatenate([k_ref[pl.ds(start, tk), :], kb_ref[pl.ds(start, tk), :]], axis=1)
        s = lax.dot_general(qs, ka, NT, preferred_element_type=F32)
        if masked:
            r = lax.broadcasted_iota(jnp.int32, s.shape, 0)
            c = lax.broadcasted_iota(jnp.int32, s.shape, 1)
            qpos = i * tq + jnp.where(r >= tq, r - tq, r)
            s = jnp.where(start + c <= qpos, s, -jnp.inf)
        m_old = m_sc[...]
        m_new = jnp.maximum(m_old, jnp.max(s, axis=-1, keepdims=True))
        alpha = jnp.exp2(m_old - m_new)
        p = jnp.exp2(s - jnp.tile(m_new, (1, reps)))
        l_sc[...] = alpha * l_sc[...] + jnp.sum(p, axis=-1, keepdims=True)
        acc_sc[...] = alpha * acc_sc[...] + _dot(p.astype(BF16), v_ref[pl.ds(start, tk), :])
        m_sc[...] = m_new

    n_full = i * (tq // tk)

    def body(jj, carry):
        block(2 * jj, False)
        block(2 * jj + 1, False)
        return carry

    lax.fori_loop(0, n_full // 2, body, 0)

    @pl.when(n_full % 2 == 1)
    def _():
        block(n_full - 1, False)

    for d in range(tq // tk):
        block(n_full + d, True)
    o = acc_sc[...] / l_sc[...]
    o_ref[...] = jnp.where(lane < A_DIM, o[:tq], o[tq:]).astype(o_ref.dtype)


def _key_bias_lanes(c):
    batch, heads, seq = c.shape
    parts = jnp.stack(_split3(-c * LOG2E), axis=-1).reshape(batch, heads // 2, 2, seq, 3)
    parts = parts.transpose(0, 3, 1, 2, 4).reshape(batch, seq, heads // 2, 6)
    parts = jnp.pad(parts, ((0, 0), (0, 0), (0, 0), (0, LANES - 6)))
    return parts.reshape(batch * seq, heads // 2 * LANES)


def _fox_prompt(qa, ka, va, kbias, batch, seq, tq, tk):
    assert tq % tk == 0
    nq = seq // tq
    pairs = A_HEADS // 2
    kv_spec = pl.BlockSpec((seq, LANES), lambda b, p, i: (b, p))
    q_spec = pl.BlockSpec((tq, LANES), lambda b, p, i: (b * nq + i, p))
    return pl.pallas_call(
        functools.partial(_fox_prompt_kernel, tq=tq, tk=tk),
        grid=(batch, pairs, nq),
        in_specs=[q_spec, kv_spec, kv_spec, kv_spec],
        out_specs=q_spec,
        out_shape=jax.ShapeDtypeStruct(qa.shape, BF16),
        scratch_shapes=[pltpu.VMEM((2 * tq, LANES), F32), pltpu.VMEM((2 * tq, LANES), F32),
                        pltpu.VMEM((2 * tq, LANES), F32)],
        compiler_params=_cparams("arbitrary", "arbitrary", "arbitrary"),
        name="fox_prompt",
    )(qa, ka, va, kbias)


def _fox_sample_kernel(pt_ref, q_ref, kn_ref, vn_ref, lfn_ref, tri_ref, sfx_ref, *rest, pages):
    page_refs = rest[:3 * pages]
    o_ref, m_sc, l_sc, acc_sc, carry_sc = rest[3 * pages:]
    g = pl.program_id(1)
    nq = q_ref.shape[2] // 2

    def tile(ref, h):
        return ref[0, 0, h].astype(BF16)

    def own_rows(x16, h):
        return x16[(h % 2) * nq:(h % 2 + 1) * nq]

    def scores(k_ref, bias):
        rows = [own_rows(_dot(q_ref[0, h // 2], tile(k_ref, h)), h) + bias[h:h + 1, :] * LOG2E
                for h in range(A_HEADS)]
        return jnp.concatenate(rows, axis=0)

    def update(s_list, v_refs):
        m_old = m_sc[...]
        m_tile = s_list[0]
        for s in s_list[1:]:
            m_tile = jnp.maximum(m_tile, s)
        m_new = jnp.maximum(m_old, jnp.max(m_tile, axis=-1, keepdims=True))
        alpha = jnp.exp2(m_old - m_new)
        ps = [jnp.exp2(s - m_new) for s in s_list]
        l_tile = ps[0]
        for p in ps[1:]:
            l_tile = l_tile + p
        l_sc[...] = alpha * l_sc[...] + jnp.sum(l_tile, axis=-1, keepdims=True)
        pb = [p.astype(BF16) for p in ps]
        outs = []
        for h in range(A_HEADS):
            pair = slice((h // 2) * 2 * nq, (h // 2 + 1) * 2 * nq)
            pv = None
            for p, v_ref in zip(pb, v_refs):
                d = lax.dot_general(p[pair], tile(v_ref, h), NT, preferred_element_type=F32)
                pv = d if pv is None else pv + d
            outs.append(own_rows(pv, h))
        acc_sc[...] = alpha[:, :A_DIM] * acc_sc[...] + jnp.concatenate(outs, axis=0)
        m_sc[...] = m_new

    @pl.when(g == 0)
    def _():
        m_sc[...] = jnp.full(m_sc.shape, -jnp.inf, F32)
        l_sc[...] = jnp.zeros(l_sc.shape, F32)
        acc_sc[...] = jnp.zeros(acc_sc.shape, F32)
        carry_sc[...] = jnp.zeros(carry_sc.shape, F32)
        lfn = lfn_ref[0, 0]
        cn = _dot_sel(jnp.concatenate([lfn, jnp.zeros_like(lfn)], axis=0), tri_ref[...])[:A_HEADS]
        s = scores(kn_ref, -cn)
        r = lax.broadcasted_iota(jnp.int32, s.shape, 0)
        c = lax.broadcasted_iota(jnp.int32, s.shape, 1)
        s = jnp.where(c <= (r & (nq - 1)), s, -jnp.inf)
        update([s], [vn_ref])

    order = list(reversed(range(pages)))
    lf_all = jnp.concatenate([page_refs[2 * pages + p][0, 0] for p in order], axis=0)
    sfx_all = _dot_sel(lf_all, sfx_ref[...])
    carry = carry_sc[...]
    s_list = []
    for i, p in enumerate(order):
        blk = sfx_all[i * A_HEADS:(i + 1) * A_HEADS]
        s_list.append(scores(page_refs[p], blk[:, :PAGE] + carry))
        carry = carry + blk[:, PAGE:]
    carry_sc[...] = carry
    update(s_list, [page_refs[pages + p] for p in order])

    @pl.when(g == pl.num_programs(1) - 1)
    def _():
        o_ref[0] = acc_sc[...] / l_sc[...][:, :A_DIM]


def _fox_sample(page_table, q, kn, vn, lfn, cache_kt, cache_vt, cache_lft, layer, pages=8):
    db, n_pages = page_table.shape
    groups = n_pages // pages
    rows = q.shape[1] * q.shape[2]
    idx = np.arange(PAGE)
    tri = jnp.asarray(idx[:, None] <= idx[None, :], BF16)
    sfx = jnp.asarray(np.concatenate([idx[:, None] > idx[None, :], np.ones((PAGE, PAGE), bool)], 1), BF16)
    kv_page = (A_HEADS, A_DIM, PAGE)
    lf_page = (A_HEADS, PAGE)

    def per_b(shape):
        nd = len(shape)
        return pl.BlockSpec((1,) + shape, lambda b, g, pt: (b,) + (0,) * nd)

    def new_page(shape):
        nd = len(shape)
        return pl.BlockSpec((1, 1) + shape, lambda b, g, pt: (0, b) + (0,) * nd)

    def const(shape):
        nd = len(shape)
        return pl.BlockSpec(shape, lambda b, g, pt: (0,) * nd)

    def page_spec(shape, p):
        nd = len(shape)
        return pl.BlockSpec(
            (1, 1) + shape,
            lambda b, g, pt: (layer, pt[b * n_pages + (groups - 1 - g) * pages + p]) + (0,) * nd)

    in_specs = [per_b(q.shape[1:]), new_page(kv_page), new_page(kv_page), new_page(lf_page),
                const(tri.shape), const(sfx.shape)]
    in_specs += [page_spec(kv_page, p) for p in range(pages)]
    in_specs += [page_spec(kv_page, p) for p in range(pages)]
    in_specs += [page_spec(lf_page, p) for p in range(pages)]
    grid_spec = pltpu.PrefetchScalarGridSpec(
        num_scalar_prefetch=1,
        grid=(db, groups),
        in_specs=in_specs,
        out_specs=pl.BlockSpec((1, rows, A_DIM), lambda b, g, pt: (b, 0, 0)),
        scratch_shapes=[pltpu.VMEM((rows, LANES), F32), pltpu.VMEM((rows, LANES), F32),
                        pltpu.VMEM((rows, A_DIM), F32), pltpu.VMEM((A_HEADS, LANES), F32)],
    )
    return pl.pallas_call(
        functools.partial(_fox_sample_kernel, pages=pages),
        grid_spec=grid_spec,
        out_shape=jax.ShapeDtypeStruct((db, rows, A_DIM), F32),
        compiler_params=_cparams("arbitrary", "arbitrary"),
        name="fox_sample",
    )(page_table.reshape(-1), q, kn, vn, lfn, tri, sfx,
      *([cache_kt] * pages), *([cache_vt] * pages), *([cache_lft] * pages))


def _gla_chunk(q, k, v, la, s_old, tril, row0, ones, cast):
    c = q.shape[0]
    bc = jnp.dot(tril, la, precision=HIGHEST, preferred_element_type=F32)
    ref = bc[c // 2 - 1:c // 2, :]
    last = bc[c - 1:c, :]
    qt = q * jnp.exp(bc - ref)
    kt = k * jnp.exp(ref - bc)
    o = _dot(cast(qt * jnp.exp(ref)), cast(s_old))
    a = lax.dot_general(cast(qt), cast(kt), NT, preferred_element_type=F32)
    a = jnp.where(tril > 0, a, 0.0)
    vc = cast(v)
    o = o + _dot(cast(a), vc)
    kd = cast(kt * jnp.exp(last - ref))
    scale = lax.dot_general(row0 * jnp.exp(last), ones, TN, precision=HIGHEST, preferred_element_type=F32)
    s_new = scale * s_old + lax.dot_general(kd, vc, TN, preferred_element_type=F32)
    return o, s_new


def _gla_prompt_kernel(q_ref, k_ref, v_ref, la_ref, tril_ref, o_ref, st_ref, *, chunk):
    @pl.when(pl.program_id(1) == 0)
    def _():
        st_ref[...] = jnp.zeros(st_ref.shape, F32)

    tril = tril_ref[...]
    r = lax.broadcasted_iota(jnp.int32, tril.shape, 0)
    c = lax.broadcasted_iota(jnp.int32, tril.shape, 1)
    causal = c <= r

    def body(ci, carry):
        rows = pl.ds(pl.multiple_of(ci * chunk, chunk), chunk)
        bc = _sel_dot(tril, la_ref[rows, :])
        ref = bc[chunk // 2 - 1:chunk // 2, :]
        last = bc[chunk - 1:chunk, :]
        qt = q_ref[rows, :] * jnp.exp(bc - ref)
        kt = k_ref[rows, :] * jnp.exp(ref - bc)
        qs = (qt * jnp.exp(ref)).astype(BF16)
        kd = (kt * jnp.exp(last - ref)).astype(BF16)
        decay = jnp.exp(last)
        qtb = qt.astype(BF16)
        ktb = kt.astype(BF16)
        for h in range(B_HEADS):
            kc = slice(h * B_DK, (h + 1) * B_DK)
            vc = slice(h * B_DV, (h + 1) * B_DV)
            st = st_ref[0, h]
            vb = v_ref[rows, vc].astype(BF16)
            o = lax.dot_general(qs[:, kc], st.astype(BF16), NT, preferred_element_type=F32)
            a = lax.dot_general(qtb[:, kc], ktb[:, kc], NT, preferred_element_type=F32)
            a = jnp.where(causal, a, 0.0).astype(BF16)
            o_ref[rows, vc] = o + _dot(a, vb)
            st_ref[0, h] = st * decay[:, kc] + lax.dot_general(vb, kd[:, kc], TN, preferred_element_type=F32)
        return carry

    lax.fori_loop(0, q_ref.shape[0] // chunk, body, 0, unroll=2)


def _gla_consts(chunk):
    tril = jnp.tril(jnp.ones((chunk, chunk), F32))
    row0 = jnp.zeros((8, B_DK), F32).at[0].set(1.0)
    ones = jnp.ones((8, B_DV), F32)
    return tril, row0, ones


def _gla_prompt(qb, kb, vb, la, batch, seq, rows, chunk=64):
    steps = seq // rows
    kw, vw = B_HEADS * B_DK, B_HEADS * B_DV
    tril = jnp.tril(jnp.ones((chunk, chunk), BF16))
    rk = pl.BlockSpec((rows, kw), lambda b, i: (b * steps + i, 0))
    rv = pl.BlockSpec((rows, vw), lambda b, i: (b * steps + i, 0))
    return pl.pallas_call(
        functools.partial(_gla_prompt_kernel, chunk=chunk),
        grid=(batch, steps),
        in_specs=[rk, rk, rv, rk, _resident(tril.shape)],
        out_specs=[rv, pl.BlockSpec((1, B_HEADS, B_DV, B_DK), lambda b, i: (b, 0, 0, 0))],
        out_shape=[jax.ShapeDtypeStruct(vb.shape, F32),
                   jax.ShapeDtypeStruct((batch, B_HEADS, B_DV, B_DK), F32)],
        compiler_params=_cparams("arbitrary", "arbitrary"),
        name="gla_prompt",
    )(qb, kb, vb, la, tril)


def _gla_sample_kernel(q_ref, k_ref, v_ref, la_ref, s_in_ref, tril_ref, row0_ref, ones_ref, o_ref, s_ref):
    cast = lambda a: a
    tril, row0, ones = tril_ref[...], row0_ref[...], ones_ref[...]
    for h in range(B_HEADS):
        kc = slice(h * B_DK, (h + 1) * B_DK)
        vc = slice(h * B_DV, (h + 1) * B_DV)
        o, s_new = _gla_chunk(q_ref[:, kc], k_ref[:, kc], v_ref[:, vc], la_ref[:, kc],
                              s_in_ref[0, h], tril, row0, ones, cast)
        o_ref[:, vc] = o
        s_ref[0, h] = s_new


def _gla_sample(qb, kb, vb, la, state, db, t):
    kw, vw = B_HEADS * B_DK, B_HEADS * B_DV
    consts = _gla_consts(t)
    rk = pl.BlockSpec((t, kw), lambda b: (b, 0))
    rv = pl.BlockSpec((t, vw), lambda b: (b, 0))
    st = pl.BlockSpec((1, B_HEADS, B_DK, B_DV), lambda b: (b, 0, 0, 0))
    return pl.pallas_call(
        _gla_sample_kernel,
        grid=(db,),
        in_specs=[rk, rk, rv, rk, st] + [_resident(c.shape) for c in consts],
        out_specs=[rv, st],
        out_shape=[jax.ShapeDtypeStruct(vb.shape, F32), jax.ShapeDtypeStruct(state.shape, F32)],
        compiler_params=_cparams("arbitrary"),
        name="gla_sample",
    )(qb, kb, vb, la, state, *consts)


def _mix_kernel(x_ref, oa_ref, ob_ref, gpre_ref, wr_ref, wga_ref, wgb_ref, ggla_ref,
                wpa_ref, wpb_ref, wout_ref, gpost_ref, y_ref):
    x = x_ref[...]
    hn = _rms(x, gpre_ref[...]).astype(BF16)
    r_b = _dot(hn, wr_ref[...])
    ob = ob_ref[...]
    parts = []
    for h in range(B_HEADS):
        seg = ob[:, h * B_DV:(h + 1) * B_DV]
        parts.append(_rms(seg, ggla_ref[...]))
    obn = jnp.concatenate(parts, axis=-1) * (r_b * jax.nn.sigmoid(r_b))
    pa = _dot(oa_ref[...], wpa_ref[...])
    pb = _dot(obn.astype(BF16), wpb_ref[...])
    merged = jax.nn.sigmoid(_dot(hn, wga_ref[...])) * pa + jax.nn.sigmoid(_dot(hn, wgb_ref[...])) * pb
    z = _dot(merged.astype(BF16), wout_ref[...])
    y_ref[...] = x + _rms(z, gpost_ref[...])


def _mix(x, oa, ob, g_pre, w_r, w_ga, w_gb, g_gla, w_pa, w_pb, w_out, g_post, tm):
    t, d = x.shape
    row = lambda n: pl.BlockSpec((tm, n), lambda i: (i, 0))
    consts = (g_pre, w_r, w_ga, w_gb, g_gla, w_pa, w_pb, w_out, g_post)
    return pl.pallas_call(
        _mix_kernel,
        grid=(t // tm,),
        in_specs=[row(d), row(oa.shape[1]), row(ob.shape[1])] + [_resident(c.shape) for c in consts],
        out_specs=row(d),
        out_shape=jax.ShapeDtypeStruct(x.shape, F32),
        compiler_params=_cparams("arbitrary"),
        name="mix",
    )(x, oa, ob, *consts)


def _gelu_tanh(g):
    return 0.5 * g * (1.0 + jnp.tanh(np.sqrt(2.0 / np.pi) * (g + 0.044715 * (g * g * g))))


def _ffn_body(x_ref, gpre_ref, wup_ref, wconv_ref, bconv_ref, wdown_ref, gpost_ref, y_ref, tail_ref,
              acc_sc, shifted):
    x = x_ref[...]
    hn = _rms(x, gpre_ref[...]).astype(BF16)
    n_chunks = wdown_ref.shape[0]
    for j in range(n_chunks):
        halves = []
        for half in range(2):
            idx = half * n_chunks + j
            u = _dot(hn, wup_ref[idx])
            u1, u2 = shifted(idx, u)
            w = wconv_ref[idx]
            halves.append(bconv_ref[idx] + u2 * w[0:1] + u1 * w[1:2] + u * w[2:3])
            tail_ref[idx] = u[u.shape[0] - 8:, :]
        h = (_gelu_tanh(halves[1]) * halves[0]).astype(BF16)
        d = _dot(h, wdown_ref[j])
        if j == 0:
            acc_sc[...] = d
        else:
            acc_sc[...] += d
    y_ref[...] = x + _rms(acc_sc[...], gpost_ref[...])


def _ffn_prompt_kernel(x_ref, gpre_ref, wup_ref, wconv_ref, bconv_ref, wdown_ref, gpost_ref,
                       y_ref, tail_ref, acc_sc, hist_sc, *, tiles_per_seq):
    first = pl.program_id(0) % tiles_per_seq == 0

    @pl.when(first)
    def _():
        hist_sc[...] = jnp.zeros(hist_sc.shape, F32)

    def shifted(idx, u):
        rows = lax.broadcasted_iota(jnp.int32, u.shape, 0)
        p1 = hist_sc[idx, 7:8, :]
        p2 = hist_sc[idx, 6:7, :]
        u1 = jnp.where(rows == 0, p1, pltpu.roll(u, 1, axis=0))
        u2 = jnp.where(rows == 0, p2, jnp.where(rows == 1, p1, pltpu.roll(u, 2, axis=0)))
        hist_sc[idx] = u[u.shape[0] - 8:, :]
        return u1, u2

    _ffn_body(x_ref, gpre_ref, wup_ref, wconv_ref, bconv_ref, wdown_ref, gpost_ref, y_ref,
              tail_ref.at[0], acc_sc, shifted)


def _ffn_sample_kernel(x_ref, h1_ref, h2_ref, gpre_ref, wup_ref, wconv_ref, bconv_ref, wdown_ref,
                       gpost_ref, y_ref, u_ref, acc_sc, *, t):
    x = x_ref[...]
    hn = _rms(x, gpre_ref[...]).astype(BF16)
    n_chunks = wdown_ref.shape[0]
    for j in range(n_chunks):
        halves = []
        for half in range(2):
            idx = half * n_chunks + j
            u = _dot(hn, wup_ref[idx])
            pos = lax.broadcasted_iota(jnp.int32, u.shape, 0) & (t - 1)
            u1 = jnp.where(pos < 1, h1_ref[idx], pltpu.roll(u, 1, axis=0))
            u2 = jnp.where(pos < 2, h2_ref[idx], pltpu.roll(u, 2, axis=0))
            w = wconv_ref[idx]
            halves.append(bconv_ref[idx] + u2 * w[0:1] + u1 * w[1:2] + u * w[2:3])
            u_ref[idx] = u
        h = (_gelu_tanh(halves[1]) * halves[0]).astype(BF16)
        d = _dot(h, wdown_ref[j])
        if j == 0:
            acc_sc[...] = d
        else:
            acc_sc[...] += d
    y_ref[...] = x + _rms(acc_sc[...], gpost_ref[...])


def _ffn_prompt(x, g_pre, wup, wconv, bconv, wdown, g_post, seq, tm):
    t, d = x.shape
    nblk, _, nc = wup.shape
    tiles = t // tm
    row = pl.BlockSpec((tm, d), lambda i: (i, 0))
    consts = (g_pre, wup, wconv, bconv, wdown, g_post)
    return pl.pallas_call(
        functools.partial(_ffn_prompt_kernel, tiles_per_seq=seq // tm),
        grid=(tiles,),
        in_specs=[row] + [_resident(c.shape) for c in consts],
        out_specs=[row, pl.BlockSpec((1, nblk, 8, nc), lambda i: (i, 0, 0, 0))],
        out_shape=[jax.ShapeDtypeStruct(x.shape, F32), jax.ShapeDtypeStruct((tiles, nblk, 8, nc), F32)],
        scratch_shapes=[pltpu.VMEM((tm, d), F32), pltpu.VMEM((nblk, 8, nc), F32)],
        compiler_params=_cparams("arbitrary"),
        name="ffn_prompt",
    )(x, *consts)


def _ffn_sample(x, h1, h2, g_pre, wup, wconv, bconv, wdown, g_post, t):
    rows, d = x.shape
    nblk, _, nc = wup.shape
    consts = (g_pre, wup, wconv, bconv, wdown, g_post)
    return pl.pallas_call(
        functools.partial(_ffn_sample_kernel, t=t),
        grid=(1,),
        in_specs=[_resident(a.shape) for a in (x, h1, h2) + consts],
        out_specs=[pl.BlockSpec(x.shape, lambda i: (0, 0)),
                   pl.BlockSpec((nblk, rows, nc), lambda i: (0, 0, 0))],
        out_shape=[jax.ShapeDtypeStruct(x.shape, F32), jax.ShapeDtypeStruct((nblk, rows, nc), F32)],
        scratch_shapes=[pltpu.VMEM((rows, d), F32)],
        compiler_params=_cparams("arbitrary"),
        name="ffn_sample",
    )(x, h1, h2, *consts)


def _pack_layer(w_in, w_alpha2, w_up, w_conv, b_conv, w_down, nc):
    aw, kw, vw = A_HEADS * A_DIM, B_HEADS * B_DK, B_HEADS * B_DV
    d = w_in.shape[0]
    offs = np.cumsum([aw, aw, aw, A_HEADS, kw, kw, vw, GATE_RANK, vw, d, d])[:-1].tolist()
    q_a, k_a, v_a, f_a, q_b, k_b, v_b, a_low, r_b, g_a, g_b = jnp.split(w_in, offs, axis=1)
    wbig = jnp.concatenate([q_a, k_a, v_a, q_b, k_b, v_b], axis=1).astype(BF16)
    pad = lambda a: jnp.pad(a, ((0, 0), (0, LANES - a.shape[1])))
    wsm = jnp.concatenate([pad(f_a), pad(a_low)], axis=1).astype(BF16)
    wa2 = jnp.pad(w_alpha2, ((0, LANES - GATE_RANK), (0, 0))).astype(BF16)
    dff = w_down.shape[0]
    n_chunks = dff // nc
    wup = w_up.reshape(d, 2 * n_chunks, nc).transpose(1, 0, 2).astype(BF16)
    wconv = w_conv.reshape(w_conv.shape[0], 2 * n_chunks, nc).transpose(1, 0, 2)
    bconv = b_conv.reshape(2 * n_chunks, 1, nc)
    wdown = w_down.reshape(n_chunks, nc, d).astype(BF16)
    return dict(wbig=wbig, wsm=wsm, wa2=wa2, w_r=r_b.astype(BF16), w_ga=g_a.astype(BF16),
                w_gb=g_b.astype(BF16), wup=wup, wconv=wconv, bconv=bconv, wdown=wdown)


def kernel(x_prompt, x_sample, cache_k, cache_v, cache_logf, state_gla, state_conv, page_table,
           g_pre_mix, w_in, b_f, w_alpha2, b_alpha, g_gla, w_proj_a, w_proj_b, w_out, g_post_mix,
           g_pre_ffn, w_up, w_conv, b_conv, w_down, g_post_ffn):
    batch, seq, d = x_prompt.shape
    db, dt, _ = x_sample.shape
    depth = w_in.shape[0]
    n_pool = cache_k.shape[1]
    aw = A_HEADS * A_DIM
    dff2 = w_up.shape[2]
    nc = 256
    assert dt == 8 and depth >= 1

    yp = x_prompt.reshape(batch * seq, d)
    ys = x_sample.reshape(db * dt, d)
    outs = [[] for _ in range(10)]
    for l in range(depth):
        w = _pack_layer(w_in[l], w_alpha2[l], w_up[l], w_conv[l], b_conv[l], w_down[l], nc)
        row = lambda a: a[l].reshape(1, -1)
        inproj_w = (row(g_pre_mix), w["wbig"], w["wsm"], row(b_f), w["wa2"], row(b_alpha))
        mix_w = (row(g_pre_mix), w["w_r"], w["w_ga"], w["w_gb"], row(g_gla), w_proj_a[l].astype(BF16),
                 w_proj_b[l].astype(BF16), w_out[l].astype(BF16), row(g_post_mix))
        ffn_w = (row(g_pre_ffn), w["wup"], w["wconv"], w["bconv"], w["wdown"], row(g_post_ffn))

        qa, ka, va, kaf, vaf, logf, qb, kb, vb, la = _inproj(yp, *inproj_w, tm=256)
        lft = logf.reshape(batch, seq, A_HEADS).transpose(0, 2, 1)
        oa = _fox_prompt(qa, ka, va, _key_bias_lanes(_cumsum_lanes(lft)), batch, seq, tq=512, tk=512)
        ob, gla_state_t = _gla_prompt(qb, kb, vb, la, batch, seq, rows=512)
        gla_state = gla_state_t.transpose(0, 1, 3, 2)
        x1 = _mix(yp, oa, ob, *mix_w, tm=512)
        yp, tails = _ffn_prompt(x1, *ffn_w, seq=seq, tm=512)
        tiles_per_seq = seq // 512
        tails = tails.reshape(batch, tiles_per_seq, 2 * (dff2 // 2 // nc), 8, nc)[:, -1, :, 6:, :]
        conv_p = tails.transpose(0, 2, 1, 3).reshape(batch, 2, dff2)
        for lst, val in zip(outs[:5], (kaf.reshape(batch, seq, A_HEADS, A_DIM),
                                       vaf.reshape(batch, seq, A_HEADS, A_DIM),
                                       logf.reshape(batch, seq, A_HEADS), gla_state, conv_p)):
            lst.append(val)

        qa, ka, va, kaf, vaf, logf, qb, kb, vb, la = _inproj(ys, *inproj_w, tm=db * dt)
        q_pairs = qa.reshape(db, dt, A_HEADS, A_DIM).transpose(0, 2, 1, 3).reshape(
            db, A_HEADS // 2, 2 * dt, A_DIM)
        padk = lambda a: jnp.pad(a.reshape(1, db, dt, A_HEADS, A_DIM).transpose(0, 1, 3, 4, 2),
                                 ((0, 0), (0, 0), (0, 0), (0, 0), (0, PAGE - dt)))
        lfn = jnp.pad(logf.reshape(1, db, dt, A_HEADS).transpose(0, 1, 3, 2),
                      ((0, 0), (0, 0), (0, 0), (0, PAGE - dt)))
        o_rows = _fox_sample(page_table, q_pairs, padk(kaf), padk(vaf), lfn,
                             cache_k.transpose(0, 1, 3, 4, 2), cache_v.transpose(0, 1, 3, 4, 2),
                             cache_logf.transpose(0, 1, 3, 2), layer=l)
        oa = o_rows.reshape(db, A_HEADS, dt, A_DIM).transpose(0, 2, 1, 3).reshape(db * dt, aw).astype(BF16)
        ob, gla_state = _gla_sample(qb, kb, vb, la, state_gla[l], db, dt)
        x1 = _mix(ys, oa, ob, *mix_w, tm=db * dt)
        st = state_conv[l].astype(F32)
        zeros = jnp.zeros((db, dt, dff2), F32)
        h1 = zeros.at[:, 0].set(st[:, 1])
        h2 = zeros.at[:, 0].set(st[:, 0]).at[:, 1].set(st[:, 1])
        chunked = lambda a: a.reshape(db * dt, dff2 // nc, nc).transpose(1, 0, 2)
        ys, u = _ffn_sample(x1, chunked(h1), chunked(h2), *ffn_w, t=dt)
        conv_s = u.transpose(1, 0, 2).reshape(db, dt, dff2)[:, dt - 2:, :]
        for lst, val in zip(outs[5:], (kaf.reshape(db, dt, A_HEADS, A_DIM),
                                       vaf.reshape(db, dt, A_HEADS, A_DIM),
                                       logf.reshape(db, dt, A_HEADS), gla_state, conv_s)):
            lst.append(val)

    stacked = [jnp.stack(o) for o in outs]
    return (yp.reshape(batch, seq, d), ys.reshape(db, dt, d), *stacked)
```

```python
import functools

import jax
import jax.numpy as jnp
import numpy as np
from jax import lax
from jax.experimental import pallas as pl
from jax.experimental.pallas import tpu as pltpu

F32 = jnp.float32
BF16 = jnp.bfloat16
EPS = 1e-6
GATE_TEMP = 16.0
PAGE = 128
A_HEADS, A_DIM = 8, 64
B_HEADS, B_DK, B_DV = 4, 128, 256
GATE_RANK = 16
LANES = 128
VMEM_LIMIT = 56 * 1024 * 1024
HIGHEST = lax.Precision.HIGHEST
LOG2E = float(np.log2(np.e))
NT = (((1,), (1,)), ((), ()))
TN = (((0,), (0,)), ((), ()))


def _cparams(*sem):
    return pltpu.CompilerParams(dimension_semantics=sem, vmem_limit_bytes=VMEM_LIMIT)


def _resident(shape):
    nd = len(shape)
    return pl.BlockSpec(shape, lambda *_: (0,) * nd, pipeline_mode=pl.Buffered(1))


def _rms(x, g):
    return x * lax.rsqrt(jnp.mean(x * x, axis=-1, keepdims=True) + EPS) * g


def _log_sigmoid(x):
    return jnp.minimum(x, 0.0) - jnp.log1p(jnp.exp(-jnp.abs(x)))


def _dot(a, b):
    return jnp.dot(a, b, preferred_element_type=F32)


def _trunc_to_bf16_bits(x):
    bits = lax.bitcast_convert_type(x, jnp.int32) & jnp.int32(-65536)
    return lax.bitcast_convert_type(bits, F32)


def _split3(x):
    hi = _trunc_to_bf16_bits(x)
    r1 = x - hi
    mid = _trunc_to_bf16_bits(r1)
    lo = r1 - mid
    return hi.astype(BF16), mid.astype(BF16), lo.astype(BF16)


def _dot_sel(x, sel):
    return sum(_dot(part, sel) for part in _split3(x))


def _sel_dot(sel, x):
    return sum(_dot(sel, part) for part in _split3(x))


def _inproj_kernel(x_ref, g_ref, wbig_ref, wsm_ref, bf_ref, wa2_ref, ba_ref,
                   qa_ref, ka_ref, va_ref, kaf_ref, vaf_ref, logf_ref,
                   qb_ref, kb_ref, vb_ref, la_ref):
    hn = _rms(x_ref[...], g_ref[...]).astype(BF16)
    aw = A_HEADS * A_DIM
    kw = B_HEADS * B_DK

    def mm(lo, hi):
        return _dot(hn, wbig_ref[:, lo:hi])

    qa_ref[...] = (mm(0, aw) * (A_DIM ** -0.5 * LOG2E)).astype(BF16)
    k_a = mm(aw, 2 * aw)
    kaf_ref[...] = k_a
    ka_ref[...] = k_a.astype(BF16)
    v_a = mm(2 * aw, 3 * aw)
    vaf_ref[...] = v_a
    va_ref[...] = v_a.astype(BF16)
    o = 3 * aw
    qb_ref[...] = mm(o, o + kw) * B_DK ** -0.5
    kb_ref[...] = mm(o + kw, o + 2 * kw)
    vb_ref[...] = mm(o + 2 * kw, o + 2 * kw + B_HEADS * B_DV)
    us = _dot(hn, wsm_ref[...])
    logf_ref[...] = _log_sigmoid(us[:, :A_HEADS] + bf_ref[...])
    z = _dot(us[:, LANES:].astype(BF16), wa2_ref[...]) + ba_ref[...]
    la_ref[...] = _log_sigmoid(z) / GATE_TEMP


def _inproj(x, g_pre, wbig, wsm, b_f, wa2, b_alpha, tm):
    t, d = x.shape
    aw, kw, vw = A_HEADS * A_DIM, B_HEADS * B_DK, B_HEADS * B_DV
    row = lambda n: pl.BlockSpec((tm, n), lambda i: (i, 0))
    outs = [(aw, BF16)] * 3 + [(aw, F32)] * 2 + [(A_HEADS, F32), (kw, F32), (kw, F32), (vw, F32), (kw, F32)]
    return pl.pallas_call(
        _inproj_kernel,
        grid=(t // tm,),
        in_specs=[row(d), _resident(g_pre.shape), _resident(wbig.shape), _resident(wsm.shape),
                  _resident(b_f.shape), _resident(wa2.shape), _resident(b_alpha.shape)],
        out_specs=[row(n) for n, _ in outs],
        out_shape=[jax.ShapeDtypeStruct((t, n), dt) for n, dt in outs],
        compiler_params=_cparams("arbitrary"),
        name="inproj",
    )(x, g_pre, wbig, wsm, b_f, wa2, b_alpha)


def _cumsum_kernel(x_ref, tri_ref, o_ref, *, seg):
    n = x_ref.shape[-1]
    carry = jnp.zeros((x_ref.shape[1], 1), F32)
    for s in range(n // seg):
        cs = jnp.dot(x_ref[0, :, s * seg:(s + 1) * seg], tri_ref[...],
                     precision=HIGHEST, preferred_element_type=F32) + carry
        o_ref[0, :, s * seg:(s + 1) * seg] = cs
        carry = cs[:, seg - 1:seg]


def _cumsum_lanes(x, seg=256):
    b, h, n = x.shape
    tri = jnp.triu(jnp.ones((seg, seg), F32))
    blk = pl.BlockSpec((1, h, n), lambda i: (i, 0, 0))
    return pl.pallas_call(
        functools.partial(_cumsum_kernel, seg=seg),
        grid=(b,),
        in_specs=[blk, _resident(tri.shape)],
        out_specs=blk,
        out_shape=jax.ShapeDtypeStruct(x.shape, F32),
        compiler_params=_cparams("arbitrary"),
        name="cumsum_logf",
    )(x, tri)


def _fox_prompt_kernel(q_ref, k_ref, v_ref, kb_ref, o_ref, m_sc, l_sc, acc_sc, *, tq, tk):
    i = pl.program_id(2)
    q = q_ref[...]
    lane = lax.broadcasted_iota(jnp.int32, q.shape, 1)
    zero = jnp.zeros_like(q)
    sel_even = jnp.where(lane < 3, 1.0, 0.0).astype(BF16)
    sel_odd = jnp.where(lane < 3, 0.0, jnp.where(lane < 6, 1.0, 0.0)).astype(BF16)
    q_even = jnp.concatenate([jnp.where(lane < A_DIM, q, zero), sel_even], axis=1)
    q_odd = jnp.concatenate([jnp.where(lane >= A_DIM, q, zero), sel_odd], axis=1)
    qs = jnp.concatenate([q_even, q_odd], axis=0)
    m_sc[...] = jnp.full(m_sc.shape, -jnp.inf, F32)
    l_sc[...] = jnp.zeros(l_sc.shape, F32)
    acc_sc[...] = jnp.zeros(acc_sc.shape, F32)
    reps = tk // LANES

    def block(j, masked):
        start = pl.multiple_of(j * tk, tk)
        ka = jnp.concatenate([k_ref[pl.ds(start, tk), :], kb_ref[pl.ds(start, tk), :]], axis=1)
        s = lax.dot_general(qs, ka, NT, preferred_element_type=F32)
        if masked:
            r = lax.broadcasted_iota(jnp.int32, s.shape, 0)
            c = lax.broadcasted_iota(jnp.int32, s.shape, 1)
            qpos = i * tq + jnp.where(r >= tq, r - tq, r)
            s = jnp.where(start + c <= qpos, s, -jnp.inf)
        m_old = m_sc[...]
        m_new = jnp.maximum(m_old, jnp.max(s, axis=-1, keepdims=True))
        alpha = jnp.exp2(m_old - m_new)
        p = jnp.exp2(s - jnp.tile(m_new, (1, reps)))
        l_sc[...] = alpha * l_sc[...] + jnp.sum(p, axis=-1, keepdims=True)
        acc_sc[...] = alpha * acc_sc[...] + _dot(p.astype(BF16), v_ref[pl.ds(start, tk), :])
        m_sc[...] = m_new

    n_full = i * (tq // tk)

    def body(jj, carry):
        block(2 * jj, False)
        block(2 * jj + 1, False)
        return carry

    lax.fori_loop(0, n_full // 2, body, 0)

    @pl.when(n_full % 2 == 1)
    def _():
        block(n_full - 1, False)

    for d in range(tq // tk):
        block(n_full + d, True)
    o = acc_sc[...] / l_sc[...]
    o_ref[...] = jnp.where(lane < A_DIM, o[:tq], o[tq:]).astype(o_ref.dtype)


def _key_bias_lanes(c):
    batch, heads, seq = c.shape
    parts = jnp.stack(_split3(-c * LOG2E), axis=-1).reshape(batch, heads // 2, 2, seq, 3)
    parts = parts.transpose(0, 3, 1, 2, 4).reshape(batch, seq, heads // 2, 6)
    parts = jnp.pad(parts, ((0, 0), (0, 0), (0, 0), (0, LANES - 6)))
    return parts.reshape(batch * seq, heads // 2 * LANES)


def _fox_prompt(qa, ka, va, kbias, batch, seq, tq, tk):
    assert tq % tk == 0
    nq = seq // tq
    pairs = A_HEADS // 2
    kv_spec = pl.BlockSpec((seq, LANES), lambda b, p, i: (b, p))
    q_spec = pl.BlockSpec((tq, LANES), lambda b, p, i: (b * nq + i, p))
    return pl.pallas_call(
        functools.partial(_fox_prompt_kernel, tq=tq, tk=tk),
        grid=(batch, pairs, nq),
        in_specs=[q_spec, kv_spec, kv_spec, kv_spec],
        out_specs=q_spec,
        out_shape=jax.ShapeDtypeStruct(qa.shape, BF16),
        scratch_shapes=[pltpu.VMEM((2 * tq, LANES), F32), pltpu.VMEM((2 * tq, LANES), F32),
                        pltpu.VMEM((2 * tq, LANES), F32)],
        compiler_params=_cparams("arbitrary", "arbitrary", "arbitrary"),
        name="fox_prompt",
    )(qa, ka, va, kbias)


def _fox_sample_kernel(pt_ref, q_ref, kn_ref, vn_ref, lfn_ref, tri_ref, sfx_ref, *rest, pages):
    page_refs = rest[:3 * pages]
    o_ref, m_sc, l_sc, acc_sc, carry_sc = rest[3 * pages:]
    g = pl.program_id(1)
    halves = 2
    hh = A_HEADS // halves
    nq = q_ref.shape[2] // hh

    def tiles(refs, half):
        return jnp.concatenate(
            [r[0, 0, half * hh:(half + 1) * hh].reshape(hh * A_DIM, PAGE).astype(BF16) for r in refs], axis=1)

    def rows_of(bias, half):
        return jnp.concatenate(
            [jnp.broadcast_to(bias[h:h + 1, :], (nq, bias.shape[1])) for h in range(half * hh, (half + 1) * hh)],
            axis=0) * LOG2E

    def update(half, s_list, v_groups):
        m_old = m_sc[half]
        m_row = jnp.max(s_list[0], axis=-1, keepdims=True)
        for s in s_list[1:]:
            m_row = jnp.maximum(m_row, jnp.max(s, axis=-1, keepdims=True))
        m_new = jnp.maximum(m_old, m_row)
        alpha = jnp.exp2(m_old - m_new)
        l_row = None
        pv = None
        for s, v_refs in zip(s_list, v_groups):
            p = jnp.exp2(s - jnp.tile(m_new, (1, s.shape[1] // LANES)))
            ls = jnp.sum(p, axis=-1, keepdims=True)
            l_row = ls if l_row is None else l_row + ls
            d = lax.dot_general(p.astype(BF16), tiles(v_refs, half), NT, preferred_element_type=F32)
            pv = d if pv is None else pv + d
        l_sc[half] = alpha * l_sc[half] + l_row
        acc_sc[half] = jnp.tile(alpha, (1, hh * A_DIM // LANES)) * acc_sc[half] + pv
        m_sc[half] = m_new

    @pl.when(g == 0)
    def _():
        m_sc[...] = jnp.full(m_sc.shape, -jnp.inf, F32)
        l_sc[...] = jnp.zeros(l_sc.shape, F32)
        acc_sc[...] = jnp.zeros(acc_sc.shape, F32)
        carry_sc[...] = jnp.zeros(carry_sc.shape, F32)
        lfn = lfn_ref[0, 0]
        cn = _dot_sel(jnp.concatenate([lfn, jnp.zeros_like(lfn)], axis=0), tri_ref[...])[:A_HEADS]
        for half in range(halves):
            s = _dot(q_ref[0, half], tiles([kn_ref], half)) - rows_of(cn, half)
            r = lax.broadcasted_iota(jnp.int32, s.shape, 0)
            c = lax.broadcasted_iota(jnp.int32, s.shape, 1)
            s = jnp.where(c <= (r & (nq - 1)), s, -jnp.inf)
            update(half, [s], [[vn_ref]])

    order = list(reversed(range(pages)))
    lf_all = jnp.concatenate([page_refs[2 * pages + p][0, 0] for p in order], axis=0)
    sfx_all = _dot_sel(lf_all, sfx_ref[...])
    carry = carry_sc[...]
    biases = []
    for i in range(pages):
        blk = sfx_all[i * A_HEADS:(i + 1) * A_HEADS]
        biases.append(blk[:, :PAGE] + carry)
        carry = carry + blk[:, PAGE:]
    carry_sc[...] = carry
    pairs = [(order[i], order[i + 1], jnp.concatenate(biases[i:i + 2], axis=1)) for i in range(0, pages, 2)]
    for half in range(halves):
        s_list = [_dot(q_ref[0, half], tiles([page_refs[a], page_refs[b]], half)) + rows_of(bias, half)
                  for a, b, bias in pairs]
        update(half, s_list, [[page_refs[pages + a], page_refs[pages + b]] for a, b, _ in pairs])

    @pl.when(g == pl.num_programs(1) - 1)
    def _():
        for half in range(halves):
            o_ref[0, half] = acc_sc[half] / jnp.tile(l_sc[half], (1, hh * A_DIM // LANES))


def _fox_sample(page_table, q, kn, vn, lfn, cache_kt, cache_vt, cache_lft, layer, pages=8):
    db, n_pages = page_table.shape
    groups = n_pages // pages
    assert pages % 2 == 0
    state = q.shape[1:3] + (LANES,)
    idx = np.arange(PAGE)
    tri = jnp.asarray(idx[:, None] <= idx[None, :], BF16)
    sfx = jnp.asarray(np.concatenate([idx[:, None] > idx[None, :], np.ones((PAGE, PAGE), bool)], 1), BF16)
    kv_page = (A_HEADS, A_DIM, PAGE)
    lf_page = (A_HEADS, PAGE)

    def per_b(shape):
        nd = len(shape)
        return pl.BlockSpec((1,) + shape, lambda b, g, pt: (b,) + (0,) * nd)

    def new_page(shape):
        nd = len(shape)
        return pl.BlockSpec((1, 1) + shape, lambda b, g, pt: (0, b) + (0,) * nd)

    def const(shape):
        nd = len(shape)
        return pl.BlockSpec(shape, lambda b, g, pt: (0,) * nd)

    def page_spec(shape, p):
        nd = len(shape)
        return pl.BlockSpec(
            (1, 1) + shape,
            lambda b, g, pt: (layer, pt[b * n_pages + (groups - 1 - g) * pages + p]) + (0,) * nd)

    in_specs = [per_b(q.shape[1:]), new_page(kv_page), new_page(kv_page), new_page(lf_page),
                const(tri.shape), const(sfx.shape)]
    in_specs += [page_spec(kv_page, p) for p in range(pages)]
    in_specs += [page_spec(kv_page, p) for p in range(pages)]
    in_specs += [page_spec(lf_page, p) for p in range(pages)]
    grid_spec = pltpu.PrefetchScalarGridSpec(
        num_scalar_prefetch=1,
        grid=(db, groups),
        in_specs=in_specs,
        out_specs=per_b(q.shape[1:]),
        scratch_shapes=[pltpu.VMEM(state, F32), pltpu.VMEM(state, F32),
                        pltpu.VMEM(q.shape[1:], F32), pltpu.VMEM((A_HEADS, LANES), F32)],
    )
    return pl.pallas_call(
        functools.partial(_fox_sample_kernel, pages=pages),
        grid_spec=grid_spec,
        out_shape=jax.ShapeDtypeStruct(q.shape, F32),
        compiler_params=_cparams("arbitrary", "arbitrary"),
        name="fox_sample",
    )(page_table.reshape(-1), q, kn, vn, lfn, tri, sfx,
      *([cache_kt] * pages), *([cache_vt] * pages), *([cache_lft] * pages))


def _gla_chunk(q, k, v, la, s_old, tril, row0, ones, cast):
    c = q.shape[0]
    bc = jnp.dot(tril, la, precision=HIGHEST, preferred_element_type=F32)
    ref = bc[c // 2 - 1:c // 2, :]
    last = bc[c - 1:c, :]
    qt = q * jnp.exp(bc - ref)
    kt = k * jnp.exp(ref - bc)
    o = _dot(cast(qt * jnp.exp(ref)), cast(s_old))
    a = lax.dot_general(cast(qt), cast(kt), NT, preferred_element_type=F32)
    a = jnp.where(tril > 0, a, 0.0)
    vc = cast(v)
    o = o + _dot(cast(a), vc)
    kd = cast(kt * jnp.exp(last - ref))
    scale = lax.dot_general(row0 * jnp.exp(last), ones, TN, precision=HIGHEST, preferred_element_type=F32)
    s_new = scale * s_old + lax.dot_general(kd, vc, TN, preferred_element_type=F32)
    return o, s_new


def _gla_prompt_kernel(q_ref, k_ref, v_ref, la_ref, tril_ref, o_ref, st_ref, *, chunk):
    @pl.when(pl.program_id(1) == 0)
    def _():
        st_ref[...] = jnp.zeros(st_ref.shape, F32)

    tril = tril_ref[...]
    r = lax.broadcasted_iota(jnp.int32, tril.shape, 0)
    c = lax.broadcasted_iota(jnp.int32, tril.shape, 1)
    causal = c <= r

    def body(ci, carry):
        rows = pl.ds(pl.multiple_of(ci * chunk, chunk), chunk)
        bc = _sel_dot(tril, la_ref[rows, :])
        ref = bc[chunk // 2 - 1:chunk // 2, :]
        last = bc[chunk - 1:chunk, :]
        qt = q_ref[rows, :] * jnp.exp(bc - ref)
        kt = k_ref[rows, :] * jnp.exp(ref - bc)
        qs = (qt * jnp.exp(ref)).astype(BF16)
        kd = (kt * jnp.exp(last - ref)).astype(BF16)
        decay = jnp.exp(last)
        qtb = qt.astype(BF16)
        ktb = kt.astype(BF16)
        for h in range(B_HEADS):
            kc = slice(h * B_DK, (h + 1) * B_DK)
            vc = slice(h * B_DV, (h + 1) * B_DV)
            st = st_ref[0, h]
            vb = v_ref[rows, vc].astype(BF16)
            o = lax.dot_general(qs[:, kc], st.astype(BF16), NT, preferred_element_type=F32)
            a = lax.dot_general(qtb[:, kc], ktb[:, kc], NT, preferred_element_type=F32)
            a = jnp.where(causal, a, 0.0).astype(BF16)
            o_ref[rows, vc] = o + _dot(a, vb)
            st_ref[0, h] = st * decay[:, kc] + lax.dot_general(vb, kd[:, kc], TN, preferred_element_type=F32)
        return carry

    lax.fori_loop(0, q_ref.shape[0] // chunk, body, 0, unroll=2)


def _gla_consts(chunk):
    tril = jnp.tril(jnp.ones((chunk, chunk), F32))
    row0 = jnp.zeros((8, B_DK), F32).at[0].set(1.0)
    ones = jnp.ones((8, B_DV), F32)
    return tril, row0, ones


def _gla_prompt(qb, kb, vb, la, batch, seq, rows, chunk=64):
    steps = seq // rows
    kw, vw = B_HEADS * B_DK, B_HEADS * B_DV
    tril = jnp.tril(jnp.ones((chunk, chunk), BF16))
    rk = pl.BlockSpec((rows, kw), lambda b, i: (b * steps + i, 0))
    rv = pl.BlockSpec((rows, vw), lambda b, i: (b * steps + i, 0))
    return pl.pallas_call(
        functools.partial(_gla_prompt_kernel, chunk=chunk),
        grid=(batch, steps),
        in_specs=[rk, rk, rv, rk, _resident(tril.shape)],
        out_specs=[rv, pl.BlockSpec((1, B_HEADS, B_DV, B_DK), lambda b, i: (b, 0, 0, 0))],
        out_shape=[jax.ShapeDtypeStruct(vb.shape, F32),
                   jax.ShapeDtypeStruct((batch, B_HEADS, B_DV, B_DK), F32)],
        compiler_params=_cparams("arbitrary", "arbitrary"),
        name="gla_prompt",
    )(qb, kb, vb, la, tril)


def _gla_sample_kernel(q_ref, k_ref, v_ref, la_ref, s_in_ref, tril_ref, row0_ref, ones_ref, o_ref, s_ref):
    cast = lambda a: a
    tril, row0, ones = tril_ref[...], row0_ref[...], ones_ref[...]
    for h in range(B_HEADS):
        kc = slice(h * B_DK, (h + 1) * B_DK)
        vc = slice(h * B_DV, (h + 1) * B_DV)
        o, s_new = _gla_chunk(q_ref[:, kc], k_ref[:, kc], v_ref[:, vc], la_ref[:, kc],
                              s_in_ref[0, h], tril, row0, ones, cast)
        o_ref[:, vc] = o
        s_ref[0, h] = s_new


def _gla_sample(qb, kb, vb, la, state, db, t):
    kw, vw = B_HEADS * B_DK, B_HEADS * B_DV
    consts = _gla_consts(t)
    rk = pl.BlockSpec((t, kw), lambda b: (b, 0))
    rv = pl.BlockSpec((t, vw), lambda b: (b, 0))
    st = pl.BlockSpec((1, B_HEADS, B_DK, B_DV), lambda b: (b, 0, 0, 0))
    return pl.pallas_call(
        _gla_sample_kernel,
        grid=(db,),
        in_specs=[rk, rk, rv, rk, st] + [_resident(c.shape) for c in consts],
        out_specs=[rv, st],
        out_shape=[jax.ShapeDtypeStruct(vb.shape, F32), jax.ShapeDtypeStruct(state.shape, F32)],
        compiler_params=_cparams("arbitrary"),
        name="gla_sample",
    )(qb, kb, vb, la, state, *consts)


def _mix_kernel(x_ref, oa_ref, ob_ref, gpre_ref, wr_ref, wga_ref, wgb_ref, ggla_ref,
                wpa_ref, wpb_ref, wout_ref, gpost_ref, y_ref):
    x = x_ref[...]
    hn = _rms(x, gpre_ref[...]).astype(BF16)
    r_b = _dot(hn, wr_ref[...])
    ob = ob_ref[...]
    parts = []
    for h in range(B_HEADS):
        seg = ob[:, h * B_DV:(h + 1) * B_DV]
        parts.append(_rms(seg, ggla_ref[...]))
    obn = jnp.concatenate(parts, axis=-1) * (r_b * jax.nn.sigmoid(r_b))
    pa = _dot(oa_ref[...], wpa_ref[...])
    pb = _dot(obn.astype(BF16), wpb_ref[...])
    merged = jax.nn.sigmoid(_dot(hn, wga_ref[...])) * pa + jax.nn.sigmoid(_dot(hn, wgb_ref[...])) * pb
    z = _dot(merged.astype(BF16), wout_ref[...])
    y_ref[...] = x + _rms(z, gpost_ref[...])


def _mix(x, oa, ob, g_pre, w_r, w_ga, w_gb, g_gla, w_pa, w_pb, w_out, g_post, tm):
    t, d = x.shape
    row = lambda n: pl.BlockSpec((tm, n), lambda i: (i, 0))
    consts = (g_pre, w_r, w_ga, w_gb, g_gla, w_pa, w_pb, w_out, g_post)
    return pl.pallas_call(
        _mix_kernel,
        grid=(t // tm,),
        in_specs=[row(d), row(oa.shape[1]), row(ob.shape[1])] + [_resident(c.shape) for c in consts],
        out_specs=row(d),
        out_shape=jax.ShapeDtypeStruct(x.shape, F32),
        compiler_params=_cparams("arbitrary"),
        name="mix",
    )(x, oa, ob, *consts)


def _gelu_tanh(g):
    return 0.5 * g * (1.0 + jnp.tanh(np.sqrt(2.0 / np.pi) * (g + 0.044715 * (g * g * g))))


def _ffn_body(x_ref, gpre_ref, wup_ref, wconv_ref, bconv_ref, wdown_ref, gpost_ref, y_ref, tail_ref,
              h_sc, shifted):
    x = x_ref[...]
    hn = _rms(x, gpre_ref[...]).astype(BF16)
    n_chunks = wup_ref.shape[0] // 2
    nc = wup_ref.shape[2]
    for j in range(n_chunks):
        halves = []
        for half in range(2):
            idx = half * n_chunks + j
            u = _dot(hn, wup_ref[idx])
            u1, u2 = shifted(idx, u)
            w = wconv_ref[idx]
            halves.append(bconv_ref[idx] + u2 * w[0:1] + u1 * w[1:2] + u * w[2:3])
            tail_ref[idx] = u[u.shape[0] - 8:, :]
        h_sc[:, j * nc:(j + 1) * nc] = (_gelu_tanh(halves[1]) * halves[0]).astype(BF16)
    y_ref[...] = x + _rms(_dot(h_sc[...], wdown_ref[...]), gpost_ref[...])


def _ffn_prompt_kernel(x_ref, gpre_ref, wup_ref, wconv_ref, bconv_ref, wdown_ref, gpost_ref,
                       y_ref, tail_ref, h_sc, hist_sc, *, tiles_per_seq):
    first = pl.program_id(0) % tiles_per_seq == 0

    @pl.when(first)
    def _():
        hist_sc[...] = jnp.zeros(hist_sc.shape, F32)

    def shifted(idx, u):
        rows = lax.broadcasted_iota(jnp.int32, u.shape, 0)
        p1 = hist_sc[idx, 7:8, :]
        p2 = hist_sc[idx, 6:7, :]
        u1 = jnp.where(rows == 0, p1, pltpu.roll(u, 1, axis=0))
        u2 = jnp.where(rows == 0, p2, jnp.where(rows == 1, p1, pltpu.roll(u, 2, axis=0)))
        hist_sc[idx] = u[u.shape[0] - 8:, :]
        return u1, u2

    _ffn_body(x_ref, gpre_ref, wup_ref, wconv_ref, bconv_ref, wdown_ref, gpost_ref, y_ref,
              tail_ref.at[0], h_sc, shifted)


def _ffn_sample_kernel(x_ref, h1_ref, h2_ref, gpre_ref, wup_ref, wconv_ref, bconv_ref, wdown_ref,
                       gpost_ref, y_ref, u_ref, h_sc, *, t):
    x = x_ref[...]
    hn = _rms(x, gpre_ref[...]).astype(BF16)
    n_chunks = wup_ref.shape[0] // 2
    nc = wup_ref.shape[2]
    for j in range(n_chunks):
        halves = []
        for half in range(2):
            idx = half * n_chunks + j
            u = _dot(hn, wup_ref[idx])
            pos = lax.broadcasted_iota(jnp.int32, u.shape, 0) & (t - 1)
            u1 = jnp.where(pos < 1, h1_ref[idx], pltpu.roll(u, 1, axis=0))
            u2 = jnp.where(pos < 2, h2_ref[idx], pltpu.roll(u, 2, axis=0))
            w = wconv_ref[idx]
            halves.append(bconv_ref[idx] + u2 * w[0:1] + u1 * w[1:2] + u * w[2:3])
            u_ref[idx] = u
        h_sc[:, j * nc:(j + 1) * nc] = (_gelu_tanh(halves[1]) * halves[0]).astype(BF16)
    y_ref[...] = x + _rms(_dot(h_sc[...], wdown_ref[...]), gpost_ref[...])


def _ffn_prompt(x, g_pre, wup, wconv, bconv, wdown, g_post, seq, tm):
    t, d = x.shape
    nblk, _, nc = wup.shape
    tiles = t // tm
    row = pl.BlockSpec((tm, d), lambda i: (i, 0))
    consts = (g_pre, wup, wconv, bconv, wdown, g_post)
    return pl.pallas_call(
        functools.partial(_ffn_prompt_kernel, tiles_per_seq=seq // tm),
        grid=(tiles,),
        in_specs=[row] + [_resident(c.shape) for c in consts],
        out_specs=[row, pl.BlockSpec((1, nblk, 8, nc), lambda i: (i, 0, 0, 0))],
        out_shape=[jax.ShapeDtypeStruct(x.shape, F32), jax.ShapeDtypeStruct((tiles, nblk, 8, nc), F32)],
        scratch_shapes=[pltpu.VMEM((tm, wdown.shape[0]), BF16), pltpu.VMEM((nblk, 8, nc), F32)],
        compiler_params=_cparams("arbitrary"),
        name="ffn_prompt",
    )(x, *consts)


def _ffn_sample(x, h1, h2, g_pre, wup, wconv, bconv, wdown, g_post, t):
    rows, d = x.shape
    nblk, _, nc = wup.shape
    consts = (g_pre, wup, wconv, bconv, wdown, g_post)
    return pl.pallas_call(
        functools.partial(_ffn_sample_kernel, t=t),
        grid=(1,),
        in_specs=[_resident(a.shape) for a in (x, h1, h2) + consts],
        out_specs=[pl.BlockSpec(x.shape, lambda i: (0, 0)),
                   pl.BlockSpec((nblk, rows, nc), lambda i: (0, 0, 0))],
        out_shape=[jax.ShapeDtypeStruct(x.shape, F32), jax.ShapeDtypeStruct((nblk, rows, nc), F32)],
        scratch_shapes=[pltpu.VMEM((rows, wdown.shape[0]), BF16)],
        compiler_params=_cparams("arbitrary"),
        name="ffn_sample",
    )(x, h1, h2, *consts)


def _pack_layer(w_in, w_alpha2, w_up, w_conv, b_conv, w_down, nc):
    aw, kw, vw = A_HEADS * A_DIM, B_HEADS * B_DK, B_HEADS * B_DV
    d = w_in.shape[0]
    offs = np.cumsum([aw, aw, aw, A_HEADS, kw, kw, vw, GATE_RANK, vw, d, d])[:-1].tolist()
    q_a, k_a, v_a, f_a, q_b, k_b, v_b, a_low, r_b, g_a, g_b = jnp.split(w_in, offs, axis=1)
    wbig = jnp.concatenate([q_a, k_a, v_a, q_b, k_b, v_b], axis=1).astype(BF16)
    pad = lambda a: jnp.pad(a, ((0, 0), (0, LANES - a.shape[1])))
    wsm = jnp.concatenate([pad(f_a), pad(a_low)], axis=1).astype(BF16)
    wa2 = jnp.pad(w_alpha2, ((0, LANES - GATE_RANK), (0, 0))).astype(BF16)
    dff = w_down.shape[0]
    n_chunks = dff // nc
    wup = w_up.reshape(d, 2 * n_chunks, nc).transpose(1, 0, 2).astype(BF16)
    wconv = w_conv.reshape(w_conv.shape[0], 2 * n_chunks, nc).transpose(1, 0, 2)
    bconv = b_conv.reshape(2 * n_chunks, 1, nc)
    wdown = w_down.astype(BF16)
    return dict(wbig=wbig, wsm=wsm, wa2=wa2, w_r=r_b.astype(BF16), w_ga=g_a.astype(BF16),
                w_gb=g_b.astype(BF16), wup=wup, wconv=wconv, bconv=bconv, wdown=wdown)


def kernel(x_prompt, x_sample, cache_k, cache_v, cache_logf, state_gla, state_conv, page_table,
           g_pre_mix, w_in, b_f, w_alpha2, b_alpha, g_gla, w_proj_a, w_proj_b, w_out, g_post_mix,
           g_pre_ffn, w_up, w_conv, b_conv, w_down, g_post_ffn):
    batch, seq, d = x_prompt.shape
    db, dt, _ = x_sample.shape
    depth = w_in.shape[0]
    n_pool = cache_k.shape[1]
    aw = A_HEADS * A_DIM
    dff2 = w_up.shape[2]
    nc = 256
    assert dt == 8 and depth >= 1

    yp = x_prompt.reshape(batch * seq, d)
    ys = x_sample.reshape(db * dt, d)
    outs = [[] for _ in range(10)]
    for l in range(depth):
        w = _pack_layer(w_in[l], w_alpha2[l], w_up[l], w_conv[l], b_conv[l], w_down[l], nc)
        row = lambda a: a[l].reshape(1, -1)
        inproj_w = (row(g_pre_mix), w["wbig"], w["wsm"], row(b_f), w["wa2"], row(b_alpha))
        mix_w = (row(g_pre_mix), w["w_r"], w["w_ga"], w["w_gb"], row(g_gla), w_proj_a[l].astype(BF16),
                 w_proj_b[l].astype(BF16), w_out[l].astype(BF16), row(g_post_mix))
        ffn_w = (row(g_pre_ffn), w["wup"], w["wconv"], w["bconv"], w["wdown"], row(g_post_ffn))

        qa, ka, va, kaf, vaf, logf, qb, kb, vb, la = _inproj(yp, *inproj_w, tm=512)
        lft = logf.reshape(batch, seq, A_HEADS).transpose(0, 2, 1)
        oa = _fox_prompt(qa, ka, va, _key_bias_lanes(_cumsum_lanes(lft)), batch, seq, tq=512, tk=512)
        ob, gla_state_t = _gla_prompt(qb, kb, vb, la, batch, seq, rows=512)
        gla_state = gla_state_t.transpose(0, 1, 3, 2)
        x1 = _mix(yp, oa, ob, *mix_w, tm=512)
        yp, tails = _ffn_prompt(x1, *ffn_w, seq=seq, tm=512)
        tiles_per_seq = seq // 512
        tails = tails.reshape(batch, tiles_per_seq, 2 * (dff2 // 2 // nc), 8, nc)[:, -1, :, 6:, :]
        conv_p = tails.transpose(0, 2, 1, 3).reshape(batch, 2, dff2)
        for lst, val in zip(outs[:5], (kaf.reshape(batch, seq, A_HEADS, A_DIM),
                                       vaf.reshape(batch, seq, A_HEADS, A_DIM),
                                       logf.reshape(batch, seq, A_HEADS), gla_state, conv_p)):
            lst.append(val)

        qa, ka, va, kaf, vaf, logf, qb, kb, vb, la = _inproj(ys, *inproj_w, tm=db * dt)
        hh = A_HEADS // 2
        eye = jnp.eye(hh, dtype=BF16)
        q_bd = jnp.einsum("btgld,lm->bgltmd", qa.reshape(db, dt, 2, hh, A_DIM), eye).reshape(
            db, 2, hh * dt, hh * A_DIM)
        padk = lambda a: jnp.pad(a.reshape(1, db, dt, A_HEADS, A_DIM).transpose(0, 1, 3, 4, 2),
                                 ((0, 0), (0, 0), (0, 0), (0, 0), (0, PAGE - dt)))
        lfn = jnp.pad(logf.reshape(1, db, dt, A_HEADS).transpose(0, 1, 3, 2),
                      ((0, 0), (0, 0), (0, 0), (0, PAGE - dt)))
        o_bd = _fox_sample(page_table, q_bd, padk(kaf), padk(vaf), lfn,
                           cache_k.transpose(0, 1, 3, 4, 2), cache_v.transpose(0, 1, 3, 4, 2),
                           cache_logf.transpose(0, 1, 3, 2), layer=l)
        oa = jnp.einsum("bgltmd,lm->btgld", o_bd.reshape(db, 2, hh, dt, hh, A_DIM),
                        jnp.eye(hh, dtype=F32)).reshape(db * dt, aw).astype(BF16)
        ob, gla_state = _gla_sample(qb, kb, vb, la, state_gla[l], db, dt)
        x1 = _mix(ys, oa, ob, *mix_w, tm=db * dt)
        st = state_conv[l].astype(F32)
        zeros = jnp.zeros((db, dt, dff2), F32)
        h1 = zeros.at[:, 0].set(st[:, 1])
        h2 = zeros.at[:, 0].set(st[:, 0]).at[:, 1].set(st[:, 1])
        chunked = lambda a: a.reshape(db * dt, dff2 // nc, nc).transpose(1, 0, 2)
        ys, u = _ffn_sample(x1, chunked(h1), chunked(h2), *ffn_w, t=dt)
        conv_s = u.transpose(1, 0, 2).reshape(db, dt, dff2)[:, dt - 2:, :]
        for lst, val in zip(outs[5:], (kaf.reshape(db, dt, A_HEADS, A_DIM),
                                       vaf.reshape(db, dt, A_HEADS, A_DIM),
                                       logf.reshape(db, dt, A_HEADS), gla_state, conv_s)):
            lst.append(val)

    stacked = [jnp.stack(o) for o in outs]
    return (yp.reshape(batch, seq, d), ys.reshape(db, dt, d), *stacked)
```

```python
import functools

import jax
import jax.numpy as jnp
import numpy as np
from jax import lax
from jax.experimental import pallas as pl
from jax.experimental.pallas import tpu as pltpu

F32 = jnp.float32
BF16 = jnp.bfloat16
EPS = 1e-6
GATE_TEMP = 16.0
PAGE = 128
A_HEADS, A_DIM = 8, 64
B_HEADS, B_DK, B_DV = 4, 128, 256
GATE_RANK = 16
LANES = 128
VMEM_LIMIT = 56 * 1024 * 1024
HIGHEST = lax.Precision.HIGHEST
LOG2E = float(np.log2(np.e))
NT = (((1,), (1,)), ((), ()))
TN = (((0,), (0,)), ((), ()))


def _cparams(*sem):
    return pltpu.CompilerParams(dimension_semantics=sem, vmem_limit_bytes=VMEM_LIMIT)


def _resident(shape):
    nd = len(shape)
    return pl.BlockSpec(shape, lambda *_: (0,) * nd, pipeline_mode=pl.Buffered(1))


def _rms(x, g):
    return x * lax.rsqrt(jnp.mean(x * x, axis=-1, keepdims=True) + EPS) * g


def _log_sigmoid(x):
    return jnp.minimum(x, 0.0) - jnp.log1p(jnp.exp(-jnp.abs(x)))


def _dot(a, b):
    return jnp.dot(a, b, preferred_element_type=F32)


def _trunc_to_bf16_bits(x):
    bits = lax.bitcast_convert_type(x, jnp.int32) & jnp.int32(-65536)
    return lax.bitcast_convert_type(bits, F32)


def _split3(x):
    hi = _trunc_to_bf16_bits(x)
    r1 = x - hi
    mid = _trunc_to_bf16_bits(r1)
    lo = r1 - mid
    return hi.astype(BF16), mid.astype(BF16), lo.astype(BF16)


def _dot_sel(x, sel):
    return sum(_dot(part, sel) for part in _split3(x))


def _sel_dot(sel, x):
    return sum(_dot(sel, part) for part in _split3(x))


def _inproj_kernel(x_ref, g_ref, wbig_ref, wsm_ref, bf_ref, wa2_ref, ba_ref,
                   qa_ref, ka_ref, va_ref, kaf_ref, vaf_ref, logf_ref,
                   qb_ref, kb_ref, vb_ref, la_ref):
    hn = _rms(x_ref[...], g_ref[...]).astype(BF16)
    aw = A_HEADS * A_DIM
    kw = B_HEADS * B_DK

    def mm(lo, hi):
        return _dot(hn, wbig_ref[:, lo:hi])

    qa_ref[...] = (mm(0, aw) * (A_DIM ** -0.5 * LOG2E)).astype(BF16)
    k_a = mm(aw, 2 * aw)
    kaf_ref[0] = k_a.T.reshape(A_HEADS, A_DIM, k_a.shape[0])
    ka_ref[...] = k_a.astype(BF16)
    v_a = mm(2 * aw, 3 * aw)
    vaf_ref[0] = v_a.T.reshape(A_HEADS, A_DIM, v_a.shape[0])
    va_ref[...] = v_a.astype(BF16)
    o = 3 * aw
    qb_ref[...] = mm(o, o + kw) * B_DK ** -0.5
    kb_ref[...] = mm(o + kw, o + 2 * kw)
    vb_ref[...] = mm(o + 2 * kw, o + 2 * kw + B_HEADS * B_DV)
    us = _dot(hn, wsm_ref[...])
    logf_ref[...] = _log_sigmoid(us[:, :A_HEADS] + bf_ref[...])
    z = _dot(us[:, LANES:].astype(BF16), wa2_ref[...]) + ba_ref[...]
    la_ref[...] = _log_sigmoid(z) / GATE_TEMP


def _inproj(x, g_pre, wbig, wsm, b_f, wa2, b_alpha, seq, tm):
    t, d = x.shape
    aw, kw, vw = A_HEADS * A_DIM, B_HEADS * B_DK, B_HEADS * B_DV
    row = lambda n: pl.BlockSpec((tm, n), lambda i: (i, 0))
    outs = [(aw, BF16)] * 3 + [(aw, F32)] * 2 + [(A_HEADS, F32), (kw, F32), (kw, F32), (vw, F32), (kw, F32)]
    out_specs = [row(n) for n, _ in outs]
    out_shape = [jax.ShapeDtypeStruct((t, n), dt) for n, dt in outs]
    tiles_per_seq = seq // tm
    for i in (3, 4):
        out_specs[i] = pl.BlockSpec((1, A_HEADS, A_DIM, tm),
                                    lambda i: (i // tiles_per_seq, 0, 0, i % tiles_per_seq))
        out_shape[i] = jax.ShapeDtypeStruct((t // seq, A_HEADS, A_DIM, seq), F32)
    return pl.pallas_call(
        _inproj_kernel,
        grid=(t // tm,),
        in_specs=[row(d), _resident(g_pre.shape), _resident(wbig.shape), _resident(wsm.shape),
                  _resident(b_f.shape), _resident(wa2.shape), _resident(b_alpha.shape)],
        out_specs=out_specs,
        out_shape=out_shape,
        compiler_params=_cparams("arbitrary"),
        name="inproj",
    )(x, g_pre, wbig, wsm, b_f, wa2, b_alpha)


def _cumsum_kernel(x_ref, tri_ref, o_ref, *, seg):
    n = x_ref.shape[-1]
    carry = jnp.zeros((x_ref.shape[1], 1), F32)
    for s in range(n // seg):
        cs = jnp.dot(x_ref[0, :, s * seg:(s + 1) * seg], tri_ref[...],
                     precision=HIGHEST, preferred_element_type=F32) + carry
        o_ref[0, :, s * seg:(s + 1) * seg] = cs
        carry = cs[:, seg - 1:seg]


def _cumsum_lanes(x, seg=256):
    b, h, n = x.shape
    tri = jnp.triu(jnp.ones((seg, seg), F32))
    blk = pl.BlockSpec((1, h, n), lambda i: (i, 0, 0))
    return pl.pallas_call(
        functools.partial(_cumsum_kernel, seg=seg),
        grid=(b,),
        in_specs=[blk, _resident(tri.shape)],
        out_specs=blk,
        out_shape=jax.ShapeDtypeStruct(x.shape, F32),
        compiler_params=_cparams("arbitrary"),
        name="cumsum_logf",
    )(x, tri)


def _fox_prompt_kernel(q_ref, k_ref, v_ref, kb_ref, o_ref, m_sc, l_sc, acc_sc, *, tq, tk):
    i = pl.program_id(2)
    q = q_ref[...]
    lane = lax.broadcasted_iota(jnp.int32, q.shape, 1)
    zero = jnp.zeros_like(q)
    sel_even = jnp.where(lane < 3, 1.0, 0.0).astype(BF16)
    sel_odd = jnp.where(lane < 3, 0.0, jnp.where(lane < 6, 1.0, 0.0)).astype(BF16)
    q_even = jnp.concatenate([jnp.where(lane < A_DIM, q, zero), sel_even], axis=1)
    q_odd = jnp.concatenate([jnp.where(lane >= A_DIM, q, zero), sel_odd], axis=1)
    qs = jnp.concatenate([q_even, q_odd], axis=0)
    m_sc[...] = jnp.full(m_sc.shape, -jnp.inf, F32)
    l_sc[...] = jnp.zeros(l_sc.shape, F32)
    acc_sc[...] = jnp.zeros(acc_sc.shape, F32)
    reps = tk // LANES

    def block(j, masked):
        start = pl.multiple_of(j * tk, tk)
        ka = jnp.concatenate([k_ref[pl.ds(start, tk), :], kb_ref[pl.ds(start, tk), :]], axis=1)
        s = lax.dot_general(qs, ka, NT, preferred_element_type=F32)
        if masked:
            r = lax.broadcasted_iota(jnp.int32, s.shape, 0)
            c = lax.broadcasted_iota(jnp.int32, s.shape, 1)
            qpos = i * tq + jnp.where(r >= tq, r - tq, r)
            s = jnp.where(start + c <= qpos, s, -jnp.inf)
        m_old = m_sc[...]
        m_new = jnp.maximum(m_old, jnp.max(s, axis=-1, keepdims=True))
        alpha = jnp.exp2(m_old - m_new)
        p = jnp.exp2(s - jnp.tile(m_new, (1, reps)))
        l_sc[...] = alpha * l_sc[...] + jnp.sum(p, axis=-1, keepdims=True)
        acc_sc[...] = alpha * acc_sc[...] + _dot(p.astype(BF16), v_ref[pl.ds(start, tk), :])
        m_sc[...] = m_new

    n_full = i * (tq // tk)

    def body(jj, carry):
        block(2 * jj, False)
        block(2 * jj + 1, False)
        return carry

    lax.fori_loop(0, n_full // 2, body, 0)

    @pl.when(n_full % 2 == 1)
    def _():
        block(n_full - 1, False)

    for d in range(tq // tk):
        block(n_full + d, True)
    o = acc_sc[...] / l_sc[...]
    o_ref[...] = jnp.where(lane < A_DIM, o[:tq], o[tq:]).astype(o_ref.dtype)


def _key_bias_lanes(c):
    batch, heads, seq = c.shape
    parts = jnp.stack(_split3(-c * LOG2E), axis=-1).reshape(batch, heads // 2, 2, seq, 3)
    parts = parts.transpose(0, 3, 1, 2, 4).reshape(batch, seq, heads // 2, 6)
    parts = jnp.pad(parts, ((0, 0), (0, 0), (0, 0), (0, LANES - 6)))
    return parts.reshape(batch * seq, heads // 2 * LANES)


def _fox_prompt(qa, ka, va, kbias, batch, seq, tq, tk):
    assert tq % tk == 0
    nq = seq // tq
    pairs = A_HEADS // 2
    kv_spec = pl.BlockSpec((seq, LANES), lambda b, p, i: (b, p))
    q_spec = pl.BlockSpec((tq, LANES), lambda b, p, i: (b * nq + i, p))
    return pl.pallas_call(
        functools.partial(_fox_prompt_kernel, tq=tq, tk=tk),
        grid=(batch, pairs, nq),
        in_specs=[q_spec, kv_spec, kv_spec, kv_spec],
        out_specs=q_spec,
        out_shape=jax.ShapeDtypeStruct(qa.shape, BF16),
        scratch_shapes=[pltpu.VMEM((2 * tq, LANES), F32), pltpu.VMEM((2 * tq, LANES), F32),
                        pltpu.VMEM((2 * tq, LANES), F32)],
        compiler_params=_cparams("arbitrary", "arbitrary", "arbitrary"),
        name="fox_prompt",
    )(qa, ka, va, kbias)


def _fox_sample_kernel(pt_ref, q_ref, kn_ref, vn_ref, lfn_ref, tri_ref, sfx_ref, *rest, pages):
    page_refs = rest[:3 * pages]
    o_ref, m_sc, l_sc, acc_sc, carry_sc = rest[3 * pages:]
    g = pl.program_id(1)
    halves = 2
    hh = A_HEADS // halves
    nq = q_ref.shape[2] // hh

    def tiles(refs, half):
        return jnp.concatenate(
            [r[0, 0, half * hh:(half + 1) * hh].reshape(hh * A_DIM, PAGE).astype(BF16) for r in refs], axis=1)

    def rows_of(bias, half):
        return jnp.concatenate(
            [jnp.broadcast_to(bias[h:h + 1, :], (nq, bias.shape[1])) for h in range(half * hh, (half + 1) * hh)],
            axis=0) * LOG2E

    def update(half, s_list, v_groups):
        m_old = m_sc[half]
        m_row = jnp.max(s_list[0], axis=-1, keepdims=True)
        for s in s_list[1:]:
            m_row = jnp.maximum(m_row, jnp.max(s, axis=-1, keepdims=True))
        m_new = jnp.maximum(m_old, m_row)
        alpha = jnp.exp2(m_old - m_new)
        l_row = None
        pv = None
        for s, v_refs in zip(s_list, v_groups):
            p = jnp.exp2(s - jnp.tile(m_new, (1, s.shape[1] // LANES)))
            ls = jnp.sum(p, axis=-1, keepdims=True)
            l_row = ls if l_row is None else l_row + ls
            d = lax.dot_general(p.astype(BF16), tiles(v_refs, half), NT, preferred_element_type=F32)
            pv = d if pv is None else pv + d
        l_sc[half] = alpha * l_sc[half] + l_row
        acc_sc[half] = jnp.tile(alpha, (1, hh * A_DIM // LANES)) * acc_sc[half] + pv
        m_sc[half] = m_new

    @pl.when(g == 0)
    def _():
        m_sc[...] = jnp.full(m_sc.shape, -jnp.inf, F32)
        l_sc[...] = jnp.zeros(l_sc.shape, F32)
        acc_sc[...] = jnp.zeros(acc_sc.shape, F32)
        carry_sc[...] = jnp.zeros(carry_sc.shape, F32)
        lfn = lfn_ref[0, 0]
        cn = _dot_sel(jnp.concatenate([lfn, jnp.zeros_like(lfn)], axis=0), tri_ref[...])[:A_HEADS]
        for half in range(halves):
            s = _dot(q_ref[0, half], tiles([kn_ref], half)) - rows_of(cn, half)
            r = lax.broadcasted_iota(jnp.int32, s.shape, 0)
            c = lax.broadcasted_iota(jnp.int32, s.shape, 1)
            s = jnp.where(c <= (r & (nq - 1)), s, -jnp.inf)
            update(half, [s], [[vn_ref]])

    order = list(reversed(range(pages)))
    lf_all = jnp.concatenate([page_refs[2 * pages + p][0, 0] for p in order], axis=0)
    sfx_all = _dot_sel(lf_all, sfx_ref[...])
    carry = carry_sc[...]
    biases = []
    for i in range(pages):
        blk = sfx_all[i * A_HEADS:(i + 1) * A_HEADS]
        biases.append(blk[:, :PAGE] + carry)
        carry = carry + blk[:, PAGE:]
    carry_sc[...] = carry
    pairs = [(order[i], order[i + 1], jnp.concatenate(biases[i:i + 2], axis=1)) for i in range(0, pages, 2)]
    for half in range(halves):
        s_list = [_dot(q_ref[0, half], tiles([page_refs[a], page_refs[b]], half)) + rows_of(bias, half)
                  for a, b, bias in pairs]
        update(half, s_list, [[page_refs[pages + a], page_refs[pages + b]] for a, b, _ in pairs])

    @pl.when(g == pl.num_programs(1) - 1)
    def _():
        for half in range(halves):
            o_ref[0, half] = acc_sc[half] / jnp.tile(l_sc[half], (1, hh * A_DIM // LANES))


def _fox_sample(page_table, q, kn, vn, lfn, cache_kt, cache_vt, cache_lft, layer, pages=16):
    db, n_pages = page_table.shape
    groups = n_pages // pages
    assert pages % 2 == 0
    state = q.shape[1:3] + (LANES,)
    idx = np.arange(PAGE)
    tri = jnp.asarray(idx[:, None] <= idx[None, :], BF16)
    sfx = jnp.asarray(np.concatenate([idx[:, None] > idx[None, :], np.ones((PAGE, PAGE), bool)], 1), BF16)
    kv_page = (A_HEADS, A_DIM, PAGE)
    lf_page = (A_HEADS, PAGE)

    def per_b(shape):
        nd = len(shape)
        return pl.BlockSpec((1,) + shape, lambda b, g, pt: (b,) + (0,) * nd)

    def new_page(shape):
        nd = len(shape)
        return pl.BlockSpec((1, 1) + shape, lambda b, g, pt: (0, b) + (0,) * nd)

    def const(shape):
        nd = len(shape)
        return pl.BlockSpec(shape, lambda b, g, pt: (0,) * nd)

    def page_spec(shape, p):
        nd = len(shape)
        return pl.BlockSpec(
            (1, 1) + shape,
            lambda b, g, pt: (layer, pt[b * n_pages + (groups - 1 - g) * pages + p]) + (0,) * nd)

    in_specs = [per_b(q.shape[1:]), new_page(kv_page), new_page(kv_page), new_page(lf_page),
                const(tri.shape), const(sfx.shape)]
    in_specs += [page_spec(kv_page, p) for p in range(pages)]
    in_specs += [page_spec(kv_page, p) for p in range(pages)]
    in_specs += [page_spec(lf_page, p) for p in range(pages)]
    grid_spec = pltpu.PrefetchScalarGridSpec(
        num_scalar_prefetch=1,
        grid=(db, groups),
        in_specs=in_specs,
        out_specs=per_b(q.shape[1:]),
        scratch_shapes=[pltpu.VMEM(state, F32), pltpu.VMEM(state, F32),
                        pltpu.VMEM(q.shape[1:], F32), pltpu.VMEM((A_HEADS, LANES), F32)],
    )
    return pl.pallas_call(
        functools.partial(_fox_sample_kernel, pages=pages),
        grid_spec=grid_spec,
        out_shape=jax.ShapeDtypeStruct(q.shape, F32),
        compiler_params=_cparams("arbitrary", "arbitrary"),
        name="fox_sample",
    )(page_table.reshape(-1), q, kn, vn, lfn, tri, sfx,
      *([cache_kt] * pages), *([cache_vt] * pages), *([cache_lft] * pages))


def _gla_chunk(q, k, v, la, s_old, tril, row0, ones, cast):
    c = q.shape[0]
    bc = jnp.dot(tril, la, precision=HIGHEST, preferred_element_type=F32)
    ref = bc[c // 2 - 1:c // 2, :]
    last = bc[c - 1:c, :]
    qt = q * jnp.exp(bc - ref)
    kt = k * jnp.exp(ref - bc)
    o = _dot(cast(qt * jnp.exp(ref)), cast(s_old))
    a = lax.dot_general(cast(qt), cast(kt), NT, preferred_element_type=F32)
    a = jnp.where(tril > 0, a, 0.0)
    vc = cast(v)
    o = o + _dot(cast(a), vc)
    kd = cast(kt * jnp.exp(last - ref))
    scale = lax.dot_general(row0 * jnp.exp(last), ones, TN, precision=HIGHEST, preferred_element_type=F32)
    s_new = scale * s_old + lax.dot_general(kd, vc, TN, preferred_element_type=F32)
    return o, s_new


def _gla_prompt_kernel(q_ref, k_ref, v_ref, la_ref, tril_ref, o_ref, st_ref, *, chunk):
    @pl.when(pl.program_id(0) == 0)
    def _():
        st_ref[...] = jnp.zeros(st_ref.shape, F32)

    tril = tril_ref[...]
    r = lax.broadcasted_iota(jnp.int32, tril.shape, 0)
    c = lax.broadcasted_iota(jnp.int32, tril.shape, 1)
    causal = c <= r

    nb = q_ref.shape[0]
    chains = [(b, h) for b in range(nb) for h in range(B_HEADS)]
    kcols = lambda h: slice(h * B_DK, (h + 1) * B_DK)
    vcols = lambda h: slice(h * B_DV, (h + 1) * B_DV)

    def body(ci, states):
        rows = pl.ds(pl.multiple_of(ci * chunk, chunk), chunk)
        qs, kd, decay, qtb, ktb = [], [], [], [], []
        for b in range(nb):
            bc = _sel_dot(tril, la_ref[b, rows, :])
            ref = bc[chunk // 2 - 1:chunk // 2, :]
            last = bc[chunk - 1:chunk, :]
            qt = q_ref[b, rows, :] * jnp.exp(bc - ref)
            kt = k_ref[b, rows, :] * jnp.exp(ref - bc)
            qs.append((qt * jnp.exp(ref)).astype(BF16))
            kd.append((kt * jnp.exp(last - ref)).astype(BF16))
            decay.append(jnp.exp(last))
            qtb.append(qt.astype(BF16))
            ktb.append(kt.astype(BF16))
        vbs = [v_ref[b, rows, vcols(h)].astype(BF16) for b, h in chains]
        intra = []
        for (b, h), vb in zip(chains, vbs):
            a = lax.dot_general(qtb[b][:, kcols(h)], ktb[b][:, kcols(h)], NT, preferred_element_type=F32)
            intra.append(_dot(jnp.where(causal, a, 0.0).astype(BF16), vb))
        new_states = []
        for (b, h), vb, st, o_intra in zip(chains, vbs, states, intra):
            o = lax.dot_general(qs[b][:, kcols(h)], st.astype(BF16), NT, preferred_element_type=F32)
            o_ref[b, rows, vcols(h)] = o + o_intra
            new_states.append(st * decay[b][:, kcols(h)]
                              + lax.dot_general(vb, kd[b][:, kcols(h)], TN, preferred_element_type=F32))
        return tuple(new_states)

    states = lax.fori_loop(0, q_ref.shape[1] // chunk, body, tuple(st_ref[b, h] for b, h in chains))
    for (b, h), st in zip(chains, states):
        st_ref[b, h] = st


def _gla_consts(chunk):
    tril = jnp.tril(jnp.ones((chunk, chunk), F32))
    row0 = jnp.zeros((8, B_DK), F32).at[0].set(1.0)
    ones = jnp.ones((8, B_DV), F32)
    return tril, row0, ones


def _gla_prompt(qb, kb, vb, la, batch, seq, rows, chunk=64):
    kw, vw = B_HEADS * B_DK, B_HEADS * B_DV
    tril = jnp.tril(jnp.ones((chunk, chunk), BF16))
    rk = pl.BlockSpec((batch, rows, kw), lambda i: (0, i, 0))
    rv = pl.BlockSpec((batch, rows, vw), lambda i: (0, i, 0))
    return pl.pallas_call(
        functools.partial(_gla_prompt_kernel, chunk=chunk),
        grid=(seq // rows,),
        in_specs=[rk, rk, rv, rk, _resident(tril.shape)],
        out_specs=[rv, pl.BlockSpec((batch, B_HEADS, B_DV, B_DK), lambda i: (0, 0, 0, 0))],
        out_shape=[jax.ShapeDtypeStruct(vb.shape, F32),
                   jax.ShapeDtypeStruct((batch, B_HEADS, B_DV, B_DK), F32)],
        compiler_params=_cparams("arbitrary"),
        name="gla_prompt",
    )(qb, kb, vb, la, tril)


def _gla_sample_kernel(q_ref, k_ref, v_ref, la_ref, s_in_ref, tril_ref, row0_ref, ones_ref, o_ref, s_ref):
    cast = lambda a: a
    tril, row0, ones = tril_ref[...], row0_ref[...], ones_ref[...]
    for h in range(B_HEADS):
        kc = slice(h * B_DK, (h + 1) * B_DK)
        vc = slice(h * B_DV, (h + 1) * B_DV)
        o, s_new = _gla_chunk(q_ref[:, kc], k_ref[:, kc], v_ref[:, vc], la_ref[:, kc],
                              s_in_ref[0, h], tril, row0, ones, cast)
        o_ref[:, vc] = o
        s_ref[0, h] = s_new


def _gla_sample(qb, kb, vb, la, state, db, t):
    kw, vw = B_HEADS * B_DK, B_HEADS * B_DV
    consts = _gla_consts(t)
    rk = pl.BlockSpec((t, kw), lambda b: (b, 0))
    rv = pl.BlockSpec((t, vw), lambda b: (b, 0))
    st = pl.BlockSpec((1, B_HEADS, B_DK, B_DV), lambda b: (b, 0, 0, 0))
    return pl.pallas_call(
        _gla_sample_kernel,
        grid=(db,),
        in_specs=[rk, rk, rv, rk, st] + [_resident(c.shape) for c in consts],
        out_specs=[rv, st],
        out_shape=[jax.ShapeDtypeStruct(vb.shape, F32), jax.ShapeDtypeStruct(state.shape, F32)],
        compiler_params=_cparams("arbitrary"),
        name="gla_sample",
    )(qb, kb, vb, la, state, *consts)


def _mix_kernel(x_ref, oa_ref, ob_ref, gpre_ref, wr_ref, wga_ref, wgb_ref, ggla_ref,
                wpa_ref, wpb_ref, wout_ref, gpost_ref, y_ref):
    x = x_ref[...]
    hn = _rms(x, gpre_ref[...]).astype(BF16)
    r_b = _dot(hn, wr_ref[...])
    ob = ob_ref[...]
    parts = []
    for h in range(B_HEADS):
        seg = ob[:, h * B_DV:(h + 1) * B_DV]
        parts.append(_rms(seg, ggla_ref[...]))
    obn = jnp.concatenate(parts, axis=-1) * (r_b * jax.nn.sigmoid(r_b))
    pa = _dot(oa_ref[...], wpa_ref[...])
    pb = _dot(obn.astype(BF16), wpb_ref[...])
    merged = jax.nn.sigmoid(_dot(hn, wga_ref[...])) * pa + jax.nn.sigmoid(_dot(hn, wgb_ref[...])) * pb
    z = _dot(merged.astype(BF16), wout_ref[...])
    y_ref[...] = x + _rms(z, gpost_ref[...])


def _mix(x, oa, ob, g_pre, w_r, w_ga, w_gb, g_gla, w_pa, w_pb, w_out, g_post, tm):
    t, d = x.shape
    row = lambda n: pl.BlockSpec((tm, n), lambda i: (i, 0))
    consts = (g_pre, w_r, w_ga, w_gb, g_gla, w_pa, w_pb, w_out, g_post)
    return pl.pallas_call(
        _mix_kernel,
        grid=(t // tm,),
        in_specs=[row(d), row(oa.shape[1]), row(ob.shape[1])] + [_resident(c.shape) for c in consts],
        out_specs=row(d),
        out_shape=jax.ShapeDtypeStruct(x.shape, F32),
        compiler_params=_cparams("arbitrary"),
        name="mix",
    )(x, oa, ob, *consts)


def _gelu_tanh(g):
    return 0.5 * g * (1.0 + jnp.tanh(np.sqrt(2.0 / np.pi) * (g + 0.044715 * (g * g * g))))


def _ffn_body(x_ref, gpre_ref, wup_ref, wconv_ref, bconv_ref, wdown_ref, gpost_ref, y_ref, tail_ref,
              h_sc, shifted):
    x = x_ref[...]
    hn = _rms(x, gpre_ref[...]).astype(BF16)
    n_chunks = wup_ref.shape[0] // 2
    nc = wup_ref.shape[2]
    for j in range(n_chunks):
        halves = []
        for half in range(2):
            idx = half * n_chunks + j
            u = _dot(hn, wup_ref[idx])
            u1, u2 = shifted(idx, u)
            w = wconv_ref[idx]
            halves.append(bconv_ref[idx] + u2 * w[0:1] + u1 * w[1:2] + u * w[2:3])
            tail_ref[idx] = u[u.shape[0] - 8:, :]
        h_sc[:, j * nc:(j + 1) * nc] = (_gelu_tanh(halves[1]) * halves[0]).astype(BF16)
    y_ref[...] = x + _rms(_dot(h_sc[...], wdown_ref[...]), gpost_ref[...])


def _ffn_prompt_kernel(x_ref, gpre_ref, wup_ref, wconv_ref, bconv_ref, wdown_ref, gpost_ref,
                       y_ref, tail_ref, h_sc, hist_sc, *, tiles_per_seq):
    first = pl.program_id(0) % tiles_per_seq == 0

    @pl.when(first)
    def _():
        hist_sc[...] = jnp.zeros(hist_sc.shape, F32)

    def shifted(idx, u):
        rows = lax.broadcasted_iota(jnp.int32, u.shape, 0)
        p1 = hist_sc[idx, 7:8, :]
        p2 = hist_sc[idx, 6:7, :]
        u1 = jnp.where(rows == 0, p1, pltpu.roll(u, 1, axis=0))
        u2 = jnp.where(rows == 0, p2, jnp.where(rows == 1, p1, pltpu.roll(u, 2, axis=0)))
        hist_sc[idx] = u[u.shape[0] - 8:, :]
        return u1, u2

    _ffn_body(x_ref, gpre_ref, wup_ref, wconv_ref, bconv_ref, wdown_ref, gpost_ref, y_ref,
              tail_ref.at[0], h_sc, shifted)


def _ffn_sample_kernel(x_ref, h1_ref, h2_ref, gpre_ref, wup_ref, wconv_ref, bconv_ref, wdown_ref,
                       gpost_ref, y_ref, u_ref, h_sc, *, t):
    x = x_ref[...]
    hn = _rms(x, gpre_ref[...]).astype(BF16)
    n_chunks = wup_ref.shape[0] // 2
    nc = wup_ref.shape[2]
    for j in range(n_chunks):
        halves = []
        for half in range(2):
            idx = half * n_chunks + j
            u = _dot(hn, wup_ref[idx])
            pos = lax.broadcasted_iota(jnp.int32, u.shape, 0) & (t - 1)
            u1 = jnp.where(pos < 1, h1_ref[idx], pltpu.roll(u, 1, axis=0))
            u2 = jnp.where(pos < 2, h2_ref[idx], pltpu.roll(u, 2, axis=0))
            w = wconv_ref[idx]
            halves.append(bconv_ref[idx] + u2 * w[0:1] + u1 * w[1:2] + u * w[2:3])
            u_ref[idx] = u
        h_sc[:, j * nc:(j + 1) * nc] = (_gelu_tanh(halves[1]) * halves[0]).astype(BF16)
    y_ref[...] = x + _rms(_dot(h_sc[...], wdown_ref[...]), gpost_ref[...])


def _ffn_prompt(x, g_pre, wup, wconv, bconv, wdown, g_post, seq, tm):
    t, d = x.shape
    nblk, _, nc = wup.shape
    tiles = t // tm
    row = pl.BlockSpec((tm, d), lambda i: (i, 0))
    consts = (g_pre, wup, wconv, bconv, wdown, g_post)
    return pl.pallas_call(
        functools.partial(_ffn_prompt_kernel, tiles_per_seq=seq // tm),
        grid=(tiles,),
        in_specs=[row] + [_resident(c.shape) for c in consts],
        out_specs=[row, pl.BlockSpec((1, nblk, 8, nc), lambda i: (i, 0, 0, 0))],
        out_shape=[jax.ShapeDtypeStruct(x.shape, F32), jax.ShapeDtypeStruct((tiles, nblk, 8, nc), F32)],
        scratch_shapes=[pltpu.VMEM((tm, wdown.shape[0]), BF16), pltpu.VMEM((nblk, 8, nc), F32)],
        compiler_params=_cparams("arbitrary"),
        name="ffn_prompt",
    )(x, *consts)


def _ffn_sample(x, h1, h2, g_pre, wup, wconv, bconv, wdown, g_post, t):
    rows, d = x.shape
    nblk, _, nc = wup.shape
    consts = (g_pre, wup, wconv, bconv, wdown, g_post)
    return pl.pallas_call(
        functools.partial(_ffn_sample_kernel, t=t),
        grid=(1,),
        in_specs=[_resident(a.shape) for a in (x, h1, h2) + consts],
        out_specs=[pl.BlockSpec(x.shape, lambda i: (0, 0)),
                   pl.BlockSpec((nblk, rows, nc), lambda i: (0, 0, 0))],
        out_shape=[jax.ShapeDtypeStruct(x.shape, F32), jax.ShapeDtypeStruct((nblk, rows, nc), F32)],
        scratch_shapes=[pltpu.VMEM((rows, wdown.shape[0]), BF16)],
        compiler_params=_cparams("arbitrary"),
        name="ffn_sample",
    )(x, h1, h2, *consts)


def _pack_layer(w_in, w_alpha2, w_up, w_conv, b_conv, w_down, nc):
    aw, kw, vw = A_HEADS * A_DIM, B_HEADS * B_DK, B_HEADS * B_DV
    d = w_in.shape[0]
    offs = np.cumsum([aw, aw, aw, A_HEADS, kw, kw, vw, GATE_RANK, vw, d, d])[:-1].tolist()
    q_a, k_a, v_a, f_a, q_b, k_b, v_b, a_low, r_b, g_a, g_b = jnp.split(w_in, offs, axis=1)
    wbig = jnp.concatenate([q_a, k_a, v_a, q_b, k_b, v_b], axis=1).astype(BF16)
    pad = lambda a: jnp.pad(a, ((0, 0), (0, LANES - a.shape[1])))
    wsm = jnp.concatenate([pad(f_a), pad(a_low)], axis=1).astype(BF16)
    wa2 = jnp.pad(w_alpha2, ((0, LANES - GATE_RANK), (0, 0))).astype(BF16)
    dff = w_down.shape[0]
    n_chunks = dff // nc
    wup = w_up.reshape(d, 2 * n_chunks, nc).transpose(1, 0, 2).astype(BF16)
    wconv = w_conv.reshape(w_conv.shape[0], 2 * n_chunks, nc).transpose(1, 0, 2)
    bconv = b_conv.reshape(2 * n_chunks, 1, nc)
    wdown = w_down.astype(BF16)
    return dict(wbig=wbig, wsm=wsm, wa2=wa2, w_r=r_b.astype(BF16), w_ga=g_a.astype(BF16),
                w_gb=g_b.astype(BF16), wup=wup, wconv=wconv, bconv=bconv, wdown=wdown)


def kernel(x_prompt, x_sample, cache_k, cache_v, cache_logf, state_gla, state_conv, page_table,
           g_pre_mix, w_in, b_f, w_alpha2, b_alpha, g_gla, w_proj_a, w_proj_b, w_out, g_post_mix,
           g_pre_ffn, w_up, w_conv, b_conv, w_down, g_post_ffn):
    batch, seq, d = x_prompt.shape
    db, dt, _ = x_sample.shape
    depth = w_in.shape[0]
    n_pool = cache_k.shape[1]
    aw = A_HEADS * A_DIM
    dff2 = w_up.shape[2]
    nc = 256
    assert dt == 8 and depth >= 1

    yp = x_prompt.reshape(batch * seq, d)
    ys = x_sample.reshape(db * dt, d)
    outs = [[] for _ in range(10)]
    for l in range(depth):
        w = _pack_layer(w_in[l], w_alpha2[l], w_up[l], w_conv[l], b_conv[l], w_down[l], nc)
        row = lambda a: a[l].reshape(1, -1)
        inproj_w = (row(g_pre_mix), w["wbig"], w["wsm"], row(b_f), w["wa2"], row(b_alpha))
        mix_w = (row(g_pre_mix), w["w_r"], w["w_ga"], w["w_gb"], row(g_gla), w_proj_a[l].astype(BF16),
                 w_proj_b[l].astype(BF16), w_out[l].astype(BF16), row(g_post_mix))
        ffn_w = (row(g_pre_ffn), w["wup"], w["wconv"], w["bconv"], w["wdown"], row(g_post_ffn))

        qa, ka, va, kaf, vaf, logf, qb, kb, vb, la = _inproj(yp, *inproj_w, seq=seq, tm=512)
        lft = logf.reshape(batch, seq, A_HEADS).transpose(0, 2, 1)
        oa = _fox_prompt(qa, ka, va, _key_bias_lanes(_cumsum_lanes(lft)), batch, seq, tq=1024, tk=512)
        seqs = lambda a: a.reshape(batch, seq, a.shape[1])
        ob, gla_state_t = _gla_prompt(seqs(qb), seqs(kb), seqs(vb), seqs(la), batch, seq, rows=512)
        ob = ob.reshape(batch * seq, ob.shape[2])
        gla_state = gla_state_t.transpose(0, 1, 3, 2)
        x1 = _mix(yp, oa, ob, *mix_w, tm=512)
        yp, tails = _ffn_prompt(x1, *ffn_w, seq=seq, tm=512)
        tiles_per_seq = seq // 512
        tails = tails.reshape(batch, tiles_per_seq, 2 * (dff2 // 2 // nc), 8, nc)[:, -1, :, 6:, :]
        conv_p = tails.transpose(0, 2, 1, 3).reshape(batch, 2, dff2)
        for lst, val in zip(outs[:5], (kaf.transpose(0, 3, 1, 2), vaf.transpose(0, 3, 1, 2),
                                       logf.reshape(batch, seq, A_HEADS), gla_state, conv_p)):
            lst.append(val)

        qa, ka, va, kaf, vaf, logf, qb, kb, vb, la = _inproj(ys, *inproj_w, seq=db * dt, tm=db * dt)
        kaf, vaf = (a.reshape(A_HEADS, A_DIM, db, dt).transpose(2, 0, 1, 3) for a in (kaf, vaf))
        hh = A_HEADS // 2
        eye = jnp.eye(hh, dtype=BF16)
        q_bd = jnp.einsum("btgld,lm->bgltmd", qa.reshape(db, dt, 2, hh, A_DIM), eye).reshape(
            db, 2, hh * dt, hh * A_DIM)
        padk = lambda a: jnp.pad(a[None], ((0, 0), (0, 0), (0, 0), (0, 0), (0, PAGE - dt)))
        lfn = jnp.pad(logf.reshape(1, db, dt, A_HEADS).transpose(0, 1, 3, 2),
                      ((0, 0), (0, 0), (0, 0), (0, PAGE - dt)))
        o_bd = _fox_sample(page_table, q_bd, padk(kaf), padk(vaf), lfn,
                           cache_k.transpose(0, 1, 3, 4, 2), cache_v.transpose(0, 1, 3, 4, 2),
                           cache_logf.transpose(0, 1, 3, 2), layer=l)
        oa = jnp.einsum("bgltmd,lm->btgld", o_bd.reshape(db, 2, hh, dt, hh, A_DIM),
                        jnp.eye(hh, dtype=F32)).reshape(db * dt, aw).astype(BF16)
        ob, gla_state = _gla_sample(qb, kb, vb, la, state_gla[l], db, dt)
        x1 = _mix(ys, oa, ob, *mix_w, tm=db * dt)
        st = state_conv[l].astype(F32)
        zeros = jnp.zeros((db, dt, dff2), F32)
        h1 = zeros.at[:, 0].set(st[:, 1])
        h2 = zeros.at[:, 0].set(st[:, 0]).at[:, 1].set(st[:, 1])
        chunked = lambda a: a.reshape(db * dt, dff2 // nc, nc).transpose(1, 0, 2)
        ys, u = _ffn_sample(x1, chunked(h1), chunked(h2), *ffn_w, t=dt)
        conv_s = u.transpose(1, 0, 2).reshape(db, dt, dff2)[:, dt - 2:, :]
        for lst, val in zip(outs[5:], (kaf.transpose(0, 3, 1, 2), vaf.transpose(0, 3, 1, 2),
                                       logf.reshape(db, dt, A_HEADS), gla_state, conv_s)):
            lst.append(val)

    stacked = [jnp.stack(o) for o in outs]
    return (yp.reshape(batch, seq, d), ys.reshape(db, dt, d), *stacked)
```

```python
import functools

import jax
import jax.numpy as jnp
import numpy as np
from jax import lax
from jax.experimental import pallas as pl
from jax.experimental.pallas import tpu as pltpu

F32 = jnp.float32
BF16 = jnp.bfloat16
EPS = 1e-6
GATE_TEMP = 16.0
PAGE = 128
A_HEADS, A_DIM = 8, 64
B_HEADS, B_DK, B_DV = 4, 128, 256
GATE_RANK = 16
LANES = 128
VMEM_LIMIT = 56 * 1024 * 1024
HIGHEST = lax.Precision.HIGHEST
LOG2E = float(np.log2(np.e))
NT = (((1,), (1,)), ((), ()))
TN = (((0,), (0,)), ((), ()))


def _cparams(*sem):
    return pltpu.CompilerParams(dimension_semantics=sem, vmem_limit_bytes=VMEM_LIMIT)


def _resident(shape):
    nd = len(shape)
    return pl.BlockSpec(shape, lambda *_: (0,) * nd, pipeline_mode=pl.Buffered(1))


def _rms(x, g):
    return x * lax.rsqrt(jnp.mean(x * x, axis=-1, keepdims=True) + EPS) * g


def _log_sigmoid(x):
    return jnp.minimum(x, 0.0) - jnp.log1p(jnp.exp(-jnp.abs(x)))


def _dot(a, b):
    return jnp.dot(a, b, preferred_element_type=F32)


def _trunc_to_bf16_bits(x):
    bits = lax.bitcast_convert_type(x, jnp.int32) & jnp.int32(-65536)
    return lax.bitcast_convert_type(bits, F32)


def _split3(x):
    hi = _trunc_to_bf16_bits(x)
    r1 = x - hi
    mid = _trunc_to_bf16_bits(r1)
    lo = r1 - mid
    return hi.astype(BF16), mid.astype(BF16), lo.astype(BF16)


def _dot_sel(x, sel):
    return sum(_dot(part, sel) for part in _split3(x))


def _sel_dot(sel, x):
    return sum(_dot(sel, part) for part in _split3(x))


def _inproj_kernel(x_ref, g_ref, wbig_ref, wsm_ref, bf_ref, wa2_ref, ba_ref,
                   qa_ref, ka_ref, va_ref, kaf_ref, vaf_ref, logf_ref,
                   qb_ref, kb_ref, vb_ref, la_ref):
    hn = _rms(x_ref[...], g_ref[...]).astype(BF16)
    aw = A_HEADS * A_DIM
    kw = B_HEADS * B_DK

    def mm(lo, hi):
        return _dot(hn, wbig_ref[:, lo:hi])

    qa_ref[...] = (mm(0, aw) * (A_DIM ** -0.5 * LOG2E)).astype(BF16)
    k_a = mm(aw, 2 * aw)
    kaf_ref[0] = k_a.T.reshape(A_HEADS, A_DIM, k_a.shape[0])
    ka_ref[...] = k_a.astype(BF16)
    v_a = mm(2 * aw, 3 * aw)
    vaf_ref[0] = v_a.T.reshape(A_HEADS, A_DIM, v_a.shape[0])
    va_ref[...] = v_a.astype(BF16)
    o = 3 * aw
    qb_ref[...] = mm(o, o + kw) * B_DK ** -0.5
    kb_ref[...] = mm(o + kw, o + 2 * kw)
    vb_ref[...] = mm(o + 2 * kw, o + 2 * kw + B_HEADS * B_DV)
    us = _dot(hn, wsm_ref[...])
    logf_ref[...] = _log_sigmoid(us[:, :A_HEADS] + bf_ref[...])
    z = _dot(us[:, LANES:].astype(BF16), wa2_ref[...]) + ba_ref[...]
    la_ref[...] = _log_sigmoid(z) / GATE_TEMP


def _inproj(x, g_pre, wbig, wsm, b_f, wa2, b_alpha, seq, tm):
    t, d = x.shape
    aw, kw, vw = A_HEADS * A_DIM, B_HEADS * B_DK, B_HEADS * B_DV
    row = lambda n: pl.BlockSpec((tm, n), lambda i: (i, 0))
    outs = [(aw, BF16)] * 3 + [(aw, F32)] * 2 + [(A_HEADS, F32), (kw, F32), (kw, F32), (vw, F32), (kw, F32)]
    out_specs = [row(n) for n, _ in outs]
    out_shape = [jax.ShapeDtypeStruct((t, n), dt) for n, dt in outs]
    tiles_per_seq = seq // tm
    for i in (3, 4):
        out_specs[i] = pl.BlockSpec((1, A_HEADS, A_DIM, tm),
                                    lambda i: (i // tiles_per_seq, 0, 0, i % tiles_per_seq))
        out_shape[i] = jax.ShapeDtypeStruct((t // seq, A_HEADS, A_DIM, seq), F32)
    return pl.pallas_call(
        _inproj_kernel,
        grid=(t // tm,),
        in_specs=[row(d), _resident(g_pre.shape), _resident(wbig.shape), _resident(wsm.shape),
                  _resident(b_f.shape), _resident(wa2.shape), _resident(b_alpha.shape)],
        out_specs=out_specs,
        out_shape=out_shape,
        compiler_params=_cparams("arbitrary"),
        name="inproj",
    )(x, g_pre, wbig, wsm, b_f, wa2, b_alpha)


def _key_bias_kernel(lf_ref, tril_ref, place_ref, o_ref, carry_sc, *, seg):
    @pl.when(pl.program_id(1) == 0)
    def _():
        carry_sc[...] = jnp.zeros(carry_sc.shape, F32)

    carry = carry_sc[...]
    for s in range(lf_ref.shape[0] // seg):
        rows = slice(s * seg, (s + 1) * seg)
        c = _sel_dot(tril_ref[...], lf_ref[rows, :]) + carry
        carry = c[seg - 1:seg, :]
        x = -c * LOG2E
        hi = _trunc_to_bf16_bits(x)
        r1 = x - hi
        mid = _trunc_to_bf16_bits(r1)
        parts = (hi, mid, r1 - mid)
        o_ref[rows, :] = sum(_dot(p, place_ref[j]) for j, p in enumerate(parts)).astype(o_ref.dtype)
    carry_sc[...] = carry


def _key_bias_lanes(logf, batch, seq, rows=2048, seg=256):
    heads = logf.shape[1]
    tril = jnp.tril(jnp.ones((seg, seg), BF16))
    place = np.zeros((3, heads, heads // 2 * LANES), np.float32)
    for j in range(3):
        for h in range(heads):
            place[j, h, (h // 2) * LANES + (h % 2) * 3 + j] = 1.0
    place = jnp.asarray(place)
    steps = seq // rows
    return pl.pallas_call(
        functools.partial(_key_bias_kernel, seg=seg),
        grid=(batch, steps),
        in_specs=[pl.BlockSpec((rows, heads), lambda b, i: (b * steps + i, 0)),
                  _resident(tril.shape), _resident(place.shape)],
        out_specs=pl.BlockSpec((rows, place.shape[2]), lambda b, i: (b * steps + i, 0)),
        out_shape=jax.ShapeDtypeStruct((batch * seq, place.shape[2]), BF16),
        scratch_shapes=[pltpu.VMEM((1, heads), F32)],
        compiler_params=_cparams("arbitrary", "arbitrary"),
        name="key_bias",
    )(logf, tril, place)


def _fox_prompt_kernel(q_ref, k_ref, v_ref, kb_ref, o_ref, m_sc, l_sc, acc_sc, *, tq, tk):
    i = pl.program_id(2)
    q = q_ref[...]
    lane = lax.broadcasted_iota(jnp.int32, q.shape, 1)
    zero = jnp.zeros_like(q)
    sel_even = jnp.where(lane < 3, 1.0, 0.0).astype(BF16)
    sel_odd = jnp.where(lane < 3, 0.0, jnp.where(lane < 6, 1.0, 0.0)).astype(BF16)
    q_even = jnp.concatenate([jnp.where(lane < A_DIM, q, zero), sel_even], axis=1)
    q_odd = jnp.concatenate([jnp.where(lane >= A_DIM, q, zero), sel_odd], axis=1)
    qs = jnp.concatenate([q_even, q_odd], axis=0)
    m_sc[...] = jnp.full(m_sc.shape, -jnp.inf, F32)
    l_sc[...] = jnp.zeros(l_sc.shape, F32)
    acc_sc[...] = jnp.zeros(acc_sc.shape, F32)
    reps = tk // LANES

    def block(j, masked):
        start = pl.multiple_of(j * tk, tk)
        ka = jnp.concatenate([k_ref[pl.ds(start, tk), :], kb_ref[pl.ds(start, tk), :]], axis=1)
        s = lax.dot_general(qs, ka, NT, preferred_element_type=F32)
        if masked:
            r = lax.broadcasted_iota(jnp.int32, s.shape, 0)
            c = lax.broadcasted_iota(jnp.int32, s.shape, 1)
            qpos = i * tq + jnp.where(r >= tq, r - tq, r)
            s = jnp.where(start + c <= qpos, s, -jnp.inf)
        m_old = m_sc[...]
        m_new = jnp.maximum(m_old, jnp.max(s, axis=-1, keepdims=True))
        alpha = jnp.exp2(m_old - m_new)
        p = jnp.exp2(s - jnp.tile(m_new, (1, reps)))
        l_sc[...] = alpha * l_sc[...] + jnp.sum(p, axis=-1, keepdims=True)
        acc_sc[...] = alpha * acc_sc[...] + _dot(p.astype(BF16), v_ref[pl.ds(start, tk), :])
        m_sc[...] = m_new

    n_full = i * (tq // tk)

    per_body = 4

    def body(jj, carry):
        for u in range(per_body):
            block(per_body * jj + u, False)
        return carry

    lax.fori_loop(0, n_full // per_body, body, 0)
    left = n_full % per_body
    for bit in (2, 1):
        @pl.when((left & bit) != 0)
        def _(bit=bit):
            first = n_full - (left & (2 * bit - 1))
            for u in range(bit):
                block(first + u, False)

    for d in range(tq // tk):
        block(n_full + d, True)
    o = acc_sc[...] / l_sc[...]
    o_ref[...] = jnp.where(lane < A_DIM, o[:tq], o[tq:]).astype(o_ref.dtype)


def _fox_prompt(qa, ka, va, kbias, batch, seq, tq, tk):
    assert tq % tk == 0
    nq = seq // tq
    pairs = A_HEADS // 2
    kv_spec = pl.BlockSpec((seq, LANES), lambda b, p, i: (b, p))
    q_spec = pl.BlockSpec((tq, LANES), lambda b, p, i: (b * nq + i, p))
    return pl.pallas_call(
        functools.partial(_fox_prompt_kernel, tq=tq, tk=tk),
        grid=(batch, pairs, nq),
        in_specs=[q_spec, kv_spec, kv_spec, kv_spec],
        out_specs=q_spec,
        out_shape=jax.ShapeDtypeStruct(qa.shape, BF16),
        scratch_shapes=[pltpu.VMEM((2 * tq, LANES), F32), pltpu.VMEM((2 * tq, LANES), F32),
                        pltpu.VMEM((2 * tq, LANES), F32)],
        compiler_params=_cparams("arbitrary", "arbitrary", "arbitrary"),
        name="fox_prompt",
    )(qa, ka, va, kbias)


def _fox_sample_kernel(pt_ref, q_ref, kn_ref, vn_ref, lfn_ref, tri_ref, sfx_ref, *rest, pages):
    page_refs = rest[:3 * pages]
    o_ref, m_sc, l_sc, acc_sc, carry_sc = rest[3 * pages:]
    g = pl.program_id(1)
    halves = 2
    hh = A_HEADS // halves
    nq = q_ref.shape[2] // hh

    def tiles(refs, half):
        return jnp.concatenate(
            [r[0, 0, half * hh:(half + 1) * hh].reshape(hh * A_DIM, PAGE).astype(BF16) for r in refs], axis=1)

    def rows_of(bias, half):
        return jnp.concatenate(
            [jnp.broadcast_to(bias[h:h + 1, :], (nq, bias.shape[1])) for h in range(half * hh, (half + 1) * hh)],
            axis=0) * LOG2E

    def update(half, s_list, v_groups):
        m_old = m_sc[half]
        m_row = jnp.max(s_list[0], axis=-1, keepdims=True)
        for s in s_list[1:]:
            m_row = jnp.maximum(m_row, jnp.max(s, axis=-1, keepdims=True))
        m_new = jnp.maximum(m_old, m_row)
        alpha = jnp.exp2(m_old - m_new)
        l_row = None
        pv = None
        for s, v_refs in zip(s_list, v_groups):
            p = jnp.exp2(s - jnp.tile(m_new, (1, s.shape[1] // LANES)))
            ls = jnp.sum(p, axis=-1, keepdims=True)
            l_row = ls if l_row is None else l_row + ls
            d = lax.dot_general(p.astype(BF16), tiles(v_refs, half), NT, preferred_element_type=F32)
            pv = d if pv is None else pv + d
        l_sc[half] = alpha * l_sc[half] + l_row
        acc_sc[half] = jnp.tile(alpha, (1, hh * A_DIM // LANES)) * acc_sc[half] + pv
        m_sc[half] = m_new

    @pl.when(g == 0)
    def _():
        m_sc[...] = jnp.full(m_sc.shape, -jnp.inf, F32)
        l_sc[...] = jnp.zeros(l_sc.shape, F32)
        acc_sc[...] = jnp.zeros(acc_sc.shape, F32)
        carry_sc[...] = jnp.zeros(carry_sc.shape, F32)
        lfn = lfn_ref[0, 0]
        cn = _dot_sel(jnp.concatenate([lfn, jnp.zeros_like(lfn)], axis=0), tri_ref[...])[:A_HEADS]
        for half in range(halves):
            s = _dot(q_ref[0, half], tiles([kn_ref], half)) - rows_of(cn, half)
            r = lax.broadcasted_iota(jnp.int32, s.shape, 0)
            c = lax.broadcasted_iota(jnp.int32, s.shape, 1)
            s = jnp.where(c <= (r & (nq - 1)), s, -jnp.inf)
            update(half, [s], [[vn_ref]])

    order = list(reversed(range(pages)))
    lf_all = jnp.concatenate([page_refs[2 * pages + p][0, 0] for p in order], axis=0)
    sfx_all = _dot_sel(lf_all, sfx_ref[...])
    carry = carry_sc[...]
    biases = []
    for i in range(pages):
        blk = sfx_all[i * A_HEADS:(i + 1) * A_HEADS]
        biases.append(blk[:, :PAGE] + carry)
        carry = carry + blk[:, PAGE:]
    carry_sc[...] = carry
    pairs = [(order[i], order[i + 1], jnp.concatenate(biases[i:i + 2], axis=1)) for i in range(0, pages, 2)]
    for half in range(halves):
        s_list = [_dot(q_ref[0, half], tiles([page_refs[a], page_refs[b]], half)) + rows_of(bias, half)
                  for a, b, bias in pairs]
        update(half, s_list, [[page_refs[pages + a], page_refs[pages + b]] for a, b, _ in pairs])

    @pl.when(g == pl.num_programs(1) - 1)
    def _():
        for half in range(halves):
            o_ref[0, half] = acc_sc[half] / jnp.tile(l_sc[half], (1, hh * A_DIM // LANES))


def _fox_sample(page_table, q, kn, vn, lfn, cache_kt, cache_vt, cache_lft, layer, pages=16):
    db, n_pages = page_table.shape
    groups = n_pages // pages
    assert pages % 2 == 0
    state = q.shape[1:3] + (LANES,)
    idx = np.arange(PAGE)
    tri = jnp.asarray(idx[:, None] <= idx[None, :], BF16)
    sfx = jnp.asarray(np.concatenate([idx[:, None] > idx[None, :], np.ones((PAGE, PAGE), bool)], 1), BF16)
    kv_page = (A_HEADS, A_DIM, PAGE)
    lf_page = (A_HEADS, PAGE)

    def per_b(shape):
        nd = len(shape)
        return pl.BlockSpec((1,) + shape, lambda b, g, pt: (b,) + (0,) * nd)

    def new_page(shape):
        nd = len(shape)
        return pl.BlockSpec((1, 1) + shape, lambda b, g, pt: (0, b) + (0,) * nd)

    def const(shape):
        nd = len(shape)
        return pl.BlockSpec(shape, lambda b, g, pt: (0,) * nd)

    def page_spec(shape, p):
        nd = len(shape)
        return pl.BlockSpec(
            (1, 1) + shape,
            lambda b, g, pt: (layer, pt[b * n_pages + (groups - 1 - g) * pages + p]) + (0,) * nd)

    in_specs = [per_b(q.shape[1:]), new_page(kv_page), new_page(kv_page), new_page(lf_page),
                const(tri.shape), const(sfx.shape)]
    in_specs += [page_spec(kv_page, p) for p in range(pages)]
    in_specs += [page_spec(kv_page, p) for p in range(pages)]
    in_specs += [page_spec(lf_page, p) for p in range(pages)]
    grid_spec = pltpu.PrefetchScalarGridSpec(
        num_scalar_prefetch=1,
        grid=(db, groups),
        in_specs=in_specs,
        out_specs=per_b(q.shape[1:]),
        scratch_shapes=[pltpu.VMEM(state, F32), pltpu.VMEM(state, F32),
                        pltpu.VMEM(q.shape[1:], F32), pltpu.VMEM((A_HEADS, LANES), F32)],
    )
    return pl.pallas_call(
        functools.partial(_fox_sample_kernel, pages=pages),
        grid_spec=grid_spec,
        out_shape=jax.ShapeDtypeStruct(q.shape, F32),
        compiler_params=_cparams("arbitrary", "arbitrary"),
        name="fox_sample",
    )(page_table.reshape(-1), q, kn, vn, lfn, tri, sfx,
      *([cache_kt] * pages), *([cache_vt] * pages), *([cache_lft] * pages))


def _gla_chunk(q, k, v, la, s_old, tril, row0, ones, cast):
    c = q.shape[0]
    bc = jnp.dot(tril, la, precision=HIGHEST, preferred_element_type=F32)
    ref = bc[c // 2 - 1:c // 2, :]
    last = bc[c - 1:c, :]
    qt = q * jnp.exp(bc - ref)
    kt = k * jnp.exp(ref - bc)
    o = _dot(cast(qt * jnp.exp(ref)), cast(s_old))
    a = lax.dot_general(cast(qt), cast(kt), NT, preferred_element_type=F32)
    a = jnp.where(tril > 0, a, 0.0)
    vc = cast(v)
    o = o + _dot(cast(a), vc)
    kd = cast(kt * jnp.exp(last - ref))
    scale = lax.dot_general(row0 * jnp.exp(last), ones, TN, precision=HIGHEST, preferred_element_type=F32)
    s_new = scale * s_old + lax.dot_general(kd, vc, TN, preferred_element_type=F32)
    return o, s_new


def _gla_prompt_kernel(q_ref, k_ref, v_ref, la_ref, tril_ref, o_ref, st_ref, *, chunk):
    @pl.when(pl.program_id(0) == 0)
    def _():
        st_ref[...] = jnp.zeros(st_ref.shape, F32)

    tril = tril_ref[...]
    r = lax.broadcasted_iota(jnp.int32, tril.shape, 0)
    c = lax.broadcasted_iota(jnp.int32, tril.shape, 1)
    causal = c <= r

    nb = q_ref.shape[0]
    chains = [(b, h) for b in range(nb) for h in range(B_HEADS)]
    kcols = lambda h: slice(h * B_DK, (h + 1) * B_DK)
    vcols = lambda h: slice(h * B_DV, (h + 1) * B_DV)

    def body(ci, states):
        rows = pl.ds(pl.multiple_of(ci * chunk, chunk), chunk)
        qs, kd, decay, qtb, ktb = [], [], [], [], []
        for b in range(nb):
            bc = _sel_dot(tril, la_ref[b, rows, :])
            ref = bc[chunk // 2 - 1:chunk // 2, :]
            last = bc[chunk - 1:chunk, :]
            qt = q_ref[b, rows, :] * jnp.exp(bc - ref)
            kt = k_ref[b, rows, :] * jnp.exp(ref - bc)
            qs.append((qt * jnp.exp(ref)).astype(BF16))
            kd.append((kt * jnp.exp(last - ref)).astype(BF16))
            decay.append(jnp.exp(last))
            qtb.append(qt.astype(BF16))
            ktb.append(kt.astype(BF16))
        vbs = [v_ref[b, rows, vcols(h)].astype(BF16) for b, h in chains]
        intra = []
        for (b, h), vb in zip(chains, vbs):
            a = lax.dot_general(qtb[b][:, kcols(h)], ktb[b][:, kcols(h)], NT, preferred_element_type=F32)
            intra.append(_dot(jnp.where(causal, a, 0.0).astype(BF16), vb))
        new_states = []
        for (b, h), vb, st, o_intra in zip(chains, vbs, states, intra):
            o = lax.dot_general(qs[b][:, kcols(h)], st.astype(BF16), NT, preferred_element_type=F32)
            o_ref[b, rows, vcols(h)] = o + o_intra
            new_states.append(st * decay[b][:, kcols(h)]
                              + lax.dot_general(vb, kd[b][:, kcols(h)], TN, preferred_element_type=F32))
        return tuple(new_states)

    states = lax.fori_loop(0, q_ref.shape[1] // chunk, body, tuple(st_ref[b, h] for b, h in chains))
    for (b, h), st in zip(chains, states):
        st_ref[b, h] = st


def _gla_consts(chunk):
    tril = jnp.tril(jnp.ones((chunk, chunk), F32))
    row0 = jnp.zeros((8, B_DK), F32).at[0].set(1.0)
    ones = jnp.ones((8, B_DV), F32)
    return tril, row0, ones


def _gla_prompt(qb, kb, vb, la, batch, seq, rows, chunk=64):
    kw, vw = B_HEADS * B_DK, B_HEADS * B_DV
    tril = jnp.tril(jnp.ones((chunk, chunk), BF16))
    rk = pl.BlockSpec((batch, rows, kw), lambda i: (0, i, 0))
    rv = pl.BlockSpec((batch, rows, vw), lambda i: (0, i, 0))
    return pl.pallas_call(
        functools.partial(_gla_prompt_kernel, chunk=chunk),
        grid=(seq // rows,),
        in_specs=[rk, rk, rv, rk, _resident(tril.shape)],
        out_specs=[rv, pl.BlockSpec((batch, B_HEADS, B_DV, B_DK), lambda i: (0, 0, 0, 0))],
        out_shape=[jax.ShapeDtypeStruct(vb.shape, F32),
                   jax.ShapeDtypeStruct((batch, B_HEADS, B_DV, B_DK), F32)],
        compiler_params=_cparams("arbitrary"),
        name="gla_prompt",
    )(qb, kb, vb, la, tril)


def _gla_sample_kernel(q_ref, k_ref, v_ref, la_ref, s_in_ref, tril_ref, row0_ref, ones_ref, o_ref, s_ref):
    cast = lambda a: a
    tril, row0, ones = tril_ref[...], row0_ref[...], ones_ref[...]
    for h in range(B_HEADS):
        kc = slice(h * B_DK, (h + 1) * B_DK)
        vc = slice(h * B_DV, (h + 1) * B_DV)
        o, s_new = _gla_chunk(q_ref[:, kc], k_ref[:, kc], v_ref[:, vc], la_ref[:, kc],
                              s_in_ref[0, h], tril, row0, ones, cast)
        o_ref[:, vc] = o
        s_ref[0, h] = s_new


def _gla_sample(qb, kb, vb, la, state, db, t):
    kw, vw = B_HEADS * B_DK, B_HEADS * B_DV
    consts = _gla_consts(t)
    rk = pl.BlockSpec((t, kw), lambda b: (b, 0))
    rv = pl.BlockSpec((t, vw), lambda b: (b, 0))
    st = pl.BlockSpec((1, B_HEADS, B_DK, B_DV), lambda b: (b, 0, 0, 0))
    return pl.pallas_call(
        _gla_sample_kernel,
        grid=(db,),
        in_specs=[rk, rk, rv, rk, st] + [_resident(c.shape) for c in consts],
        out_specs=[rv, st],
        out_shape=[jax.ShapeDtypeStruct(vb.shape, F32), jax.ShapeDtypeStruct(state.shape, F32)],
        compiler_params=_cparams("arbitrary"),
        name="gla_sample",
    )(qb, kb, vb, la, state, *consts)


def _mix_kernel(x_ref, oa_ref, ob_ref, gpre_ref, wr_ref, wga_ref, wgb_ref, ggla_ref,
                wpa_ref, wpb_ref, wout_ref, gpost_ref, y_ref):
    x = x_ref[...]
    hn = _rms(x, gpre_ref[...]).astype(BF16)
    r_b = _dot(hn, wr_ref[...])
    ob = ob_ref[...]
    parts = []
    for h in range(B_HEADS):
        seg = ob[:, h * B_DV:(h + 1) * B_DV]
        parts.append(_rms(seg, ggla_ref[...]))
    obn = jnp.concatenate(parts, axis=-1) * (r_b * jax.nn.sigmoid(r_b))
    pa = _dot(oa_ref[...], wpa_ref[...])
    pb = _dot(obn.astype(BF16), wpb_ref[...])
    merged = jax.nn.sigmoid(_dot(hn, wga_ref[...])) * pa + jax.nn.sigmoid(_dot(hn, wgb_ref[...])) * pb
    z = _dot(merged.astype(BF16), wout_ref[...])
    y_ref[...] = x + _rms(z, gpost_ref[...])


def _mix(x, oa, ob, g_pre, w_r, w_ga, w_gb, g_gla, w_pa, w_pb, w_out, g_post, tm):
    t, d = x.shape
    row = lambda n: pl.BlockSpec((tm, n), lambda i: (i, 0))
    consts = (g_pre, w_r, w_ga, w_gb, g_gla, w_pa, w_pb, w_out, g_post)
    return pl.pallas_call(
        _mix_kernel,
        grid=(t // tm,),
        in_specs=[row(d), row(oa.shape[1]), row(ob.shape[1])] + [_resident(c.shape) for c in consts],
        out_specs=row(d),
        out_shape=jax.ShapeDtypeStruct(x.shape, F32),
        compiler_params=_cparams("arbitrary"),
        name="mix",
    )(x, oa, ob, *consts)


def _gelu_tanh(g):
    return 0.5 * g * (1.0 + jnp.tanh(np.sqrt(2.0 / np.pi) * (g + 0.044715 * (g * g * g))))


def _ffn_body(x_ref, gpre_ref, wup_ref, wconv_ref, bconv_ref, wdown_ref, gpost_ref, y_ref, tail_ref,
              h_sc, shifted):
    x = x_ref[...]
    hn = _rms(x, gpre_ref[...]).astype(BF16)
    n_chunks = wup_ref.shape[0] // 2
    nc = wup_ref.shape[2]
    for j in range(n_chunks):
        halves = []
        for half in range(2):
            idx = half * n_chunks + j
            u = _dot(hn, wup_ref[idx])
            u1, u2 = shifted(idx, u)
            w = wconv_ref[idx]
            halves.append(bconv_ref[idx] + u2 * w[0:1] + u1 * w[1:2] + u * w[2:3])
            tail_ref[idx] = u[u.shape[0] - 8:, :]
        h_sc[:, j * nc:(j + 1) * nc] = (_gelu_tanh(halves[1]) * halves[0]).astype(BF16)
    y_ref[...] = x + _rms(_dot(h_sc[...], wdown_ref[...]), gpost_ref[...])


def _ffn_prompt_kernel(x_ref, gpre_ref, wup_ref, wconv_ref, bconv_ref, wdown_ref, gpost_ref,
                       y_ref, tail_ref, h_sc, hist_sc, *, tiles_per_seq):
    first = pl.program_id(0) % tiles_per_seq == 0

    @pl.when(first)
    def _():
        hist_sc[...] = jnp.zeros(hist_sc.shape, F32)

    def shifted(idx, u):
        rows = lax.broadcasted_iota(jnp.int32, u.shape, 0)
        p1 = hist_sc[idx, 7:8, :]
        p2 = hist_sc[idx, 6:7, :]
        u1 = jnp.where(rows == 0, p1, pltpu.roll(u, 1, axis=0))
        u2 = jnp.where(rows == 0, p2, jnp.where(rows == 1, p1, pltpu.roll(u, 2, axis=0)))
        hist_sc[idx] = u[u.shape[0] - 8:, :]
        return u1, u2

    _ffn_body(x_ref, gpre_ref, wup_ref, wconv_ref, bconv_ref, wdown_ref, gpost_ref, y_ref,
              tail_ref.at[0], h_sc, shifted)


def _ffn_sample_kernel(x_ref, st_ref, gpre_ref, wup_ref, wconv_ref, bconv_ref, wdown_ref,
                       gpost_ref, y_ref, u_ref, h_sc, *, t):
    x = x_ref[...]
    hn = _rms(x, gpre_ref[...]).astype(BF16)
    n_chunks = wup_ref.shape[0] // 2
    nc = wup_ref.shape[2]
    nseq = x.shape[0] // t
    pos = lax.broadcasted_iota(jnp.int32, (nseq, t, nc), 1)
    for j in range(n_chunks):
        halves = []
        for half in range(2):
            idx = half * n_chunks + j
            u = _dot(hn, wup_ref[idx])
            u_ref[idx] = u
            u = u.reshape(nseq, t, nc)
            st = st_ref[idx]
            old2 = jnp.broadcast_to(st[:, 0:1, :], u.shape)
            old1 = jnp.broadcast_to(st[:, 1:2, :], u.shape)
            u1 = jnp.where(pos < 1, old1, pltpu.roll(u, 1, axis=1))
            u2 = jnp.where(pos < 1, old2, jnp.where(pos < 2, old1, pltpu.roll(u, 2, axis=1)))
            w = wconv_ref[idx]
            halves.append(bconv_ref[idx] + u2 * w[0:1] + u1 * w[1:2] + u * w[2:3])
        h = (_gelu_tanh(halves[1]) * halves[0]).reshape(x.shape[0], nc)
        h_sc[:, j * nc:(j + 1) * nc] = h.astype(BF16)
    y_ref[...] = x + _rms(_dot(h_sc[...], wdown_ref[...]), gpost_ref[...])


def _ffn_prompt(x, g_pre, wup, wconv, bconv, wdown, g_post, seq, tm):
    t, d = x.shape
    nblk, _, nc = wup.shape
    tiles = t // tm
    row = pl.BlockSpec((tm, d), lambda i: (i, 0))
    consts = (g_pre, wup, wconv, bconv, wdown, g_post)
    return pl.pallas_call(
        functools.partial(_ffn_prompt_kernel, tiles_per_seq=seq // tm),
        grid=(tiles,),
        in_specs=[row] + [_resident(c.shape) for c in consts],
        out_specs=[row, pl.BlockSpec((1, nblk, 8, nc), lambda i: (i, 0, 0, 0))],
        out_shape=[jax.ShapeDtypeStruct(x.shape, F32), jax.ShapeDtypeStruct((tiles, nblk, 8, nc), F32)],
        scratch_shapes=[pltpu.VMEM((tm, wdown.shape[0]), BF16), pltpu.VMEM((nblk, 8, nc), F32)],
        compiler_params=_cparams("arbitrary"),
        name="ffn_prompt",
    )(x, *consts)


def _ffn_sample(x, hist, g_pre, wup, wconv, bconv, wdown, g_post, t):
    rows, d = x.shape
    nblk, _, nc = wup.shape
    consts = (g_pre, wup, wconv, bconv, wdown, g_post)
    return pl.pallas_call(
        functools.partial(_ffn_sample_kernel, t=t),
        grid=(1,),
        in_specs=[_resident(a.shape) for a in (x, hist) + consts],
        out_specs=[pl.BlockSpec(x.shape, lambda i: (0, 0)),
                   pl.BlockSpec((nblk, rows, nc), lambda i: (0, 0, 0))],
        out_shape=[jax.ShapeDtypeStruct(x.shape, F32), jax.ShapeDtypeStruct((nblk, rows, nc), F32)],
        scratch_shapes=[pltpu.VMEM((rows, wdown.shape[0]), BF16)],
        compiler_params=_cparams("arbitrary"),
        name="ffn_sample",
    )(x, hist, *consts)


def _pack_layer(w_in, w_alpha2, w_up, w_conv, b_conv, w_down, nc):
    aw, kw, vw = A_HEADS * A_DIM, B_HEADS * B_DK, B_HEADS * B_DV
    d = w_in.shape[0]
    offs = np.cumsum([aw, aw, aw, A_HEADS, kw, kw, vw, GATE_RANK, vw, d, d])[:-1].tolist()
    q_a, k_a, v_a, f_a, q_b, k_b, v_b, a_low, r_b, g_a, g_b = jnp.split(w_in, offs, axis=1)
    wbig = jnp.concatenate([q_a, k_a, v_a, q_b, k_b, v_b], axis=1).astype(BF16)
    pad = lambda a: jnp.pad(a, ((0, 0), (0, LANES - a.shape[1])))
    wsm = jnp.concatenate([pad(f_a), pad(a_low)], axis=1).astype(BF16)
    wa2 = jnp.pad(w_alpha2, ((0, LANES - GATE_RANK), (0, 0))).astype(BF16)
    dff = w_down.shape[0]
    n_chunks = dff // nc
    wup = w_up.reshape(d, 2 * n_chunks, nc).transpose(1, 0, 2).astype(BF16)
    wconv = w_conv.reshape(w_conv.shape[0], 2 * n_chunks, nc).transpose(1, 0, 2)
    bconv = b_conv.reshape(2 * n_chunks, 1, nc)
    wdown = w_down.astype(BF16)
    return dict(wbig=wbig, wsm=wsm, wa2=wa2, w_r=r_b.astype(BF16), w_ga=g_a.astype(BF16),
                w_gb=g_b.astype(BF16), wup=wup, wconv=wconv, bconv=bconv, wdown=wdown)


def kernel(x_prompt, x_sample, cache_k, cache_v, cache_logf, state_gla, state_conv, page_table,
           g_pre_mix, w_in, b_f, w_alpha2, b_alpha, g_gla, w_proj_a, w_proj_b, w_out, g_post_mix,
           g_pre_ffn, w_up, w_conv, b_conv, w_down, g_post_ffn):
    batch, seq, d = x_prompt.shape
    db, dt, _ = x_sample.shape
    depth = w_in.shape[0]
    n_pool = cache_k.shape[1]
    aw = A_HEADS * A_DIM
    dff2 = w_up.shape[2]
    nc = 256
    assert dt == 8 and depth >= 1

    yp = x_prompt.reshape(batch * seq, d)
    ys = x_sample.reshape(db * dt, d)
    outs = [[] for _ in range(10)]
    for l in range(depth):
        w = _pack_layer(w_in[l], w_alpha2[l], w_up[l], w_conv[l], b_conv[l], w_down[l], nc)
        row = lambda a: a[l].reshape(1, -1)
        inproj_w = (row(g_pre_mix), w["wbig"], w["wsm"], row(b_f), w["wa2"], row(b_alpha))
        mix_w = (row(g_pre_mix), w["w_r"], w["w_ga"], w["w_gb"], row(g_gla), w_proj_a[l].astype(BF16),
                 w_proj_b[l].astype(BF16), w_out[l].astype(BF16), row(g_post_mix))
        ffn_w = (row(g_pre_ffn), w["wup"], w["wconv"], w["bconv"], w["wdown"], row(g_post_ffn))

        qa, ka, va, kaf, vaf, logf, qb, kb, vb, la = _inproj(yp, *inproj_w, seq=seq, tm=512)
        oa = _fox_prompt(qa, ka, va, _key_bias_lanes(logf, batch, seq), batch, seq, tq=1024, tk=512)
        seqs = lambda a: a.reshape(batch, seq, a.shape[1])
        ob, gla_state_t = _gla_prompt(seqs(qb), seqs(kb), seqs(vb), seqs(la), batch, seq, rows=512)
        ob = ob.reshape(batch * seq, ob.shape[2])
        gla_state = gla_state_t.transpose(0, 1, 3, 2)
        x1 = _mix(yp, oa, ob, *mix_w, tm=512)
        yp, tails = _ffn_prompt(x1, *ffn_w, seq=seq, tm=512)
        tiles_per_seq = seq // 512
        tails = tails.reshape(batch, tiles_per_seq, 2 * (dff2 // 2 // nc), 8, nc)[:, -1, :, 6:, :]
        conv_p = tails.transpose(0, 2, 1, 3).reshape(batch, 2, dff2)
        for lst, val in zip(outs[:5], (kaf.transpose(0, 3, 1, 2), vaf.transpose(0, 3, 1, 2),
                                       logf.reshape(batch, seq, A_HEADS), gla_state, conv_p)):
            lst.append(val)

        qa, ka, va, kaf, vaf, logf, qb, kb, vb, la = _inproj(ys, *inproj_w, seq=db * dt, tm=db * dt)
        kaf, vaf = (a.reshape(A_HEADS, A_DIM, db, dt).transpose(2, 0, 1, 3) for a in (kaf, vaf))
        hh = A_HEADS // 2
        eye = jnp.eye(hh, dtype=BF16)
        q_bd = jnp.einsum("btgld,lm->bgltmd", qa.reshape(db, dt, 2, hh, A_DIM), eye).reshape(
            db, 2, hh * dt, hh * A_DIM)
        padk = lambda a: jnp.pad(a[None], ((0, 0), (0, 0), (0, 0), (0, 0), (0, PAGE - dt)))
        lfn = jnp.pad(logf.reshape(1, db, dt, A_HEADS).transpose(0, 1, 3, 2),
                      ((0, 0), (0, 0), (0, 0), (0, PAGE - dt)))
        o_bd = _fox_sample(page_table, q_bd, padk(kaf), padk(vaf), lfn,
                           cache_k.transpose(0, 1, 3, 4, 2), cache_v.transpose(0, 1, 3, 4, 2),
                           cache_logf.transpose(0, 1, 3, 2), layer=l)
        oa = jnp.einsum("bgltmd,lm->btgld", o_bd.reshape(db, 2, hh, dt, hh, A_DIM),
                        jnp.eye(hh, dtype=F32)).reshape(db * dt, aw).astype(BF16)
        ob, gla_state = _gla_sample(qb, kb, vb, la, state_gla[l], db, dt)
        x1 = _mix(ys, oa, ob, *mix_w, tm=db * dt)
        hist = state_conv[l].astype(F32).reshape(db, state_conv.shape[2], dff2 // nc, nc).transpose(2, 0, 1, 3)
        ys, u = _ffn_sample(x1, hist, *ffn_w, t=dt)
        conv_s = u.transpose(1, 0, 2).reshape(db, dt, dff2)[:, dt - 2:, :]
        for lst, val in zip(outs[5:], (kaf.transpose(0, 3, 1, 2), vaf.transpose(0, 3, 1, 2),
                                       logf.reshape(db, dt, A_HEADS), gla_state, conv_s)):
            lst.append(val)

    stacked = [jnp.stack(o) for o in outs]
    return (yp.reshape(batch, seq, d), ys.reshape(db, dt, d), *stacked)
```

```python
import functools

import jax
import jax.numpy as jnp
import numpy as np
from jax import lax
from jax.experimental import pallas as pl
from jax.experimental.pallas import tpu as pltpu

F32 = jnp.float32
BF16 = jnp.bfloat16
EPS = 1e-6
GATE_TEMP = 16.0
PAGE = 128
A_HEADS, A_DIM = 8, 64
B_HEADS, B_DK, B_DV = 4, 128, 256
GATE_RANK = 16
LANES = 128
VMEM_LIMIT = 56 * 1024 * 1024
HIGHEST = lax.Precision.HIGHEST
LOG2E = float(np.log2(np.e))
NT = (((1,), (1,)), ((), ()))
TN = (((0,), (0,)), ((), ()))


def _cparams(*sem):
    return pltpu.CompilerParams(dimension_semantics=sem, vmem_limit_bytes=VMEM_LIMIT)


def _resident(shape):
    nd = len(shape)
    return pl.BlockSpec(shape, lambda *_: (0,) * nd, pipeline_mode=pl.Buffered(1))


def _rms(x, g):
    return x * lax.rsqrt(jnp.mean(x * x, axis=-1, keepdims=True) + EPS) * g


def _log_sigmoid(x):
    return jnp.minimum(x, 0.0) - jnp.log1p(jnp.exp(-jnp.abs(x)))


def _dot(a, b):
    return jnp.dot(a, b, preferred_element_type=F32)


def _trunc_to_bf16_bits(x):
    bits = lax.bitcast_convert_type(x, jnp.int32) & jnp.int32(-65536)
    return lax.bitcast_convert_type(bits, F32)


def _split3(x):
    hi = _trunc_to_bf16_bits(x)
    r1 = x - hi
    mid = _trunc_to_bf16_bits(r1)
    lo = r1 - mid
    return hi.astype(BF16), mid.astype(BF16), lo.astype(BF16)


def _dot_sel(x, sel):
    return sum(_dot(part, sel) for part in _split3(x))


def _sel_dot(sel, x):
    return sum(_dot(sel, part) for part in _split3(x))


def _inproj_kernel(x_ref, g_ref, wbig_ref, wsm_ref, bf_ref, wa2_ref, ba_ref,
                   qa_ref, ka_ref, va_ref, kaf_ref, vaf_ref, logf_ref,
                   qb_ref, kb_ref, vb_ref, la_ref):
    hn = _rms(x_ref[...], g_ref[...]).astype(BF16)
    aw = A_HEADS * A_DIM
    kw = B_HEADS * B_DK

    def mm(lo, hi):
        return _dot(hn, wbig_ref[:, lo:hi])

    qa_ref[...] = (mm(0, aw) * (A_DIM ** -0.5 * LOG2E)).astype(BF16)
    k_a = mm(aw, 2 * aw)
    kaf_ref[0] = k_a.T.reshape(A_HEADS, A_DIM, k_a.shape[0])
    ka_ref[...] = k_a.astype(BF16)
    v_a = mm(2 * aw, 3 * aw)
    vaf_ref[0] = v_a.T.reshape(A_HEADS, A_DIM, v_a.shape[0])
    va_ref[...] = v_a.astype(BF16)
    o = 3 * aw
    qb_ref[...] = mm(o, o + kw) * B_DK ** -0.5
    kb_ref[...] = mm(o + kw, o + 2 * kw)
    vb_ref[...] = mm(o + 2 * kw, o + 2 * kw + B_HEADS * B_DV)
    us = _dot(hn, wsm_ref[...])
    logf_ref[...] = _log_sigmoid(us[:, :A_HEADS] + bf_ref[...])
    z = _dot(us[:, LANES:].astype(BF16), wa2_ref[...]) + ba_ref[...]
    la_ref[...] = _log_sigmoid(z) / GATE_TEMP


def _inproj(x, g_pre, wbig, wsm, b_f, wa2, b_alpha, seq, tm):
    t, d = x.shape
    aw, kw, vw = A_HEADS * A_DIM, B_HEADS * B_DK, B_HEADS * B_DV
    row = lambda n: pl.BlockSpec((tm, n), lambda i: (i, 0))
    outs = [(aw, BF16)] * 3 + [(aw, F32)] * 2 + [(A_HEADS, F32), (kw, F32), (kw, F32), (vw, F32), (kw, F32)]
    out_specs = [row(n) for n, _ in outs]
    out_shape = [jax.ShapeDtypeStruct((t, n), dt) for n, dt in outs]
    tiles_per_seq = seq // tm
    for i in (3, 4):
        out_specs[i] = pl.BlockSpec((1, A_HEADS, A_DIM, tm),
                                    lambda i: (i // tiles_per_seq, 0, 0, i % tiles_per_seq))
        out_shape[i] = jax.ShapeDtypeStruct((t // seq, A_HEADS, A_DIM, seq), F32)
    return pl.pallas_call(
        _inproj_kernel,
        grid=(t // tm,),
        in_specs=[row(d), _resident(g_pre.shape), _resident(wbig.shape), _resident(wsm.shape),
                  _resident(b_f.shape), _resident(wa2.shape), _resident(b_alpha.shape)],
        out_specs=out_specs,
        out_shape=out_shape,
        compiler_params=_cparams("arbitrary"),
        name="inproj",
    )(x, g_pre, wbig, wsm, b_f, wa2, b_alpha)


def _key_bias_kernel(lf_ref, tril_ref, place_ref, o_ref, carry_sc, *, seg):
    @pl.when(pl.program_id(1) == 0)
    def _():
        carry_sc[...] = jnp.zeros(carry_sc.shape, F32)

    carry = carry_sc[...]
    for s in range(lf_ref.shape[0] // seg):
        rows = slice(s * seg, (s + 1) * seg)
        c = _sel_dot(tril_ref[...], lf_ref[rows, :]) + carry
        carry = c[seg - 1:seg, :]
        x = -c * LOG2E
        hi = _trunc_to_bf16_bits(x)
        r1 = x - hi
        mid = _trunc_to_bf16_bits(r1)
        parts = (hi, mid, r1 - mid)
        o_ref[rows, :] = sum(_dot(p, place_ref[j]) for j, p in enumerate(parts)).astype(o_ref.dtype)
    carry_sc[...] = carry


def _key_bias_lanes(logf, batch, seq, rows=2048, seg=256):
    heads = logf.shape[1]
    tril = jnp.tril(jnp.ones((seg, seg), BF16))
    place = np.zeros((3, heads, heads // 2 * LANES), np.float32)
    for j in range(3):
        for h in range(heads):
            place[j, h, (h // 2) * LANES + (h % 2) * 3 + j] = 1.0
    place = jnp.asarray(place)
    steps = seq // rows
    return pl.pallas_call(
        functools.partial(_key_bias_kernel, seg=seg),
        grid=(batch, steps),
        in_specs=[pl.BlockSpec((rows, heads), lambda b, i: (b * steps + i, 0)),
                  _resident(tril.shape), _resident(place.shape)],
        out_specs=pl.BlockSpec((rows, place.shape[2]), lambda b, i: (b * steps + i, 0)),
        out_shape=jax.ShapeDtypeStruct((batch * seq, place.shape[2]), BF16),
        scratch_shapes=[pltpu.VMEM((1, heads), F32)],
        compiler_params=_cparams("arbitrary", "arbitrary"),
        name="key_bias",
    )(logf, tril, place)


def _fox_prompt_kernel(q_ref, k_ref, v_ref, kb_ref, o_ref, m_sc, l_sc, acc_sc, *, tq, tk):
    i = pl.program_id(2)
    q = q_ref[...]
    lane = lax.broadcasted_iota(jnp.int32, q.shape, 1)
    zero = jnp.zeros_like(q)
    sel_even = jnp.where(lane < 3, 1.0, 0.0).astype(BF16)
    sel_odd = jnp.where(lane < 3, 0.0, jnp.where(lane < 6, 1.0, 0.0)).astype(BF16)
    q_even = jnp.concatenate([jnp.where(lane < A_DIM, q, zero), sel_even], axis=1)
    q_odd = jnp.concatenate([jnp.where(lane >= A_DIM, q, zero), sel_odd], axis=1)
    qs = jnp.concatenate([q_even, q_odd], axis=0)
    m_sc[...] = jnp.full(m_sc.shape, -jnp.inf, F32)
    l_sc[...] = jnp.zeros(l_sc.shape, F32)
    acc_sc[...] = jnp.zeros(acc_sc.shape, F32)
    reps = tk // LANES

    def block(j, masked, r0=0, nrows=2 * tq):
        rs = slice(r0, r0 + nrows)
        start = pl.multiple_of(j * tk, tk)
        ka = jnp.concatenate([k_ref[pl.ds(start, tk), :], kb_ref[pl.ds(start, tk), :]], axis=1)
        s = lax.dot_general(qs[rs], ka, NT, preferred_element_type=F32)
        if masked:
            r = lax.broadcasted_iota(jnp.int32, s.shape, 0) + r0
            c = lax.broadcasted_iota(jnp.int32, s.shape, 1)
            qpos = i * tq + jnp.where(r >= tq, r - tq, r)
            s = jnp.where(start + c <= qpos, s, -jnp.inf)
        m_old = m_sc[rs]
        m_new = jnp.maximum(m_old, jnp.max(s, axis=-1, keepdims=True))
        alpha = jnp.exp2(m_old - m_new)
        p = jnp.exp2(s - jnp.tile(m_new, (1, reps)))
        l_sc[rs] = alpha * l_sc[rs] + jnp.sum(p, axis=-1, keepdims=True)
        acc_sc[rs] = alpha * acc_sc[rs] + _dot(p.astype(BF16), v_ref[pl.ds(start, tk), :])
        m_sc[rs] = m_new

    n_full = i * (tq // tk)

    per_body = 4

    def body(jj, carry):
        for u in range(per_body):
            block(per_body * jj + u, False)
        return carry

    lax.fori_loop(0, n_full // per_body, body, 0)
    left = n_full % per_body
    for bit in (2, 1):
        @pl.when((left & bit) != 0)
        def _(bit=bit):
            first = n_full - (left & (2 * bit - 1))
            for u in range(bit):
                block(first + u, False)

    block(n_full, True)
    for d in range(1, tq // tk):
        for e in range(2):
            block(n_full + d, True, r0=e * tq + d * tk, nrows=tq - d * tk)
    o = acc_sc[...] / l_sc[...]
    o_ref[...] = jnp.where(lane < A_DIM, o[:tq], o[tq:]).astype(o_ref.dtype)


def _fox_prompt(qa, ka, va, kbias, batch, seq, tq, tk):
    assert tq % tk == 0
    nq = seq // tq
    pairs = A_HEADS // 2
    kv_spec = pl.BlockSpec((seq, LANES), lambda b, p, i: (b, p))
    q_spec = pl.BlockSpec((tq, LANES), lambda b, p, i: (b * nq + i, p))
    return pl.pallas_call(
        functools.partial(_fox_prompt_kernel, tq=tq, tk=tk),
        grid=(batch, pairs, nq),
        in_specs=[q_spec, kv_spec, kv_spec, kv_spec],
        out_specs=q_spec,
        out_shape=jax.ShapeDtypeStruct(qa.shape, BF16),
        scratch_shapes=[pltpu.VMEM((2 * tq, LANES), F32), pltpu.VMEM((2 * tq, LANES), F32),
                        pltpu.VMEM((2 * tq, LANES), F32)],
        compiler_params=_cparams("arbitrary", "arbitrary", "arbitrary"),
        name="fox_prompt",
    )(qa, ka, va, kbias)


def _fox_sample_kernel(pt_ref, q_ref, kn_ref, vn_ref, lfn_ref, tri_ref, sfx_ref, *rest, pages):
    page_refs = rest[:3 * pages]
    o_ref, m_sc, l_sc, acc_sc, carry_sc = rest[3 * pages:]
    g = pl.program_id(1)
    halves = 2
    hh = A_HEADS // halves
    nq = q_ref.shape[2] // hh

    def tiles(refs, half):
        return jnp.concatenate(
            [r[0, 0, half * hh:(half + 1) * hh].reshape(hh * A_DIM, PAGE).astype(BF16) for r in refs], axis=1)

    def rows_of(bias, half):
        return jnp.concatenate(
            [jnp.broadcast_to(bias[h:h + 1, :], (nq, bias.shape[1])) for h in range(half * hh, (half + 1) * hh)],
            axis=0) * LOG2E

    def update(half, s_list, v_groups):
        m_old = m_sc[half]
        m_row = jnp.max(s_list[0], axis=-1, keepdims=True)
        for s in s_list[1:]:
            m_row = jnp.maximum(m_row, jnp.max(s, axis=-1, keepdims=True))
        m_new = jnp.maximum(m_old, m_row)
        alpha = jnp.exp2(m_old - m_new)
        l_row = None
        pv = None
        for s, v_refs in zip(s_list, v_groups):
            p = jnp.exp2(s - jnp.tile(m_new, (1, s.shape[1] // LANES)))
            ls = jnp.sum(p, axis=-1, keepdims=True)
            l_row = ls if l_row is None else l_row + ls
            d = lax.dot_general(p.astype(BF16), tiles(v_refs, half), NT, preferred_element_type=F32)
            pv = d if pv is None else pv + d
        l_sc[half] = alpha * l_sc[half] + l_row
        acc_sc[half] = jnp.tile(alpha, (1, hh * A_DIM // LANES)) * acc_sc[half] + pv
        m_sc[half] = m_new

    @pl.when(g == 0)
    def _():
        m_sc[...] = jnp.full(m_sc.shape, -jnp.inf, F32)
        l_sc[...] = jnp.zeros(l_sc.shape, F32)
        acc_sc[...] = jnp.zeros(acc_sc.shape, F32)
        carry_sc[...] = jnp.zeros(carry_sc.shape, F32)
        lfn = lfn_ref[0, 0]
        cn = _dot_sel(jnp.concatenate([lfn, jnp.zeros_like(lfn)], axis=0), tri_ref[...])[:A_HEADS]
        for half in range(halves):
            s = _dot(q_ref[0, half], tiles([kn_ref], half)) - rows_of(cn, half)
            r = lax.broadcasted_iota(jnp.int32, s.shape, 0)
            c = lax.broadcasted_iota(jnp.int32, s.shape, 1)
            s = jnp.where(c <= (r & (nq - 1)), s, -jnp.inf)
            update(half, [s], [[vn_ref]])

    order = list(reversed(range(pages)))
    lf_all = jnp.concatenate([page_refs[2 * pages + p][0, 0] for p in order], axis=0)
    sfx_all = _dot_sel(lf_all, sfx_ref[...])
    carry = carry_sc[...]
    biases = []
    for i in range(pages):
        blk = sfx_all[i * A_HEADS:(i + 1) * A_HEADS]
        biases.append(blk[:, :PAGE] + carry)
        carry = carry + blk[:, PAGE:]
    carry_sc[...] = carry
    pairs = [(order[i], order[i + 1], jnp.concatenate(biases[i:i + 2], axis=1)) for i in range(0, pages, 2)]
    for half in range(halves):
        s_list = [_dot(q_ref[0, half], tiles([page_refs[a], page_refs[b]], half)) + rows_of(bias, half)
                  for a, b, bias in pairs]
        update(half, s_list, [[page_refs[pages + a], page_refs[pages + b]] for a, b, _ in pairs])

    @pl.when(g == pl.num_programs(1) - 1)
    def _():
        for half in range(halves):
            o_ref[0, half] = acc_sc[half] / jnp.tile(l_sc[half], (1, hh * A_DIM // LANES))


def _fox_sample(page_table, q, kn, vn, lfn, cache_kt, cache_vt, cache_lft, layer, pages=16):
    db, n_pages = page_table.shape
    groups = n_pages // pages
    assert pages % 2 == 0
    state = q.shape[1:3] + (LANES,)
    idx = np.arange(PAGE)
    tri = jnp.asarray(idx[:, None] <= idx[None, :], BF16)
    sfx = jnp.asarray(np.concatenate([idx[:, None] > idx[None, :], np.ones((PAGE, PAGE), bool)], 1), BF16)
    kv_page = (A_HEADS, A_DIM, PAGE)
    lf_page = (A_HEADS, PAGE)

    def per_b(shape):
        nd = len(shape)
        return pl.BlockSpec((1,) + shape, lambda b, g, pt: (b,) + (0,) * nd)

    def new_page(shape):
        nd = len(shape)
        return pl.BlockSpec((1, 1) + shape, lambda b, g, pt: (0, b) + (0,) * nd)

    def const(shape):
        nd = len(shape)
        return pl.BlockSpec(shape, lambda b, g, pt: (0,) * nd)

    def page_spec(shape, p):
        nd = len(shape)
        return pl.BlockSpec(
            (1, 1) + shape,
            lambda b, g, pt: (layer, pt[b * n_pages + (groups - 1 - g) * pages + p]) + (0,) * nd)

    in_specs = [per_b(q.shape[1:]), new_page(kv_page), new_page(kv_page), new_page(lf_page),
                const(tri.shape), const(sfx.shape)]
    in_specs += [page_spec(kv_page, p) for p in range(pages)]
    in_specs += [page_spec(kv_page, p) for p in range(pages)]
    in_specs += [page_spec(lf_page, p) for p in range(pages)]
    grid_spec = pltpu.PrefetchScalarGridSpec(
        num_scalar_prefetch=1,
        grid=(db, groups),
        in_specs=in_specs,
        out_specs=per_b(q.shape[1:]),
        scratch_shapes=[pltpu.VMEM(state, F32), pltpu.VMEM(state, F32),
                        pltpu.VMEM(q.shape[1:], F32), pltpu.VMEM((A_HEADS, LANES), F32)],
    )
    return pl.pallas_call(
        functools.partial(_fox_sample_kernel, pages=pages),
        grid_spec=grid_spec,
        out_shape=jax.ShapeDtypeStruct(q.shape, F32),
        compiler_params=_cparams("arbitrary", "arbitrary"),
        name="fox_sample",
    )(page_table.reshape(-1), q, kn, vn, lfn, tri, sfx,
      *([cache_kt] * pages), *([cache_vt] * pages), *([cache_lft] * pages))


def _gla_chunk(q, k, v, la, s_old, tril, row0, ones, cast):
    c = q.shape[0]
    bc = jnp.dot(tril, la, precision=HIGHEST, preferred_element_type=F32)
    ref = bc[c // 2 - 1:c // 2, :]
    last = bc[c - 1:c, :]
    qt = q * jnp.exp(bc - ref)
    kt = k * jnp.exp(ref - bc)
    o = _dot(cast(qt * jnp.exp(ref)), cast(s_old))
    a = lax.dot_general(cast(qt), cast(kt), NT, preferred_element_type=F32)
    a = jnp.where(tril > 0, a, 0.0)
    vc = cast(v)
    o = o + _dot(cast(a), vc)
    kd = cast(kt * jnp.exp(last - ref))
    scale = lax.dot_general(row0 * jnp.exp(last), ones, TN, precision=HIGHEST, preferred_element_type=F32)
    s_new = scale * s_old + lax.dot_general(kd, vc, TN, preferred_element_type=F32)
    return o, s_new


def _gla_prompt_kernel(q_ref, k_ref, v_ref, la_ref, tril_ref, o_ref, st_ref, *, chunk):
    @pl.when(pl.program_id(0) == 0)
    def _():
        st_ref[...] = jnp.zeros(st_ref.shape, F32)

    tril = tril_ref[...]
    r = lax.broadcasted_iota(jnp.int32, tril.shape, 0)
    c = lax.broadcasted_iota(jnp.int32, tril.shape, 1)
    causal = c <= r

    nb = q_ref.shape[0]
    chains = [(b, h) for b in range(nb) for h in range(B_HEADS)]
    kcols = lambda h: slice(h * B_DK, (h + 1) * B_DK)
    vcols = lambda h: slice(h * B_DV, (h + 1) * B_DV)

    def body(ci, states):
        rows = pl.ds(pl.multiple_of(ci * chunk, chunk), chunk)
        qs, kd, decay, qtb, ktb = [], [], [], [], []
        for b in range(nb):
            bc = _sel_dot(tril, la_ref[b, rows, :])
            ref = bc[chunk // 2 - 1:chunk // 2, :]
            last = bc[chunk - 1:chunk, :]
            qt = q_ref[b, rows, :] * jnp.exp(bc - ref)
            kt = k_ref[b, rows, :] * jnp.exp(ref - bc)
            qs.append((qt * jnp.exp(ref)).astype(BF16))
            kd.append((kt * jnp.exp(last - ref)).astype(BF16))
            decay.append(jnp.exp(last))
            qtb.append(qt.astype(BF16))
            ktb.append(kt.astype(BF16))
        vbs = [v_ref[b, rows, vcols(h)].astype(BF16) for b, h in chains]
        intra = []
        for (b, h), vb in zip(chains, vbs):
            a = lax.dot_general(qtb[b][:, kcols(h)], ktb[b][:, kcols(h)], NT, preferred_element_type=F32)
            intra.append(_dot(jnp.where(causal, a, 0.0).astype(BF16), vb))
        new_states = []
        for (b, h), vb, st, o_intra in zip(chains, vbs, states, intra):
            o = lax.dot_general(qs[b][:, kcols(h)], st.astype(BF16), NT, preferred_element_type=F32)
            o_ref[b, rows, vcols(h)] = o + o_intra
            new_states.append(st * decay[b][:, kcols(h)]
                              + lax.dot_general(vb, kd[b][:, kcols(h)], TN, preferred_element_type=F32))
        return tuple(new_states)

    states = lax.fori_loop(0, q_ref.shape[1] // chunk, body, tuple(st_ref[b, h] for b, h in chains))
    for (b, h), st in zip(chains, states):
        st_ref[b, h] = st


def _gla_consts(chunk):
    tril = jnp.tril(jnp.ones((chunk, chunk), F32))
    row0 = jnp.zeros((8, B_DK), F32).at[0].set(1.0)
    ones = jnp.ones((8, B_DV), F32)
    return tril, row0, ones


def _gla_prompt(qb, kb, vb, la, batch, seq, rows, chunk=64):
    kw, vw = B_HEADS * B_DK, B_HEADS * B_DV
    tril = jnp.tril(jnp.ones((chunk, chunk), BF16))
    rk = pl.BlockSpec((batch, rows, kw), lambda i: (0, i, 0))
    rv = pl.BlockSpec((batch, rows, vw), lambda i: (0, i, 0))
    return pl.pallas_call(
        functools.partial(_gla_prompt_kernel, chunk=chunk),
        grid=(seq // rows,),
        in_specs=[rk, rk, rv, rk, _resident(tril.shape)],
        out_specs=[rv, pl.BlockSpec((batch, B_HEADS, B_DV, B_DK), lambda i: (0, 0, 0, 0))],
        out_shape=[jax.ShapeDtypeStruct(vb.shape, F32),
                   jax.ShapeDtypeStruct((batch, B_HEADS, B_DV, B_DK), F32)],
        compiler_params=_cparams("arbitrary"),
        name="gla_prompt",
    )(qb, kb, vb, la, tril)


def _gla_sample_kernel(q_ref, k_ref, v_ref, la_ref, s_in_ref, tril_ref, row0_ref, ones_ref, o_ref, s_ref):
    cast = lambda a: a
    tril, row0, ones = tril_ref[...], row0_ref[...], ones_ref[...]
    for h in range(B_HEADS):
        kc = slice(h * B_DK, (h + 1) * B_DK)
        vc = slice(h * B_DV, (h + 1) * B_DV)
        o, s_new = _gla_chunk(q_ref[:, kc], k_ref[:, kc], v_ref[:, vc], la_ref[:, kc],
                              s_in_ref[0, h], tril, row0, ones, cast)
        o_ref[:, vc] = o
        s_ref[0, h] = s_new


def _gla_sample(qb, kb, vb, la, state, db, t):
    kw, vw = B_HEADS * B_DK, B_HEADS * B_DV
    consts = _gla_consts(t)
    rk = pl.BlockSpec((t, kw), lambda b: (b, 0))
    rv = pl.BlockSpec((t, vw), lambda b: (b, 0))
    st = pl.BlockSpec((1, B_HEADS, B_DK, B_DV), lambda b: (b, 0, 0, 0))
    return pl.pallas_call(
        _gla_sample_kernel,
        grid=(db,),
        in_specs=[rk, rk, rv, rk, st] + [_resident(c.shape) for c in consts],
        out_specs=[rv, st],
        out_shape=[jax.ShapeDtypeStruct(vb.shape, F32), jax.ShapeDtypeStruct(state.shape, F32)],
        compiler_params=_cparams("arbitrary"),
        name="gla_sample",
    )(qb, kb, vb, la, state, *consts)


def _mix_kernel(x_ref, oa_ref, ob_ref, gpre_ref, wr_ref, wga_ref, wgb_ref, ggla_ref,
                wpa_ref, wpb_ref, wout_ref, gpost_ref, y_ref):
    x = x_ref[...]
    hn = _rms(x, gpre_ref[...]).astype(BF16)
    r_b = _dot(hn, wr_ref[...])
    ob = ob_ref[...]
    parts = []
    for h in range(B_HEADS):
        seg = ob[:, h * B_DV:(h + 1) * B_DV]
        parts.append(_rms(seg, ggla_ref[...]))
    obn = jnp.concatenate(parts, axis=-1) * (r_b * jax.nn.sigmoid(r_b))
    pa = _dot(oa_ref[...], wpa_ref[...])
    pb = _dot(obn.astype(BF16), wpb_ref[...])
    merged = jax.nn.sigmoid(_dot(hn, wga_ref[...])) * pa + jax.nn.sigmoid(_dot(hn, wgb_ref[...])) * pb
    z = _dot(merged.astype(BF16), wout_ref[...])
    y_ref[...] = x + _rms(z, gpost_ref[...])


def _mix(x, oa, ob, g_pre, w_r, w_ga, w_gb, g_gla, w_pa, w_pb, w_out, g_post, tm):
    t, d = x.shape
    row = lambda n: pl.BlockSpec((tm, n), lambda i: (i, 0))
    consts = (g_pre, w_r, w_ga, w_gb, g_gla, w_pa, w_pb, w_out, g_post)
    return pl.pallas_call(
        _mix_kernel,
        grid=(t // tm,),
        in_specs=[row(d), row(oa.shape[1]), row(ob.shape[1])] + [_resident(c.shape) for c in consts],
        out_specs=row(d),
        out_shape=jax.ShapeDtypeStruct(x.shape, F32),
        compiler_params=_cparams("arbitrary"),
        name="mix",
    )(x, oa, ob, *consts)


def _gelu_tanh(g):
    return 0.5 * g * (1.0 + jnp.tanh(np.sqrt(2.0 / np.pi) * (g + 0.044715 * (g * g * g))))


def _ffn_body(x_ref, gpre_ref, wup_ref, wconv_ref, bconv_ref, wdown_ref, gpost_ref, y_ref, tail_ref,
              h_sc, shifted):
    x = x_ref[...]
    hn = _rms(x, gpre_ref[...]).astype(BF16)
    n_chunks = wup_ref.shape[0] // 2
    nc = wup_ref.shape[2]
    for j in range(n_chunks):
        halves = []
        for half in range(2):
            idx = half * n_chunks + j
            u = _dot(hn, wup_ref[idx])
            u1, u2 = shifted(idx, u)
            w = wconv_ref[idx]
            halves.append(bconv_ref[idx] + u2 * w[0:1] + u1 * w[1:2] + u * w[2:3])
            tail_ref[idx] = u[u.shape[0] - 8:, :]
        h_sc[:, j * nc:(j + 1) * nc] = (_gelu_tanh(halves[1]) * halves[0]).astype(BF16)
    y_ref[...] = x + _rms(_dot(h_sc[...], wdown_ref[...]), gpost_ref[...])


def _ffn_prompt_kernel(x_ref, gpre_ref, wup_ref, wconv_ref, bconv_ref, wdown_ref, gpost_ref,
                       y_ref, tail_ref, h_sc, hist_sc, *, tiles_per_seq):
    first = pl.program_id(0) % tiles_per_seq == 0

    @pl.when(first)
    def _():
        hist_sc[...] = jnp.zeros(hist_sc.shape, F32)

    def shifted(idx, u):
        rows = lax.broadcasted_iota(jnp.int32, u.shape, 0)
        p1 = hist_sc[idx, 7:8, :]
        p2 = hist_sc[idx, 6:7, :]
        u1 = jnp.where(rows == 0, p1, pltpu.roll(u, 1, axis=0))
        u2 = jnp.where(rows == 0, p2, jnp.where(rows == 1, p1, pltpu.roll(u, 2, axis=0)))
        hist_sc[idx] = u[u.shape[0] - 8:, :]
        return u1, u2

    _ffn_body(x_ref, gpre_ref, wup_ref, wconv_ref, bconv_ref, wdown_ref, gpost_ref, y_ref,
              tail_ref.at[0], h_sc, shifted)


def _ffn_sample_kernel(x_ref, st_ref, gpre_ref, wup_ref, wconv_ref, bconv_ref, wdown_ref,
                       gpost_ref, y_ref, u_ref, h_sc, *, t):
    x = x_ref[...]
    hn = _rms(x, gpre_ref[...]).astype(BF16)
    n_chunks = wup_ref.shape[0] // 2
    nc = wup_ref.shape[2]
    nseq = x.shape[0] // t
    pos = lax.broadcasted_iota(jnp.int32, (nseq, t, nc), 1)
    for j in range(n_chunks):
        halves = []
        for half in range(2):
            idx = half * n_chunks + j
            u = _dot(hn, wup_ref[idx])
            u_ref[idx] = u
            u = u.reshape(nseq, t, nc)
            st = st_ref[idx]
            old2 = jnp.broadcast_to(st[:, 0:1, :], u.shape)
            old1 = jnp.broadcast_to(st[:, 1:2, :], u.shape)
            u1 = jnp.where(pos < 1, old1, pltpu.roll(u, 1, axis=1))
            u2 = jnp.where(pos < 1, old2, jnp.where(pos < 2, old1, pltpu.roll(u, 2, axis=1)))
            w = wconv_ref[idx]
            halves.append(bconv_ref[idx] + u2 * w[0:1] + u1 * w[1:2] + u * w[2:3])
        h = (_gelu_tanh(halves[1]) * halves[0]).reshape(x.shape[0], nc)
        h_sc[:, j * nc:(j + 1) * nc] = h.astype(BF16)
    y_ref[...] = x + _rms(_dot(h_sc[...], wdown_ref[...]), gpost_ref[...])


def _ffn_prompt(x, g_pre, wup, wconv, bconv, wdown, g_post, seq, tm):
    t, d = x.shape
    nblk, _, nc = wup.shape
    tiles = t // tm
    row = pl.BlockSpec((tm, d), lambda i: (i, 0))
    consts = (g_pre, wup, wconv, bconv, wdown, g_post)
    return pl.pallas_call(
        functools.partial(_ffn_prompt_kernel, tiles_per_seq=seq // tm),
        grid=(tiles,),
        in_specs=[row] + [_resident(c.shape) for c in consts],
        out_specs=[row, pl.BlockSpec((1, nblk, 8, nc), lambda i: (i, 0, 0, 0))],
        out_shape=[jax.ShapeDtypeStruct(x.shape, F32), jax.ShapeDtypeStruct((tiles, nblk, 8, nc), F32)],
        scratch_shapes=[pltpu.VMEM((tm, wdown.shape[0]), BF16), pltpu.VMEM((nblk, 8, nc), F32)],
        compiler_params=_cparams("arbitrary"),
        name="ffn_prompt",
    )(x, *consts)


def _ffn_sample(x, hist, g_pre, wup, wconv, bconv, wdown, g_post, t):
    rows, d = x.shape
    nblk, _, nc = wup.shape
    consts = (g_pre, wup, wconv, bconv, wdown, g_post)
    return pl.pallas_call(
        functools.partial(_ffn_sample_kernel, t=t),
        grid=(1,),
        in_specs=[_resident(a.shape) for a in (x, hist) + consts],
        out_specs=[pl.BlockSpec(x.shape, lambda i: (0, 0)),
                   pl.BlockSpec((nblk, rows, nc), lambda i: (0, 0, 0))],
        out_shape=[jax.ShapeDtypeStruct(x.shape, F32), jax.ShapeDtypeStruct((nblk, rows, nc), F32)],
        scratch_shapes=[pltpu.VMEM((rows, wdown.shape[0]), BF16)],
        compiler_params=_cparams("arbitrary"),
        name="ffn_sample",
    )(x, hist, *consts)


def _pack_layer(w_in, w_alpha2, w_up, w_conv, b_conv, w_down, nc):
    aw, kw, vw = A_HEADS * A_DIM, B_HEADS * B_DK, B_HEADS * B_DV
    d = w_in.shape[0]
    offs = np.cumsum([aw, aw, aw, A_HEADS, kw, kw, vw, GATE_RANK, vw, d, d])[:-1].tolist()
    q_a, k_a, v_a, f_a, q_b, k_b, v_b, a_low, r_b, g_a, g_b = jnp.split(w_in, offs, axis=1)
    wbig = jnp.concatenate([q_a, k_a, v_a, q_b, k_b, v_b], axis=1).astype(BF16)
    pad = lambda a: jnp.pad(a, ((0, 0), (0, LANES - a.shape[1])))
    wsm = jnp.concatenate([pad(f_a), pad(a_low)], axis=1).astype(BF16)
    wa2 = jnp.pad(w_alpha2, ((0, LANES - GATE_RANK), (0, 0))).astype(BF16)
    dff = w_down.shape[0]
    n_chunks = dff // nc
    wup = w_up.reshape(d, 2 * n_chunks, nc).transpose(1, 0, 2).astype(BF16)
    wconv = w_conv.reshape(w_conv.shape[0], 2 * n_chunks, nc).transpose(1, 0, 2)
    bconv = b_conv.reshape(2 * n_chunks, 1, nc)
    wdown = w_down.astype(BF16)
    return dict(wbig=wbig, wsm=wsm, wa2=wa2, w_r=r_b.astype(BF16), w_ga=g_a.astype(BF16),
                w_gb=g_b.astype(BF16), wup=wup, wconv=wconv, bconv=bconv, wdown=wdown)


def kernel(x_prompt, x_sample, cache_k, cache_v, cache_logf, state_gla, state_conv, page_table,
           g_pre_mix, w_in, b_f, w_alpha2, b_alpha, g_gla, w_proj_a, w_proj_b, w_out, g_post_mix,
           g_pre_ffn, w_up, w_conv, b_conv, w_down, g_post_ffn):
    batch, seq, d = x_prompt.shape
    db, dt, _ = x_sample.shape
    depth = w_in.shape[0]
    n_pool = cache_k.shape[1]
    aw = A_HEADS * A_DIM
    dff2 = w_up.shape[2]
    nc = 256
    assert dt == 8 and depth >= 1

    yp = x_prompt.reshape(batch * seq, d)
    ys = x_sample.reshape(db * dt, d)
    outs = [[] for _ in range(10)]
    for l in range(depth):
        w = _pack_layer(w_in[l], w_alpha2[l], w_up[l], w_conv[l], b_conv[l], w_down[l], nc)
        row = lambda a: a[l].reshape(1, -1)
        inproj_w = (row(g_pre_mix), w["wbig"], w["wsm"], row(b_f), w["wa2"], row(b_alpha))
        mix_w = (row(g_pre_mix), w["w_r"], w["w_ga"], w["w_gb"], row(g_gla), w_proj_a[l].astype(BF16),
                 w_proj_b[l].astype(BF16), w_out[l].astype(BF16), row(g_post_mix))
        ffn_w = (row(g_pre_ffn), w["wup"], w["wconv"], w["bconv"], w["wdown"], row(g_post_ffn))

        qa, ka, va, kaf, vaf, logf, qb, kb, vb, la = _inproj(yp, *inproj_w, seq=seq, tm=512)
        oa = _fox_prompt(qa, ka, va, _key_bias_lanes(logf, batch, seq), batch, seq, tq=1024, tk=512)
        seqs = lambda a: a.reshape(batch, seq, a.shape[1])
        ob, gla_state_t = _gla_prompt(seqs(qb), seqs(kb), seqs(vb), seqs(la), batch, seq, rows=512)
        ob = ob.reshape(batch * seq, ob.shape[2])
        gla_state = gla_state_t.transpose(0, 1, 3, 2)
        x1 = _mix(yp, oa, ob, *mix_w, tm=512)
        ffn_tm = 1024
        yp, tails = _ffn_prompt(x1, *ffn_w, seq=seq, tm=ffn_tm)
        tiles_per_seq = seq // ffn_tm
        tails = tails.reshape(batch, tiles_per_seq, 2 * (dff2 // 2 // nc), 8, nc)[:, -1, :, 6:, :]
        conv_p = tails.transpose(0, 2, 1, 3).reshape(batch, 2, dff2)
        for lst, val in zip(outs[:5], (kaf.transpose(0, 3, 1, 2), vaf.transpose(0, 3, 1, 2),
                                       logf.reshape(batch, seq, A_HEADS), gla_state, conv_p)):
            lst.append(val)

        qa, ka, va, kaf, vaf, logf, qb, kb, vb, la = _inproj(ys, *inproj_w, seq=db * dt, tm=db * dt)
        kaf, vaf = (a.reshape(A_HEADS, A_DIM, db, dt).transpose(2, 0, 1, 3) for a in (kaf, vaf))
        hh = A_HEADS // 2
        eye = jnp.eye(hh, dtype=BF16)
        q_bd = jnp.einsum("btgld,lm->bgltmd", qa.reshape(db, dt, 2, hh, A_DIM), eye).reshape(
            db, 2, hh * dt, hh * A_DIM)
        padk = lambda a: jnp.pad(a[None], ((0, 0), (0, 0), (0, 0), (0, 0), (0, PAGE - dt)))
        lfn = jnp.pad(logf.reshape(1, db, dt, A_HEADS).transpose(0, 1, 3, 2),
                      ((0, 0), (0, 0), (0, 0), (0, PAGE - dt)))
        o_bd = _fox_sample(page_table, q_bd, padk(kaf), padk(vaf), lfn,
                           cache_k.transpose(0, 1, 3, 4, 2), cache_v.transpose(0, 1, 3, 4, 2),
                           cache_logf.transpose(0, 1, 3, 2), layer=l)
        oa = jnp.einsum("bgltmd,lm->btgld", o_bd.reshape(db, 2, hh, dt, hh, A_DIM),
                        jnp.eye(hh, dtype=F32)).reshape(db * dt, aw).astype(BF16)
        ob, gla_state = _gla_sample(qb, kb, vb, la, state_gla[l], db, dt)
        x1 = _mix(ys, oa, ob, *mix_w, tm=db * dt)
        hist = state_conv[l].astype(F32).reshape(db, state_conv.shape[2], dff2 // nc, nc).transpose(2, 0, 1, 3)
        ys, u = _ffn_sample(x1, hist, *ffn_w, t=dt)
        conv_s = u.transpose(1, 0, 2).reshape(db, dt, dff2)[:, dt - 2:, :]
        for lst, val in zip(outs[5:], (kaf.transpose(0, 3, 1, 2), vaf.transpose(0, 3, 1, 2),
                                       logf.reshape(db, dt, A_HEADS), gla_state, conv_s)):
            lst.append(val)

    stacked = [jnp.stack(o) for o in outs]
    return (yp.reshape(batch, seq, d), ys.reshape(db, dt, d), *stacked)
```

```python
import functools

import jax
import jax.numpy as jnp
import numpy as np
from jax import lax
from jax.experimental import pallas as pl
from jax.experimental.pallas import tpu as pltpu

F32 = jnp.float32
BF16 = jnp.bfloat16
EPS = 1e-6
GATE_TEMP = 16.0
PAGE = 128
A_HEADS, A_DIM = 8, 64
B_HEADS, B_DK, B_DV = 4, 128, 256
GATE_RANK = 16
LANES = 128
VMEM_LIMIT = 56 * 1024 * 1024
HIGHEST = lax.Precision.HIGHEST
LOG2E = float(np.log2(np.e))
NT = (((1,), (1,)), ((), ()))
TN = (((0,), (0,)), ((), ()))


def _cparams(*sem):
    return pltpu.CompilerParams(dimension_semantics=sem, vmem_limit_bytes=VMEM_LIMIT)


def _resident(shape):
    nd = len(shape)
    return pl.BlockSpec(shape, lambda *_: (0,) * nd, pipeline_mode=pl.Buffered(1))


def _rms(x, g):
    return x * lax.rsqrt(jnp.mean(x * x, axis=-1, keepdims=True) + EPS) * g


def _log_sigmoid(x):
    return jnp.minimum(x, 0.0) - jnp.log1p(jnp.exp(-jnp.abs(x)))


def _dot(a, b):
    return jnp.dot(a, b, preferred_element_type=F32)


def _trunc_to_bf16_bits(x):
    bits = lax.bitcast_convert_type(x, jnp.int32) & jnp.int32(-65536)
    return lax.bitcast_convert_type(bits, F32)


def _split3(x):
    hi = _trunc_to_bf16_bits(x)
    r1 = x - hi
    mid = _trunc_to_bf16_bits(r1)
    lo = r1 - mid
    return hi.astype(BF16), mid.astype(BF16), lo.astype(BF16)


def _dot_sel(x, sel):
    return sum(_dot(part, sel) for part in _split3(x))


def _sel_dot(sel, x):
    return sum(_dot(sel, part) for part in _split3(x))


def _inproj_kernel(x_ref, g_ref, wbig_ref, wsm_ref, bf_ref, wa2_ref, ba_ref,
                   qa_ref, ka_ref, va_ref, kaf_ref, vaf_ref, logf_ref,
                   qb_ref, kb_ref, vb_ref, la_ref):
    hn = _rms(x_ref[...], g_ref[...]).astype(BF16)
    aw = A_HEADS * A_DIM
    kw = B_HEADS * B_DK

    def mm(lo, hi):
        return _dot(hn, wbig_ref[:, lo:hi])

    qa_ref[...] = (mm(0, aw) * (A_DIM ** -0.5 * LOG2E)).astype(BF16)
    k_a = mm(aw, 2 * aw)
    kaf_ref[0] = k_a.T.reshape(A_HEADS, A_DIM, k_a.shape[0])
    ka_ref[...] = k_a.astype(BF16)
    v_a = mm(2 * aw, 3 * aw)
    vaf_ref[0] = v_a.T.reshape(A_HEADS, A_DIM, v_a.shape[0])
    va_ref[...] = v_a.astype(BF16)
    o = 3 * aw
    qb_ref[...] = mm(o, o + kw) * B_DK ** -0.5
    kb_ref[...] = mm(o + kw, o + 2 * kw)
    vb_ref[...] = mm(o + 2 * kw, o + 2 * kw + B_HEADS * B_DV)
    us = _dot(hn, wsm_ref[...])
    logf_ref[...] = _log_sigmoid(us[:, :A_HEADS] + bf_ref[...])
    z = _dot(us[:, LANES:].astype(BF16), wa2_ref[...]) + ba_ref[...]
    la_ref[...] = _log_sigmoid(z) / GATE_TEMP


def _inproj(x, g_pre, wbig, wsm, b_f, wa2, b_alpha, seq, tm):
    t, d = x.shape
    aw, kw, vw = A_HEADS * A_DIM, B_HEADS * B_DK, B_HEADS * B_DV
    row = lambda n: pl.BlockSpec((tm, n), lambda i: (i, 0))
    outs = [(aw, BF16)] * 3 + [(aw, F32)] * 2 + [(A_HEADS, F32), (kw, F32), (kw, F32), (vw, F32), (kw, F32)]
    out_specs = [row(n) for n, _ in outs]
    out_shape = [jax.ShapeDtypeStruct((t, n), dt) for n, dt in outs]
    tiles_per_seq = seq // tm
    for i in (3, 4):
        out_specs[i] = pl.BlockSpec((1, A_HEADS, A_DIM, tm),
                                    lambda i: (i // tiles_per_seq, 0, 0, i % tiles_per_seq))
        out_shape[i] = jax.ShapeDtypeStruct((t // seq, A_HEADS, A_DIM, seq), F32)
    return pl.pallas_call(
        _inproj_kernel,
        grid=(t // tm,),
        in_specs=[row(d), _resident(g_pre.shape), _resident(wbig.shape), _resident(wsm.shape),
                  _resident(b_f.shape), _resident(wa2.shape), _resident(b_alpha.shape)],
        out_specs=out_specs,
        out_shape=out_shape,
        compiler_params=_cparams("arbitrary"),
        name="inproj",
    )(x, g_pre, wbig, wsm, b_f, wa2, b_alpha)


def _key_bias_kernel(lf_ref, tril_ref, place_ref, o_ref, carry_sc, *, seg):
    @pl.when(pl.program_id(1) == 0)
    def _():
        carry_sc[...] = jnp.zeros(carry_sc.shape, F32)

    carry = carry_sc[...]
    for s in range(lf_ref.shape[0] // seg):
        rows = slice(s * seg, (s + 1) * seg)
        c = _sel_dot(tril_ref[...], lf_ref[rows, :]) + carry
        carry = c[seg - 1:seg, :]
        x = -c * LOG2E
        hi = _trunc_to_bf16_bits(x)
        r1 = x - hi
        mid = _trunc_to_bf16_bits(r1)
        parts = (hi, mid, r1 - mid)
        o_ref[rows, :] = sum(_dot(p, place_ref[j]) for j, p in enumerate(parts)).astype(o_ref.dtype)
    carry_sc[...] = carry


def _key_bias_lanes(logf, batch, seq, rows=2048, seg=256):
    heads = logf.shape[1]
    tril = jnp.tril(jnp.ones((seg, seg), BF16))
    place = np.zeros((3, heads, heads // 2 * LANES), np.float32)
    for j in range(3):
        for h in range(heads):
            place[j, h, (h // 2) * LANES + (h % 2) * 3 + j] = 1.0
    place = jnp.asarray(place)
    steps = seq // rows
    return pl.pallas_call(
        functools.partial(_key_bias_kernel, seg=seg),
        grid=(batch, steps),
        in_specs=[pl.BlockSpec((rows, heads), lambda b, i: (b * steps + i, 0)),
                  _resident(tril.shape), _resident(place.shape)],
        out_specs=pl.BlockSpec((rows, place.shape[2]), lambda b, i: (b * steps + i, 0)),
        out_shape=jax.ShapeDtypeStruct((batch * seq, place.shape[2]), BF16),
        scratch_shapes=[pltpu.VMEM((1, heads), F32)],
        compiler_params=_cparams("arbitrary", "arbitrary"),
        name="key_bias",
    )(logf, tril, place)


def _fox_prompt_kernel(q_ref, k_ref, v_ref, kb_ref, o_ref, m_sc, l_sc, acc_sc, *, tq, tk):
    i = pl.program_id(2)
    q = q_ref[...]
    lane = lax.broadcasted_iota(jnp.int32, q.shape, 1)
    zero = jnp.zeros_like(q)
    sel_even = jnp.where(lane < 3, 1.0, 0.0).astype(BF16)
    sel_odd = jnp.where(lane < 3, 0.0, jnp.where(lane < 6, 1.0, 0.0)).astype(BF16)
    q_even = jnp.concatenate([jnp.where(lane < A_DIM, q, zero), sel_even], axis=1)
    q_odd = jnp.concatenate([jnp.where(lane >= A_DIM, q, zero), sel_odd], axis=1)
    qs = jnp.concatenate([q_even, q_odd], axis=0)
    m_sc[...] = jnp.full(m_sc.shape, -jnp.inf, F32)
    l_sc[...] = jnp.zeros(l_sc.shape, F32)
    acc_sc[...] = jnp.zeros(acc_sc.shape, F32)
    reps = tk // LANES

    def block(j, masked, r0=0, nrows=2 * tq):
        rs = slice(r0, r0 + nrows)
        start = pl.multiple_of(j * tk, tk)
        ka = jnp.concatenate([k_ref[pl.ds(start, tk), :], kb_ref[pl.ds(start, tk), :]], axis=1)
        s = lax.dot_general(qs[rs], ka, NT, preferred_element_type=F32)
        if masked:
            r = lax.broadcasted_iota(jnp.int32, s.shape, 0) + r0
            c = lax.broadcasted_iota(jnp.int32, s.shape, 1)
            qpos = i * tq + jnp.where(r >= tq, r - tq, r)
            s = jnp.where(start + c <= qpos, s, -jnp.inf)
        m_old = m_sc[rs]
        m_new = jnp.maximum(m_old, jnp.max(s, axis=-1, keepdims=True))
        alpha = jnp.exp2(m_old - m_new)
        p = jnp.exp2(s - jnp.tile(m_new, (1, reps)))
        l_sc[rs] = alpha * l_sc[rs] + jnp.sum(p, axis=-1, keepdims=True)
        acc_sc[rs] = alpha * acc_sc[rs] + _dot(p.astype(BF16), v_ref[pl.ds(start, tk), :])
        m_sc[rs] = m_new

    n_full = i * (tq // tk)

    per_body = 4

    def body(jj, carry):
        for u in range(per_body):
            block(per_body * jj + u, False)
        return carry

    lax.fori_loop(0, n_full // per_body, body, 0)
    left = n_full % per_body
    for bit in (2, 1):
        @pl.when((left & bit) != 0)
        def _(bit=bit):
            first = n_full - (left & (2 * bit - 1))
            for u in range(bit):
                block(first + u, False)

    block(n_full, True)
    for d in range(1, tq // tk):
        for e in range(2):
            block(n_full + d, True, r0=e * tq + d * tk, nrows=tq - d * tk)
    o = acc_sc[...] / l_sc[...]
    o_ref[...] = jnp.where(lane < A_DIM, o[:tq], o[tq:]).astype(o_ref.dtype)


def _fox_prompt(qa, ka, va, kbias, batch, seq, tq, tk):
    assert tq % tk == 0
    nq = seq // tq
    pairs = A_HEADS // 2
    kv_spec = pl.BlockSpec((seq, LANES), lambda b, p, i: (b, p))
    q_spec = pl.BlockSpec((tq, LANES), lambda b, p, i: (b * nq + i, p))
    return pl.pallas_call(
        functools.partial(_fox_prompt_kernel, tq=tq, tk=tk),
        grid=(batch, pairs, nq),
        in_specs=[q_spec, kv_spec, kv_spec, kv_spec],
        out_specs=q_spec,
        out_shape=jax.ShapeDtypeStruct(qa.shape, BF16),
        scratch_shapes=[pltpu.VMEM((2 * tq, LANES), F32), pltpu.VMEM((2 * tq, LANES), F32),
                        pltpu.VMEM((2 * tq, LANES), F32)],
        compiler_params=_cparams("arbitrary", "arbitrary", "arbitrary"),
        name="fox_prompt",
    )(qa, ka, va, kbias)


def _fox_sample_kernel(pt_ref, q_ref, kn_ref, vn_ref, lfn_ref, tri_ref, sfx_ref, *rest, pages):
    page_refs = rest[:3 * pages]
    o_ref, m_sc, l_sc, acc_sc, carry_sc = rest[3 * pages:]
    g = pl.program_id(1)
    halves = 2
    hh = A_HEADS // halves
    nq = q_ref.shape[2] // hh

    def tiles(refs, half):
        return jnp.concatenate(
            [r[0, 0, half * hh:(half + 1) * hh].reshape(hh * A_DIM, PAGE).astype(BF16) for r in refs], axis=1)

    def rows_of(bias, half):
        return jnp.concatenate(
            [jnp.broadcast_to(bias[h:h + 1, :], (nq, bias.shape[1])) for h in range(half * hh, (half + 1) * hh)],
            axis=0) * LOG2E

    def update(half, s_list, v_groups):
        m_old = m_sc[half]
        m_row = jnp.max(s_list[0], axis=-1, keepdims=True)
        for s in s_list[1:]:
            m_row = jnp.maximum(m_row, jnp.max(s, axis=-1, keepdims=True))
        m_new = jnp.maximum(m_old, m_row)
        alpha = jnp.exp2(m_old - m_new)
        l_row = None
        pv = None
        for s, v_refs in zip(s_list, v_groups):
            p = jnp.exp2(s - jnp.tile(m_new, (1, s.shape[1] // LANES)))
            ls = jnp.sum(p, axis=-1, keepdims=True)
            l_row = ls if l_row is None else l_row + ls
            d = lax.dot_general(p.astype(BF16), tiles(v_refs, half), NT, preferred_element_type=F32)
            pv = d if pv is None else pv + d
        l_sc[half] = alpha * l_sc[half] + l_row
        acc_sc[half] = jnp.tile(alpha, (1, hh * A_DIM // LANES)) * acc_sc[half] + pv
        m_sc[half] = m_new

    @pl.when(g == 0)
    def _():
        m_sc[...] = jnp.full(m_sc.shape, -jnp.inf, F32)
        l_sc[...] = jnp.zeros(l_sc.shape, F32)
        acc_sc[...] = jnp.zeros(acc_sc.shape, F32)
        carry_sc[...] = jnp.zeros(carry_sc.shape, F32)
        lfn = lfn_ref[0, 0]
        cn = _dot_sel(jnp.concatenate([lfn, jnp.zeros_like(lfn)], axis=0), tri_ref[...])[:A_HEADS]
        for half in range(halves):
            s = _dot(q_ref[0, half], tiles([kn_ref], half)) - rows_of(cn, half)
            r = lax.broadcasted_iota(jnp.int32, s.shape, 0)
            c = lax.broadcasted_iota(jnp.int32, s.shape, 1)
            s = jnp.where(c <= (r & (nq - 1)), s, -jnp.inf)
            update(half, [s], [[vn_ref]])

    order = list(reversed(range(pages)))
    lf_all = jnp.concatenate([page_refs[2 * pages + p][0, 0] for p in order], axis=0)
    sfx_all = _dot_sel(lf_all, sfx_ref[...])
    carry = carry_sc[...]
    biases = []
    for i in range(pages):
        blk = sfx_all[i * A_HEADS:(i + 1) * A_HEADS]
        biases.append(blk[:, :PAGE] + carry)
        carry = carry + blk[:, PAGE:]
    carry_sc[...] = carry
    pairs = [(order[i], order[i + 1], jnp.concatenate(biases[i:i + 2], axis=1)) for i in range(0, pages, 2)]
    for half in range(halves):
        s_list = [_dot(q_ref[0, half], tiles([page_refs[a], page_refs[b]], half)) + rows_of(bias, half)
                  for a, b, bias in pairs]
        update(half, s_list, [[page_refs[pages + a], page_refs[pages + b]] for a, b, _ in pairs])

    @pl.when(g == pl.num_programs(1) - 1)
    def _():
        for half in range(halves):
            o_ref[0, half] = acc_sc[half] / jnp.tile(l_sc[half], (1, hh * A_DIM // LANES))


def _fox_sample(page_table, q, kn, vn, lfn, cache_kt, cache_vt, cache_lft, layer, pages=16):
    db, n_pages = page_table.shape
    groups = n_pages // pages
    assert pages % 2 == 0
    state = q.shape[1:3] + (LANES,)
    idx = np.arange(PAGE)
    tri = jnp.asarray(idx[:, None] <= idx[None, :], BF16)
    sfx = jnp.asarray(np.concatenate([idx[:, None] > idx[None, :], np.ones((PAGE, PAGE), bool)], 1), BF16)
    kv_page = (A_HEADS, A_DIM, PAGE)
    lf_page = (A_HEADS, PAGE)

    def per_b(shape):
        nd = len(shape)
        return pl.BlockSpec((1,) + shape, lambda b, g, pt: (b,) + (0,) * nd)

    def new_page(shape):
        nd = len(shape)
        return pl.BlockSpec((1, 1) + shape, lambda b, g, pt: (0, b) + (0,) * nd)

    def const(shape):
        nd = len(shape)
        return pl.BlockSpec(shape, lambda b, g, pt: (0,) * nd)

    def page_spec(shape, p):
        nd = len(shape)
        return pl.BlockSpec(
            (1, 1) + shape,
            lambda b, g, pt: (layer, pt[b * n_pages + (groups - 1 - g) * pages + p]) + (0,) * nd)

    in_specs = [per_b(q.shape[1:]), new_page(kv_page), new_page(kv_page), new_page(lf_page),
                const(tri.shape), const(sfx.shape)]
    in_specs += [page_spec(kv_page, p) for p in range(pages)]
    in_specs += [page_spec(kv_page, p) for p in range(pages)]
    in_specs += [page_spec(lf_page, p) for p in range(pages)]
    grid_spec = pltpu.PrefetchScalarGridSpec(
        num_scalar_prefetch=1,
        grid=(db, groups),
        in_specs=in_specs,
        out_specs=per_b(q.shape[1:]),
        scratch_shapes=[pltpu.VMEM(state, F32), pltpu.VMEM(state, F32),
                        pltpu.VMEM(q.shape[1:], F32), pltpu.VMEM((A_HEADS, LANES), F32)],
    )
    return pl.pallas_call(
        functools.partial(_fox_sample_kernel, pages=pages),
        grid_spec=grid_spec,
        out_shape=jax.ShapeDtypeStruct(q.shape, F32),
        compiler_params=_cparams("arbitrary", "arbitrary"),
        name="fox_sample",
    )(page_table.reshape(-1), q, kn, vn, lfn, tri, sfx,
      *([cache_kt] * pages), *([cache_vt] * pages), *([cache_lft] * pages))


def _gla_chunk(q, k, v, la, s_old, tril, row0, ones, cast):
    c = q.shape[0]
    bc = jnp.dot(tril, la, precision=HIGHEST, preferred_element_type=F32)
    ref = bc[c // 2 - 1:c // 2, :]
    last = bc[c - 1:c, :]
    qt = q * jnp.exp(bc - ref)
    kt = k * jnp.exp(ref - bc)
    o = _dot(cast(qt * jnp.exp(ref)), cast(s_old))
    a = lax.dot_general(cast(qt), cast(kt), NT, preferred_element_type=F32)
    a = jnp.where(tril > 0, a, 0.0)
    vc = cast(v)
    o = o + _dot(cast(a), vc)
    kd = cast(kt * jnp.exp(last - ref))
    scale = lax.dot_general(row0 * jnp.exp(last), ones, TN, precision=HIGHEST, preferred_element_type=F32)
    s_new = scale * s_old + lax.dot_general(kd, vc, TN, preferred_element_type=F32)
    return o, s_new


def _gla_prompt_kernel(q_ref, k_ref, v_ref, la_ref, tril_ref, o_ref, st_ref, *, chunk):
    @pl.when(pl.program_id(0) == 0)
    def _():
        st_ref[...] = jnp.zeros(st_ref.shape, F32)

    tril = tril_ref[...]
    r = lax.broadcasted_iota(jnp.int32, tril.shape, 0)
    c = lax.broadcasted_iota(jnp.int32, tril.shape, 1)
    causal = c <= r

    nb = q_ref.shape[0]
    chains = [(b, h) for b in range(nb) for h in range(B_HEADS)]
    kcols = lambda h: slice(h * B_DK, (h + 1) * B_DK)
    vcols = lambda h: slice(h * B_DV, (h + 1) * B_DV)

    def body(ci, states):
        rows = pl.ds(pl.multiple_of(ci * chunk, chunk), chunk)
        qs, kd, decay, qtb, ktb = [], [], [], [], []
        for b in range(nb):
            bc = _sel_dot(tril, la_ref[b, rows, :])
            ref = bc[chunk // 2 - 1:chunk // 2, :]
            last = bc[chunk - 1:chunk, :]
            qt = q_ref[b, rows, :] * jnp.exp(bc - ref)
            kt = k_ref[b, rows, :] * jnp.exp(ref - bc)
            qs.append((qt * jnp.exp(ref)).astype(BF16))
            kd.append((kt * jnp.exp(last - ref)).astype(BF16))
            decay.append(jnp.exp(last))
            qtb.append(qt.astype(BF16))
            ktb.append(kt.astype(BF16))
        vbs = [v_ref[b, rows, vcols(h)].astype(BF16) for b, h in chains]
        intra = []
        for (b, h), vb in zip(chains, vbs):
            a = lax.dot_general(qtb[b][:, kcols(h)], ktb[b][:, kcols(h)], NT, preferred_element_type=F32)
            intra.append(_dot(jnp.where(causal, a, 0.0).astype(BF16), vb))
        new_states = []
        for (b, h), vb, st, o_intra in zip(chains, vbs, states, intra):
            o = lax.dot_general(qs[b][:, kcols(h)], st.astype(BF16), NT, preferred_element_type=F32)
            o_ref[b, rows, vcols(h)] = o + o_intra
            new_states.append(st * decay[b][:, kcols(h)]
                              + lax.dot_general(vb, kd[b][:, kcols(h)], TN, preferred_element_type=F32))
        return tuple(new_states)

    states = lax.fori_loop(0, q_ref.shape[1] // chunk, body, tuple(st_ref[b, h] for b, h in chains),
                           unroll=2)
    for (b, h), st in zip(chains, states):
        st_ref[b, h] = st


def _gla_consts(chunk):
    tril = jnp.tril(jnp.ones((chunk, chunk), F32))
    row0 = jnp.zeros((8, B_DK), F32).at[0].set(1.0)
    ones = jnp.ones((8, B_DV), F32)
    return tril, row0, ones


def _gla_prompt(qb, kb, vb, la, batch, seq, rows, chunk=64):
    kw, vw = B_HEADS * B_DK, B_HEADS * B_DV
    tril = jnp.tril(jnp.ones((chunk, chunk), BF16))
    rk = pl.BlockSpec((batch, rows, kw), lambda i: (0, i, 0))
    rv = pl.BlockSpec((batch, rows, vw), lambda i: (0, i, 0))
    return pl.pallas_call(
        functools.partial(_gla_prompt_kernel, chunk=chunk),
        grid=(seq // rows,),
        in_specs=[rk, rk, rv, rk, _resident(tril.shape)],
        out_specs=[rv, pl.BlockSpec((batch, B_HEADS, B_DV, B_DK), lambda i: (0, 0, 0, 0))],
        out_shape=[jax.ShapeDtypeStruct(vb.shape, F32),
                   jax.ShapeDtypeStruct((batch, B_HEADS, B_DV, B_DK), F32)],
        compiler_params=_cparams("arbitrary"),
        name="gla_prompt",
    )(qb, kb, vb, la, tril)


def _gla_sample_kernel(q_ref, k_ref, v_ref, la_ref, s_in_ref, tril_ref, row0_ref, ones_ref, o_ref, s_ref, *, t):
    cast = lambda a: a
    tril, row0, ones = tril_ref[...], row0_ref[...], ones_ref[...]
    results = []
    for b in range(q_ref.shape[0] // t):
        rows = slice(b * t, (b + 1) * t)
        for h in range(B_HEADS):
            kc = slice(h * B_DK, (h + 1) * B_DK)
            vc = slice(h * B_DV, (h + 1) * B_DV)
            results.append((b, h, rows, vc,
                            _gla_chunk(q_ref[rows, kc], k_ref[rows, kc], v_ref[rows, vc], la_ref[rows, kc],
                                       s_in_ref[b, h], tril, row0, ones, cast)))
    for b, h, rows, vc, (o, s_new) in results:
        o_ref[rows, vc] = o
        s_ref[b, h] = s_new


def _gla_sample(qb, kb, vb, la, state, db, t, per_step=4):
    kw, vw = B_HEADS * B_DK, B_HEADS * B_DV
    consts = _gla_consts(t)
    rk = pl.BlockSpec((per_step * t, kw), lambda b: (b, 0))
    rv = pl.BlockSpec((per_step * t, vw), lambda b: (b, 0))
    st = pl.BlockSpec((per_step, B_HEADS, B_DK, B_DV), lambda b: (b, 0, 0, 0))
    return pl.pallas_call(
        functools.partial(_gla_sample_kernel, t=t),
        grid=(db // per_step,),
        in_specs=[rk, rk, rv, rk, st] + [_resident(c.shape) for c in consts],
        out_specs=[rv, st],
        out_shape=[jax.ShapeDtypeStruct(vb.shape, F32), jax.ShapeDtypeStruct(state.shape, F32)],
        compiler_params=_cparams("arbitrary"),
        name="gla_sample",
    )(qb, kb, vb, la, state, *consts)


def _mix_kernel(x_ref, oa_ref, ob_ref, gpre_ref, wr_ref, wga_ref, wgb_ref, ggla_ref,
                wpa_ref, wpb_ref, wout_ref, gpost_ref, y_ref):
    x = x_ref[...]
    hn = _rms(x, gpre_ref[...]).astype(BF16)
    r_b = _dot(hn, wr_ref[...])
    ob = ob_ref[...]
    parts = []
    for h in range(B_HEADS):
        seg = ob[:, h * B_DV:(h + 1) * B_DV]
        parts.append(_rms(seg, ggla_ref[...]))
    obn = jnp.concatenate(parts, axis=-1) * (r_b * jax.nn.sigmoid(r_b))
    pa = _dot(oa_ref[...], wpa_ref[...])
    pb = _dot(obn.astype(BF16), wpb_ref[...])
    merged = jax.nn.sigmoid(_dot(hn, wga_ref[...])) * pa + jax.nn.sigmoid(_dot(hn, wgb_ref[...])) * pb
    z = _dot(merged.astype(BF16), wout_ref[...])
    y_ref[...] = x + _rms(z, gpost_ref[...])


def _mix(x, oa, ob, g_pre, w_r, w_ga, w_gb, g_gla, w_pa, w_pb, w_out, g_post, tm):
    t, d = x.shape
    row = lambda n: pl.BlockSpec((tm, n), lambda i: (i, 0))
    consts = (g_pre, w_r, w_ga, w_gb, g_gla, w_pa, w_pb, w_out, g_post)
    return pl.pallas_call(
        _mix_kernel,
        grid=(t // tm,),
        in_specs=[row(d), row(oa.shape[1]), row(ob.shape[1])] + [_resident(c.shape) for c in consts],
        out_specs=row(d),
        out_shape=jax.ShapeDtypeStruct(x.shape, F32),
        compiler_params=_cparams("arbitrary"),
        name="mix",
    )(x, oa, ob, *consts)


def _gelu_tanh(g):
    return 0.5 * g * (1.0 + jnp.tanh(np.sqrt(2.0 / np.pi) * (g + 0.044715 * (g * g * g))))


def _ffn_body(x_ref, gpre_ref, wup_ref, wconv_ref, bconv_ref, wdown_ref, gpost_ref, y_ref, tail_ref,
              h_sc, shifted):
    x = x_ref[...]
    hn = _rms(x, gpre_ref[...]).astype(BF16)
    n_chunks = wup_ref.shape[0] // 2
    nc = wup_ref.shape[2]
    for j in range(n_chunks):
        halves = []
        for half in range(2):
            idx = half * n_chunks + j
            u = _dot(hn, wup_ref[idx])
            u1, u2 = shifted(idx, u)
            w = wconv_ref[idx]
            halves.append(bconv_ref[idx] + u2 * w[0:1] + u1 * w[1:2] + u * w[2:3])
            tail_ref[idx] = u[u.shape[0] - 8:, :]
        h_sc[:, j * nc:(j + 1) * nc] = (_gelu_tanh(halves[1]) * halves[0]).astype(BF16)
    y_ref[...] = x + _rms(_dot(h_sc[...], wdown_ref[...]), gpost_ref[...])


def _ffn_prompt_kernel(x_ref, gpre_ref, wup_ref, wconv_ref, bconv_ref, wdown_ref, gpost_ref,
                       y_ref, tail_ref, h_sc, hist_sc, *, tiles_per_seq):
    first = pl.program_id(0) % tiles_per_seq == 0

    @pl.when(first)
    def _():
        hist_sc[...] = jnp.zeros(hist_sc.shape, F32)

    def shifted(idx, u):
        rows = lax.broadcasted_iota(jnp.int32, u.shape, 0)
        p1 = hist_sc[idx, 7:8, :]
        p2 = hist_sc[idx, 6:7, :]
        u1 = jnp.where(rows == 0, p1, pltpu.roll(u, 1, axis=0))
        u2 = jnp.where(rows == 0, p2, jnp.where(rows == 1, p1, pltpu.roll(u, 2, axis=0)))
        hist_sc[idx] = u[u.shape[0] - 8:, :]
        return u1, u2

    _ffn_body(x_ref, gpre_ref, wup_ref, wconv_ref, bconv_ref, wdown_ref, gpost_ref, y_ref,
              tail_ref.at[0], h_sc, shifted)


def _ffn_sample_kernel(x_ref, st_ref, gpre_ref, wup_ref, wconv_ref, bconv_ref, wdown_ref,
                       gpost_ref, y_ref, u_ref, h_sc, *, t):
    x = x_ref[...]
    hn = _rms(x, gpre_ref[...]).astype(BF16)
    n_chunks = wup_ref.shape[0] // 2
    nc = wup_ref.shape[2]
    nseq = x.shape[0] // t
    pos = lax.broadcasted_iota(jnp.int32, (nseq, t, nc), 1)
    for j in range(n_chunks):
        halves = []
        for half in range(2):
            idx = half * n_chunks + j
            u = _dot(hn, wup_ref[idx])
            u_ref[idx] = u
            u = u.reshape(nseq, t, nc)
            st = st_ref[idx]
            old2 = jnp.broadcast_to(st[:, 0:1, :], u.shape)
            old1 = jnp.broadcast_to(st[:, 1:2, :], u.shape)
            u1 = jnp.where(pos < 1, old1, pltpu.roll(u, 1, axis=1))
            u2 = jnp.where(pos < 1, old2, jnp.where(pos < 2, old1, pltpu.roll(u, 2, axis=1)))
            w = wconv_ref[idx]
            halves.append(bconv_ref[idx] + u2 * w[0:1] + u1 * w[1:2] + u * w[2:3])
        h = (_gelu_tanh(halves[1]) * halves[0]).reshape(x.shape[0], nc)
        h_sc[:, j * nc:(j + 1) * nc] = h.astype(BF16)
    y_ref[...] = x + _rms(_dot(h_sc[...], wdown_ref[...]), gpost_ref[...])


def _ffn_prompt(x, g_pre, wup, wconv, bconv, wdown, g_post, seq, tm):
    t, d = x.shape
    nblk, _, nc = wup.shape
    tiles = t // tm
    row = pl.BlockSpec((tm, d), lambda i: (i, 0))
    consts = (g_pre, wup, wconv, bconv, wdown, g_post)
    return pl.pallas_call(
        functools.partial(_ffn_prompt_kernel, tiles_per_seq=seq // tm),
        grid=(tiles,),
        in_specs=[row] + [_resident(c.shape) for c in consts],
        out_specs=[row, pl.BlockSpec((1, nblk, 8, nc), lambda i: (i, 0, 0, 0))],
        out_shape=[jax.ShapeDtypeStruct(x.shape, F32), jax.ShapeDtypeStruct((tiles, nblk, 8, nc), F32)],
        scratch_shapes=[pltpu.VMEM((tm, wdown.shape[0]), BF16), pltpu.VMEM((nblk, 8, nc), F32)],
        compiler_params=_cparams("arbitrary"),
        name="ffn_prompt",
    )(x, *consts)


def _ffn_sample(x, hist, g_pre, wup, wconv, bconv, wdown, g_post, t):
    rows, d = x.shape
    nblk, _, nc = wup.shape
    consts = (g_pre, wup, wconv, bconv, wdown, g_post)
    return pl.pallas_call(
        functools.partial(_ffn_sample_kernel, t=t),
        grid=(1,),
        in_specs=[_resident(a.shape) for a in (x, hist) + consts],
        out_specs=[pl.BlockSpec(x.shape, lambda i: (0, 0)),
                   pl.BlockSpec((nblk, rows, nc), lambda i: (0, 0, 0))],
        out_shape=[jax.ShapeDtypeStruct(x.shape, F32), jax.ShapeDtypeStruct((nblk, rows, nc), F32)],
        scratch_shapes=[pltpu.VMEM((rows, wdown.shape[0]), BF16)],
        compiler_params=_cparams("arbitrary"),
        name="ffn_sample",
    )(x, hist, *consts)


def _pack_layer(w_in, w_alpha2, w_up, w_conv, b_conv, w_down, nc):
    aw, kw, vw = A_HEADS * A_DIM, B_HEADS * B_DK, B_HEADS * B_DV
    d = w_in.shape[0]
    offs = np.cumsum([aw, aw, aw, A_HEADS, kw, kw, vw, GATE_RANK, vw, d, d])[:-1].tolist()
    q_a, k_a, v_a, f_a, q_b, k_b, v_b, a_low, r_b, g_a, g_b = jnp.split(w_in, offs, axis=1)
    wbig = jnp.concatenate([q_a, k_a, v_a, q_b, k_b, v_b], axis=1).astype(BF16)
    pad = lambda a: jnp.pad(a, ((0, 0), (0, LANES - a.shape[1])))
    wsm = jnp.concatenate([pad(f_a), pad(a_low)], axis=1).astype(BF16)
    wa2 = jnp.pad(w_alpha2, ((0, LANES - GATE_RANK), (0, 0))).astype(BF16)
    dff = w_down.shape[0]
    n_chunks = dff // nc
    wup = w_up.reshape(d, 2 * n_chunks, nc).transpose(1, 0, 2).astype(BF16)
    wconv = w_conv.reshape(w_conv.shape[0], 2 * n_chunks, nc).transpose(1, 0, 2)
    bconv = b_conv.reshape(2 * n_chunks, 1, nc)
    wdown = w_down.astype(BF16)
    return dict(wbig=wbig, wsm=wsm, wa2=wa2, w_r=r_b.astype(BF16), w_ga=g_a.astype(BF16),
                w_gb=g_b.astype(BF16), wup=wup, wconv=wconv, bconv=bconv, wdown=wdown)


def kernel(x_prompt, x_sample, cache_k, cache_v, cache_logf, state_gla, state_conv, page_table,
           g_pre_mix, w_in, b_f, w_alpha2, b_alpha, g_gla, w_proj_a, w_proj_b, w_out, g_post_mix,
           g_pre_ffn, w_up, w_conv, b_conv, w_down, g_post_ffn):
    batch, seq, d = x_prompt.shape
    db, dt, _ = x_sample.shape
    depth = w_in.shape[0]
    n_pool = cache_k.shape[1]
    aw = A_HEADS * A_DIM
    dff2 = w_up.shape[2]
    nc = 256
    assert dt == 8 and depth >= 1

    yp = x_prompt.reshape(batch * seq, d)
    ys = x_sample.reshape(db * dt, d)
    outs = [[] for _ in range(10)]
    for l in range(depth):
        w = _pack_layer(w_in[l], w_alpha2[l], w_up[l], w_conv[l], b_conv[l], w_down[l], nc)
        row = lambda a: a[l].reshape(1, -1)
        inproj_w = (row(g_pre_mix), w["wbig"], w["wsm"], row(b_f), w["wa2"], row(b_alpha))
        mix_w = (row(g_pre_mix), w["w_r"], w["w_ga"], w["w_gb"], row(g_gla), w_proj_a[l].astype(BF16),
                 w_proj_b[l].astype(BF16), w_out[l].astype(BF16), row(g_post_mix))
        ffn_w = (row(g_pre_ffn), w["wup"], w["wconv"], w["bconv"], w["wdown"], row(g_post_ffn))

        qa, ka, va, kaf, vaf, logf, qb, kb, vb, la = _inproj(yp, *inproj_w, seq=seq, tm=512)
        oa = _fox_prompt(qa, ka, va, _key_bias_lanes(logf, batch, seq), batch, seq, tq=1024, tk=512)
        seqs = lambda a: a.reshape(batch, seq, a.shape[1])
        ob, gla_state_t = _gla_prompt(seqs(qb), seqs(kb), seqs(vb), seqs(la), batch, seq, rows=512)
        ob = ob.reshape(batch * seq, ob.shape[2])
        gla_state = gla_state_t.transpose(0, 1, 3, 2)
        x1 = _mix(yp, oa, ob, *mix_w, tm=512)
        ffn_tm = 1024
        yp, tails = _ffn_prompt(x1, *ffn_w, seq=seq, tm=ffn_tm)
        tiles_per_seq = seq // ffn_tm
        tails = tails.reshape(batch, tiles_per_seq, 2 * (dff2 // 2 // nc), 8, nc)[:, -1, :, 6:, :]
        conv_p = tails.transpose(0, 2, 1, 3).reshape(batch, 2, dff2)
        for lst, val in zip(outs[:5], (kaf.transpose(0, 3, 1, 2), vaf.transpose(0, 3, 1, 2),
                                       logf.reshape(batch, seq, A_HEADS), gla_state, conv_p)):
            lst.append(val)

        qa, ka, va, kaf, vaf, logf, qb, kb, vb, la = _inproj(ys, *inproj_w, seq=db * dt, tm=db * dt)
        kaf, vaf = (a.reshape(A_HEADS, A_DIM, db, dt).transpose(2, 0, 1, 3) for a in (kaf, vaf))
        hh = A_HEADS // 2
        eye = jnp.eye(hh, dtype=BF16)
        q_bd = jnp.einsum("btgld,lm->bgltmd", qa.reshape(db, dt, 2, hh, A_DIM), eye).reshape(
            db, 2, hh * dt, hh * A_DIM)
        padk = lambda a: jnp.pad(a[None], ((0, 0), (0, 0), (0, 0), (0, 0), (0, PAGE - dt)))
        lfn = jnp.pad(logf.reshape(1, db, dt, A_HEADS).transpose(0, 1, 3, 2),
                      ((0, 0), (0, 0), (0, 0), (0, PAGE - dt)))
        o_bd = _fox_sample(page_table, q_bd, padk(kaf), padk(vaf), lfn,
                           cache_k.transpose(0, 1, 3, 4, 2), cache_v.transpose(0, 1, 3, 4, 2),
                           cache_logf.transpose(0, 1, 3, 2), layer=l)
        oa = jnp.einsum("bgltmd,lm->btgld", o_bd.reshape(db, 2, hh, dt, hh, A_DIM),
                        jnp.eye(hh, dtype=F32)).reshape(db * dt, aw).astype(BF16)
        ob, gla_state = _gla_sample(qb, kb, vb, la, state_gla[l], db, dt)
        x1 = _mix(ys, oa, ob, *mix_w, tm=db * dt)
        hist = state_conv[l].astype(F32).reshape(db, state_conv.shape[2], dff2 // nc, nc).transpose(2, 0, 1, 3)
        ys, u = _ffn_sample(x1, hist, *ffn_w, t=dt)
        conv_s = u.transpose(1, 0, 2).reshape(db, dt, dff2)[:, dt - 2:, :]
        for lst, val in zip(outs[5:], (kaf.transpose(0, 3, 1, 2), vaf.transpose(0, 3, 1, 2),
                                       logf.reshape(db, dt, A_HEADS), gla_state, conv_s)):
            lst.append(val)

    stacked = [jnp.stack(o) for o in outs]
    return (yp.reshape(batch, seq, d), ys.reshape(db, dt, d), *stacked)
```

```python
import functools

import jax
import jax.numpy as jnp
import numpy as np
from jax import lax
from jax.experimental import pallas as pl
from jax.experimental.pallas import tpu as pltpu

F32 = jnp.float32
BF16 = jnp.bfloat16
EPS = 1e-6
GATE_TEMP = 16.0
PAGE = 128
A_HEADS, A_DIM = 8, 64
B_HEADS, B_DK, B_DV = 4, 128, 256
GATE_RANK = 16
LANES = 128
VMEM_LIMIT = 56 * 1024 * 1024
HIGHEST = lax.Precision.HIGHEST
LOG2E = float(np.log2(np.e))
NT = (((1,), (1,)), ((), ()))
TN = (((0,), (0,)), ((), ()))


def _cparams(*sem):
    return pltpu.CompilerParams(dimension_semantics=sem, vmem_limit_bytes=VMEM_LIMIT)


def _resident(shape):
    nd = len(shape)
    return pl.BlockSpec(shape, lambda *_: (0,) * nd, pipeline_mode=pl.Buffered(1))


def _rms(x, g):
    return x * lax.rsqrt(jnp.mean(x * x, axis=-1, keepdims=True) + EPS) * g


def _log_sigmoid(x):
    return jnp.minimum(x, 0.0) - jnp.log1p(jnp.exp(-jnp.abs(x)))


def _dot(a, b):
    return jnp.dot(a, b, preferred_element_type=F32)


def _trunc_to_bf16_bits(x):
    bits = lax.bitcast_convert_type(x, jnp.int32) & jnp.int32(-65536)
    return lax.bitcast_convert_type(bits, F32)


def _split3(x):
    hi = _trunc_to_bf16_bits(x)
    r1 = x - hi
    mid = _trunc_to_bf16_bits(r1)
    lo = r1 - mid
    return hi.astype(BF16), mid.astype(BF16), lo.astype(BF16)


def _dot_sel(x, sel):
    return sum(_dot(part, sel) for part in _split3(x))


def _sel_dot(sel, x):
    return sum(_dot(sel, part) for part in _split3(x))


def _inproj_kernel(x_ref, g_ref, wbig_ref, wsm_ref, bf_ref, wa2_ref, ba_ref,
                   qa_ref, ka_ref, va_ref, kaf_ref, vaf_ref, logf_ref,
                   qb_ref, kb_ref, vb_ref, la_ref):
    hn = _rms(x_ref[...], g_ref[...]).astype(BF16)
    aw = A_HEADS * A_DIM
    kw = B_HEADS * B_DK

    def mm(lo, hi):
        return _dot(hn, wbig_ref[:, lo:hi])

    qa_ref[...] = (mm(0, aw) * (A_DIM ** -0.5 * LOG2E)).astype(BF16)
    k_a = mm(aw, 2 * aw)
    kaf_ref[0] = k_a.T.reshape(A_HEADS, A_DIM, k_a.shape[0])
    ka_ref[...] = k_a.astype(BF16)
    v_a = mm(2 * aw, 3 * aw)
    vaf_ref[0] = v_a.T.reshape(A_HEADS, A_DIM, v_a.shape[0])
    va_ref[...] = v_a.astype(BF16)
    o = 3 * aw
    qb_ref[...] = mm(o, o + kw) * B_DK ** -0.5
    kb_ref[...] = mm(o + kw, o + 2 * kw)
    vb_ref[...] = mm(o + 2 * kw, o + 2 * kw + B_HEADS * B_DV)
    us = _dot(hn, wsm_ref[...])
    logf_ref[...] = _log_sigmoid(us[:, :A_HEADS] + bf_ref[...])
    z = _dot(us[:, LANES:].astype(BF16), wa2_ref[...]) + ba_ref[...]
    la_ref[...] = _log_sigmoid(z) / GATE_TEMP


def _inproj(x, g_pre, wbig, wsm, b_f, wa2, b_alpha, seq, tm):
    t, d = x.shape
    aw, kw, vw = A_HEADS * A_DIM, B_HEADS * B_DK, B_HEADS * B_DV
    row = lambda n: pl.BlockSpec((tm, n), lambda i: (i, 0))
    outs = [(aw, BF16)] * 3 + [(aw, F32)] * 2 + [(A_HEADS, F32), (kw, F32), (kw, F32), (vw, F32), (kw, F32)]
    out_specs = [row(n) for n, _ in outs]
    out_shape = [jax.ShapeDtypeStruct((t, n), dt) for n, dt in outs]
    tiles_per_seq = seq // tm
    for i in (3, 4):
        out_specs[i] = pl.BlockSpec((1, A_HEADS, A_DIM, tm),
                                    lambda i: (i // tiles_per_seq, 0, 0, i % tiles_per_seq))
        out_shape[i] = jax.ShapeDtypeStruct((t // seq, A_HEADS, A_DIM, seq), F32)
    return pl.pallas_call(
        _inproj_kernel,
        grid=(t // tm,),
        in_specs=[row(d), _resident(g_pre.shape), _resident(wbig.shape), _resident(wsm.shape),
                  _resident(b_f.shape), _resident(wa2.shape), _resident(b_alpha.shape)],
        out_specs=out_specs,
        out_shape=out_shape,
        compiler_params=_cparams("arbitrary"),
        name="inproj",
    )(x, g_pre, wbig, wsm, b_f, wa2, b_alpha)


def _key_bias_kernel(lf_ref, tril_ref, place_ref, o_ref, carry_sc, *, seg):
    @pl.when(pl.program_id(1) == 0)
    def _():
        carry_sc[...] = jnp.zeros(carry_sc.shape, F32)

    carry = carry_sc[...]
    for s in range(lf_ref.shape[0] // seg):
        rows = slice(s * seg, (s + 1) * seg)
        c = _sel_dot(tril_ref[...], lf_ref[rows, :]) + carry
        carry = c[seg - 1:seg, :]
        x = -c * LOG2E
        hi = _trunc_to_bf16_bits(x)
        r1 = x - hi
        mid = _trunc_to_bf16_bits(r1)
        parts = (hi, mid, r1 - mid)
        o_ref[rows, :] = sum(_dot(p, place_ref[j]) for j, p in enumerate(parts)).astype(o_ref.dtype)
    carry_sc[...] = carry


def _key_bias_lanes(logf, batch, seq, rows=2048, seg=256):
    heads = logf.shape[1]
    tril = jnp.tril(jnp.ones((seg, seg), BF16))
    place = np.zeros((3, heads, heads // 2 * LANES), np.float32)
    for j in range(3):
        for h in range(heads):
            place[j, h, (h // 2) * LANES + (h % 2) * 3 + j] = 1.0
    place = jnp.asarray(place)
    steps = seq // rows
    return pl.pallas_call(
        functools.partial(_key_bias_kernel, seg=seg),
        grid=(batch, steps),
        in_specs=[pl.BlockSpec((rows, heads), lambda b, i: (b * steps + i, 0)),
                  _resident(tril.shape), _resident(place.shape)],
        out_specs=pl.BlockSpec((rows, place.shape[2]), lambda b, i: (b * steps + i, 0)),
        out_shape=jax.ShapeDtypeStruct((batch * seq, place.shape[2]), BF16),
        scratch_shapes=[pltpu.VMEM((1, heads), F32)],
        compiler_params=_cparams("arbitrary", "arbitrary"),
        name="key_bias",
    )(logf, tril, place)


def _fox_prompt_kernel(q_ref, k_ref, v_ref, kb_ref, o_ref, m_sc, l_sc, acc_sc, *, tq, tk):
    i = pl.program_id(2)
    q = q_ref[...]
    lane = lax.broadcasted_iota(jnp.int32, q.shape, 1)
    zero = jnp.zeros_like(q)
    sel_even = jnp.where(lane < 3, 1.0, 0.0).astype(BF16)
    sel_odd = jnp.where(lane < 3, 0.0, jnp.where(lane < 6, 1.0, 0.0)).astype(BF16)
    q_even = jnp.concatenate([jnp.where(lane < A_DIM, q, zero), sel_even], axis=1)
    q_odd = jnp.concatenate([jnp.where(lane >= A_DIM, q, zero), sel_odd], axis=1)
    qs = jnp.concatenate([q_even, q_odd], axis=0)
    m_sc[...] = jnp.full(m_sc.shape, -jnp.inf, F32)
    l_sc[...] = jnp.zeros(l_sc.shape, F32)
    acc_sc[...] = jnp.zeros(acc_sc.shape, F32)
    reps = tk // LANES

    def block(j, masked, r0=0, nrows=2 * tq):
        rs = slice(r0, r0 + nrows)
        start = pl.multiple_of(j * tk, tk)
        ka = jnp.concatenate([k_ref[pl.ds(start, tk), :], kb_ref[pl.ds(start, tk), :]], axis=1)
        s = lax.dot_general(qs[rs], ka, NT, preferred_element_type=F32)
        if masked:
            r = lax.broadcasted_iota(jnp.int32, s.shape, 0) + r0
            c = lax.broadcasted_iota(jnp.int32, s.shape, 1)
            qpos = i * tq + jnp.where(r >= tq, r - tq, r)
            s = jnp.where(start + c <= qpos, s, -jnp.inf)
        m_old = m_sc[rs]
        m_new = jnp.maximum(m_old, jnp.max(s, axis=-1, keepdims=True))
        alpha = jnp.exp2(m_old - m_new)
        p = jnp.exp2(s - jnp.tile(m_new, (1, reps)))
        l_sc[rs] = alpha * l_sc[rs] + jnp.sum(p, axis=-1, keepdims=True)
        acc_sc[rs] = alpha * acc_sc[rs] + _dot(p.astype(BF16), v_ref[pl.ds(start, tk), :])
        m_sc[rs] = m_new

    n_full = i * (tq // tk)

    per_body = 4

    def body(jj, carry):
        for u in range(per_body):
            block(per_body * jj + u, False)
        return carry

    lax.fori_loop(0, n_full // per_body, body, 0)
    left = n_full % per_body
    for bit in (2, 1):
        @pl.when((left & bit) != 0)
        def _(bit=bit):
            first = n_full - (left & (2 * bit - 1))
            for u in range(bit):
                block(first + u, False)

    block(n_full, True)
    for d in range(1, tq // tk):
        for e in range(2):
            block(n_full + d, True, r0=e * tq + d * tk, nrows=tq - d * tk)
    o = acc_sc[...] / l_sc[...]
    o_ref[...] = jnp.where(lane < A_DIM, o[:tq], o[tq:]).astype(o_ref.dtype)


def _fox_prompt(qa, ka, va, kbias, batch, seq, tq, tk):
    assert tq % tk == 0
    nq = seq // tq
    pairs = A_HEADS // 2
    kv_spec = pl.BlockSpec((seq, LANES), lambda b, p, i: (b, p))
    q_spec = pl.BlockSpec((tq, LANES), lambda b, p, i: (b * nq + i, p))
    return pl.pallas_call(
        functools.partial(_fox_prompt_kernel, tq=tq, tk=tk),
        grid=(batch, pairs, nq),
        in_specs=[q_spec, kv_spec, kv_spec, kv_spec],
        out_specs=q_spec,
        out_shape=jax.ShapeDtypeStruct(qa.shape, BF16),
        scratch_shapes=[pltpu.VMEM((2 * tq, LANES), F32), pltpu.VMEM((2 * tq, LANES), F32),
                        pltpu.VMEM((2 * tq, LANES), F32)],
        compiler_params=_cparams("arbitrary", "arbitrary", "arbitrary"),
        name="fox_prompt",
    )(qa, ka, va, kbias)


def _fox_sample_kernel(pt_ref, q_ref, kn_ref, vn_ref, lfn_ref, tri_ref, sfx_ref, *rest, pages):
    page_refs = rest[:3 * pages]
    o_ref, m_sc, l_sc, acc_sc, carry_sc = rest[3 * pages:]
    g = pl.program_id(1)
    halves = 2
    hh = A_HEADS // halves
    nq = q_ref.shape[2] // hh

    def tiles(refs, half):
        return jnp.concatenate(
            [r[0, 0, half * hh:(half + 1) * hh].reshape(hh * A_DIM, PAGE).astype(BF16) for r in refs], axis=1)

    def rows_of(bias, half):
        return jnp.concatenate(
            [jnp.broadcast_to(bias[h:h + 1, :], (nq, bias.shape[1])) for h in range(half * hh, (half + 1) * hh)],
            axis=0) * LOG2E

    def update(half, s_list, v_groups):
        m_old = m_sc[half]
        m_row = jnp.max(s_list[0], axis=-1, keepdims=True)
        for s in s_list[1:]:
            m_row = jnp.maximum(m_row, jnp.max(s, axis=-1, keepdims=True))
        m_new = jnp.maximum(m_old, m_row)
        alpha = jnp.exp2(m_old - m_new)
        l_row = None
        pv = None
        for s, v_refs in zip(s_list, v_groups):
            p = jnp.exp2(s - jnp.tile(m_new, (1, s.shape[1] // LANES)))
            ls = jnp.sum(p, axis=-1, keepdims=True)
            l_row = ls if l_row is None else l_row + ls
            d = lax.dot_general(p.astype(BF16), tiles(v_refs, half), NT, preferred_element_type=F32)
            pv = d if pv is None else pv + d
        l_sc[half] = alpha * l_sc[half] + l_row
        acc_sc[half] = jnp.tile(alpha, (1, hh * A_DIM // LANES)) * acc_sc[half] + pv
        m_sc[half] = m_new

    @pl.when(g == 0)
    def _():
        m_sc[...] = jnp.full(m_sc.shape, -jnp.inf, F32)
        l_sc[...] = jnp.zeros(l_sc.shape, F32)
        acc_sc[...] = jnp.zeros(acc_sc.shape, F32)
        carry_sc[...] = jnp.zeros(carry_sc.shape, F32)
        lfn = lfn_ref[0, 0]
        cn = _dot_sel(jnp.concatenate([lfn, jnp.zeros_like(lfn)], axis=0), tri_ref[...])[:A_HEADS]
        for half in range(halves):
            s = _dot(q_ref[0, half], tiles([kn_ref], half)) - rows_of(cn, half)
            r = lax.broadcasted_iota(jnp.int32, s.shape, 0)
            c = lax.broadcasted_iota(jnp.int32, s.shape, 1)
            s = jnp.where(c <= (r & (nq - 1)), s, -jnp.inf)
            update(half, [s], [[vn_ref]])

    order = list(reversed(range(pages)))
    lf_all = jnp.concatenate([page_refs[2 * pages + p][0, 0] for p in order], axis=0)
    sfx_all = _dot_sel(lf_all, sfx_ref[...])
    carry = carry_sc[...]
    biases = []
    for i in range(pages):
        blk = sfx_all[i * A_HEADS:(i + 1) * A_HEADS]
        biases.append(blk[:, :PAGE] + carry)
        carry = carry + blk[:, PAGE:]
    carry_sc[...] = carry
    pairs = [(order[i], order[i + 1], jnp.concatenate(biases[i:i + 2], axis=1)) for i in range(0, pages, 2)]
    for half in range(halves):
        s_list = [_dot(q_ref[0, half], tiles([page_refs[a], page_refs[b]], half)) + rows_of(bias, half)
                  for a, b, bias in pairs]
        update(half, s_list, [[page_refs[pages + a], page_refs[pages + b]] for a, b, _ in pairs])

    @pl.when(g == pl.num_programs(1) - 1)
    def _():
        for half in range(halves):
            o_ref[0, half] = acc_sc[half] / jnp.tile(l_sc[half], (1, hh * A_DIM // LANES))


def _fox_sample(page_table, q, kn, vn, lfn, cache_kt, cache_vt, cache_lft, layer, pages=16):
    db, n_pages = page_table.shape
    groups = n_pages // pages
    assert pages % 2 == 0
    state = q.shape[1:3] + (LANES,)
    idx = np.arange(PAGE)
    tri = jnp.asarray(idx[:, None] <= idx[None, :], BF16)
    sfx = jnp.asarray(np.concatenate([idx[:, None] > idx[None, :], np.ones((PAGE, PAGE), bool)], 1), BF16)
    kv_page = (A_HEADS, A_DIM, PAGE)
    lf_page = (A_HEADS, PAGE)

    def per_b(shape):
        nd = len(shape)
        return pl.BlockSpec((1,) + shape, lambda b, g, pt: (b,) + (0,) * nd)

    def new_page(shape):
        nd = len(shape)
        return pl.BlockSpec((1, 1) + shape, lambda b, g, pt: (0, b) + (0,) * nd)

    def const(shape):
        nd = len(shape)
        return pl.BlockSpec(shape, lambda b, g, pt: (0,) * nd)

    def page_spec(shape, p):
        nd = len(shape)
        return pl.BlockSpec(
            (1, 1) + shape,
            lambda b, g, pt: (layer, pt[b * n_pages + (groups - 1 - g) * pages + p]) + (0,) * nd)

    in_specs = [per_b(q.shape[1:]), new_page(kv_page), new_page(kv_page), new_page(lf_page),
                const(tri.shape), const(sfx.shape)]
    in_specs += [page_spec(kv_page, p) for p in range(pages)]
    in_specs += [page_spec(kv_page, p) for p in range(pages)]
    in_specs += [page_spec(lf_page, p) for p in range(pages)]
    grid_spec = pltpu.PrefetchScalarGridSpec(
        num_scalar_prefetch=1,
        grid=(db, groups),
        in_specs=in_specs,
        out_specs=per_b(q.shape[1:]),
        scratch_shapes=[pltpu.VMEM(state, F32), pltpu.VMEM(state, F32),
                        pltpu.VMEM(q.shape[1:], F32), pltpu.VMEM((A_HEADS, LANES), F32)],
    )
    return pl.pallas_call(
        functools.partial(_fox_sample_kernel, pages=pages),
        grid_spec=grid_spec,
        out_shape=jax.ShapeDtypeStruct(q.shape, F32),
        compiler_params=_cparams("arbitrary", "arbitrary"),
        name="fox_sample",
    )(page_table.reshape(-1), q, kn, vn, lfn, tri, sfx,
      *([cache_kt] * pages), *([cache_vt] * pages), *([cache_lft] * pages))


def _gla_chunk(q, k, v, la, s_old, tril, row0, ones, cast):
    c = q.shape[0]
    bc = jnp.dot(tril, la, precision=HIGHEST, preferred_element_type=F32)
    ref = bc[c // 2 - 1:c // 2, :]
    last = bc[c - 1:c, :]
    qt = q * jnp.exp(bc - ref)
    kt = k * jnp.exp(ref - bc)
    o = _dot(cast(qt * jnp.exp(ref)), cast(s_old))
    a = lax.dot_general(cast(qt), cast(kt), NT, preferred_element_type=F32)
    a = jnp.where(tril > 0, a, 0.0)
    vc = cast(v)
    o = o + _dot(cast(a), vc)
    kd = cast(kt * jnp.exp(last - ref))
    scale = lax.dot_general(row0 * jnp.exp(last), ones, TN, precision=HIGHEST, preferred_element_type=F32)
    s_new = scale * s_old + lax.dot_general(kd, vc, TN, preferred_element_type=F32)
    return o, s_new


def _gla_prompt_kernel(q_ref, k_ref, v_ref, la_ref, tril_ref, o_ref, st_ref, *, chunk):
    @pl.when(pl.program_id(0) == 0)
    def _():
        st_ref[...] = jnp.zeros(st_ref.shape, F32)

    tril = tril_ref[...]
    r = lax.broadcasted_iota(jnp.int32, tril.shape, 0)
    c = lax.broadcasted_iota(jnp.int32, tril.shape, 1)
    causal = c <= r

    nb = q_ref.shape[0]
    chains = [(b, h) for b in range(nb) for h in range(B_HEADS)]
    kcols = lambda h: slice(h * B_DK, (h + 1) * B_DK)
    vcols = lambda h: slice(h * B_DV, (h + 1) * B_DV)

    def body(ci, states):
        rows = pl.ds(pl.multiple_of(ci * chunk, chunk), chunk)
        qs, kd, decay, qtb, ktb = [], [], [], [], []
        for b in range(nb):
            bc = _sel_dot(tril, la_ref[b, rows, :])
            ref = bc[chunk // 2 - 1:chunk // 2, :]
            last = bc[chunk - 1:chunk, :]
            qt = q_ref[b, rows, :] * jnp.exp(bc - ref)
            kt = k_ref[b, rows, :] * jnp.exp(ref - bc)
            qs.append((qt * jnp.exp(ref)).astype(BF16))
            kd.append((kt * jnp.exp(last - ref)).astype(BF16))
            decay.append(jnp.exp(last))
            qtb.append(qt.astype(BF16))
            ktb.append(kt.astype(BF16))
        vbs = [v_ref[b, rows, vcols(h)].astype(BF16) for b, h in chains]
        intra = []
        for (b, h), vb in zip(chains, vbs):
            a = lax.dot_general(qtb[b][:, kcols(h)], ktb[b][:, kcols(h)], NT, preferred_element_type=F32)
            intra.append(_dot(jnp.where(causal, a, 0.0).astype(BF16), vb))
        new_states = []
        for (b, h), vb, st, o_intra in zip(chains, vbs, states, intra):
            o = lax.dot_general(qs[b][:, kcols(h)], st.astype(BF16), NT, preferred_element_type=F32)
            o_ref[b, rows, vcols(h)] = o + o_intra
            new_states.append(st * decay[b][:, kcols(h)]
                              + lax.dot_general(vb, kd[b][:, kcols(h)], TN, preferred_element_type=F32))
        return tuple(new_states)

    states = lax.fori_loop(0, q_ref.shape[1] // chunk, body, tuple(st_ref[b, h] for b, h in chains),
                           unroll=2)
    for (b, h), st in zip(chains, states):
        st_ref[b, h] = st


def _gla_consts(chunk):
    tril = jnp.tril(jnp.ones((chunk, chunk), F32))
    row0 = jnp.zeros((8, B_DK), F32).at[0].set(1.0)
    ones = jnp.ones((8, B_DV), F32)
    return tril, row0, ones


def _gla_prompt(qb, kb, vb, la, batch, seq, rows, chunk=64):
    kw, vw = B_HEADS * B_DK, B_HEADS * B_DV
    tril = jnp.tril(jnp.ones((chunk, chunk), BF16))
    rk = pl.BlockSpec((batch, rows, kw), lambda i: (0, i, 0))
    rv = pl.BlockSpec((batch, rows, vw), lambda i: (0, i, 0))
    return pl.pallas_call(
        functools.partial(_gla_prompt_kernel, chunk=chunk),
        grid=(seq // rows,),
        in_specs=[rk, rk, rv, rk, _resident(tril.shape)],
        out_specs=[rv, pl.BlockSpec((batch, B_HEADS, B_DV, B_DK), lambda i: (0, 0, 0, 0))],
        out_shape=[jax.ShapeDtypeStruct(vb.shape, F32),
                   jax.ShapeDtypeStruct((batch, B_HEADS, B_DV, B_DK), F32)],
        compiler_params=_cparams("arbitrary"),
        name="gla_prompt",
    )(qb, kb, vb, la, tril)


def _gla_sample_kernel(q_ref, k_ref, v_ref, la_ref, s_in_ref, tril_ref, row0_ref, ones_ref, o_ref, s_ref, *, t):
    cast = lambda a: a
    tril, row0, ones = tril_ref[...], row0_ref[...], ones_ref[...]
    results = []
    for b in range(q_ref.shape[0] // t):
        rows = slice(b * t, (b + 1) * t)
        for h in range(B_HEADS):
            kc = slice(h * B_DK, (h + 1) * B_DK)
            vc = slice(h * B_DV, (h + 1) * B_DV)
            results.append((b, h, rows, vc,
                            _gla_chunk(q_ref[rows, kc], k_ref[rows, kc], v_ref[rows, vc], la_ref[rows, kc],
                                       s_in_ref[b, h], tril, row0, ones, cast)))
    for b, h, rows, vc, (o, s_new) in results:
        o_ref[rows, vc] = o
        s_ref[b, h] = s_new


def _gla_sample(qb, kb, vb, la, state, db, t, per_step=4):
    kw, vw = B_HEADS * B_DK, B_HEADS * B_DV
    consts = _gla_consts(t)
    rk = pl.BlockSpec((per_step * t, kw), lambda b: (b, 0))
    rv = pl.BlockSpec((per_step * t, vw), lambda b: (b, 0))
    st = pl.BlockSpec((per_step, B_HEADS, B_DK, B_DV), lambda b: (b, 0, 0, 0))
    return pl.pallas_call(
        functools.partial(_gla_sample_kernel, t=t),
        grid=(db // per_step,),
        in_specs=[rk, rk, rv, rk, st] + [_resident(c.shape) for c in consts],
        out_specs=[rv, st],
        out_shape=[jax.ShapeDtypeStruct(vb.shape, F32), jax.ShapeDtypeStruct(state.shape, F32)],
        compiler_params=_cparams("arbitrary"),
        name="gla_sample",
    )(qb, kb, vb, la, state, *consts)


def _mix_kernel(x_ref, oa_ref, ob_ref, gpre_ref, wr_ref, wga_ref, wgb_ref, ggla_ref,
                wpa_ref, wpb_ref, wout_ref, gpost_ref, y_ref):
    x = x_ref[...]
    hn = _rms(x, gpre_ref[...]).astype(BF16)
    r_b = _dot(hn, wr_ref[...])
    ob = ob_ref[...]
    parts = []
    for h in range(B_HEADS):
        seg = ob[:, h * B_DV:(h + 1) * B_DV]
        parts.append(_rms(seg, ggla_ref[...]))
    obn = jnp.concatenate(parts, axis=-1) * (r_b * jax.nn.sigmoid(r_b))
    pa = _dot(oa_ref[...], wpa_ref[...])
    pb = _dot(obn.astype(BF16), wpb_ref[...])
    merged = jax.nn.sigmoid(_dot(hn, wga_ref[...])) * pa + jax.nn.sigmoid(_dot(hn, wgb_ref[...])) * pb
    z = _dot(merged.astype(BF16), wout_ref[...])
    y_ref[...] = x + _rms(z, gpost_ref[...])


def _mix(x, oa, ob, g_pre, w_r, w_ga, w_gb, g_gla, w_pa, w_pb, w_out, g_post, tm):
    t, d = x.shape
    row = lambda n: pl.BlockSpec((tm, n), lambda i: (i, 0))
    consts = (g_pre, w_r, w_ga, w_gb, g_gla, w_pa, w_pb, w_out, g_post)
    return pl.pallas_call(
        _mix_kernel,
        grid=(t // tm,),
        in_specs=[row(d), row(oa.shape[1]), row(ob.shape[1])] + [_resident(c.shape) for c in consts],
        out_specs=row(d),
        out_shape=jax.ShapeDtypeStruct(x.shape, F32),
        compiler_params=_cparams("arbitrary"),
        name="mix",
    )(x, oa, ob, *consts)


def _gelu_tanh(g):
    return 0.5 * g * (1.0 + jnp.tanh(np.sqrt(2.0 / np.pi) * (g + 0.044715 * (g * g * g))))


def _ffn_body(x_ref, gpre_ref, wup_ref, wconv_ref, bconv_ref, wdown_ref, gpost_ref, y_ref, tail_ref,
              h_sc, shifted):
    x = x_ref[...]
    hn = _rms(x, gpre_ref[...]).astype(BF16)
    n_chunks = wup_ref.shape[0] // 2
    nc = wup_ref.shape[2]
    for j in range(n_chunks):
        halves = []
        for half in range(2):
            idx = half * n_chunks + j
            u = _dot(hn, wup_ref[idx])
            u1, u2 = shifted(idx, u)
            w = wconv_ref[idx]
            halves.append(bconv_ref[idx] + u2 * w[0:1] + u1 * w[1:2] + u * w[2:3])
            tail_ref[idx] = u[u.shape[0] - 8:, :]
        h_sc[:, j * nc:(j + 1) * nc] = (_gelu_tanh(halves[1]) * halves[0]).astype(BF16)
    y_ref[...] = x + _rms(_dot(h_sc[...], wdown_ref[...]), gpost_ref[...])


def _ffn_prompt_kernel(x_ref, gpre_ref, wup_ref, wconv_ref, bconv_ref, wdown_ref, gpost_ref,
                       y_ref, tail_ref, h_sc, hist_sc, *, tiles_per_seq):
    first = pl.program_id(0) % tiles_per_seq == 0

    @pl.when(first)
    def _():
        hist_sc[...] = jnp.zeros(hist_sc.shape, F32)

    def shifted(idx, u):
        rows = lax.broadcasted_iota(jnp.int32, u.shape, 0)
        p1 = hist_sc[idx, 7:8, :]
        p2 = hist_sc[idx, 6:7, :]
        u1 = jnp.where(rows == 0, p1, pltpu.roll(u, 1, axis=0))
        u2 = jnp.where(rows == 0, p2, jnp.where(rows == 1, p1, pltpu.roll(u, 2, axis=0)))
        hist_sc[idx] = u[u.shape[0] - 8:, :]
        return u1, u2

    _ffn_body(x_ref, gpre_ref, wup_ref, wconv_ref, bconv_ref, wdown_ref, gpost_ref, y_ref,
              tail_ref.at[0], h_sc, shifted)


def _ffn_sample_kernel(x_ref, st_ref, gpre_ref, wup_ref, wconv_ref, bconv_ref, wdown_ref,
                       gpost_ref, y_ref, u_ref, h_sc, *, t):
    x = x_ref[...]
    hn = _rms(x, gpre_ref[...]).astype(BF16)
    n_chunks = wup_ref.shape[0] // 2
    nc = wup_ref.shape[2]
    nseq = x.shape[0] // t
    pos = lax.broadcasted_iota(jnp.int32, (nseq, t, nc), 1)
    for j in range(n_chunks):
        halves = []
        for half in range(2):
            idx = half * n_chunks + j
            u = _dot(hn, wup_ref[idx])
            u_ref[idx] = u
            u = u.reshape(nseq, t, nc)
            st = st_ref[idx]
            old2 = jnp.broadcast_to(st[:, 0:1, :], u.shape)
            old1 = jnp.broadcast_to(st[:, 1:2, :], u.shape)
            u1 = jnp.where(pos < 1, old1, pltpu.roll(u, 1, axis=1))
            u2 = jnp.where(pos < 1, old2, jnp.where(pos < 2, old1, pltpu.roll(u, 2, axis=1)))
            w = wconv_ref[idx]
            halves.append(bconv_ref[idx] + u2 * w[0:1] + u1 * w[1:2] + u * w[2:3])
        h = (_gelu_tanh(halves[1]) * halves[0]).reshape(x.shape[0], nc)
        h_sc[:, j * nc:(j + 1) * nc] = h.astype(BF16)
    y_ref[...] = x + _rms(_dot(h_sc[...], wdown_ref[...]), gpost_ref[...])


def _ffn_prompt(x, g_pre, wup, wconv, bconv, wdown, g_post, seq, tm):
    t, d = x.shape
    nblk, _, nc = wup.shape
    tiles = t // tm
    row = pl.BlockSpec((tm, d), lambda i: (i, 0))
    consts = (g_pre, wup, wconv, bconv, wdown, g_post)
    return pl.pallas_call(
        functools.partial(_ffn_prompt_kernel, tiles_per_seq=seq // tm),
        grid=(tiles,),
        in_specs=[row] + [_resident(c.shape) for c in consts],
        out_specs=[row, pl.BlockSpec((1, nblk, 8, nc), lambda i: (i, 0, 0, 0))],
        out_shape=[jax.ShapeDtypeStruct(x.shape, F32), jax.ShapeDtypeStruct((tiles, nblk, 8, nc), F32)],
        scratch_shapes=[pltpu.VMEM((tm, wdown.shape[0]), BF16), pltpu.VMEM((nblk, 8, nc), F32)],
        compiler_params=_cparams("arbitrary"),
        name="ffn_prompt",
    )(x, *consts)


def _ffn_sample(x, hist, g_pre, wup, wconv, bconv, wdown, g_post, t):
    rows, d = x.shape
    nblk, _, nc = wup.shape
    consts = (g_pre, wup, wconv, bconv, wdown, g_post)
    return pl.pallas_call(
        functools.partial(_ffn_sample_kernel, t=t),
        grid=(1,),
        in_specs=[_resident(a.shape) for a in (x, hist) + consts],
        out_specs=[pl.BlockSpec(x.shape, lambda i: (0, 0)),
                   pl.BlockSpec((nblk, rows, nc), lambda i: (0, 0, 0))],
        out_shape=[jax.ShapeDtypeStruct(x.shape, F32), jax.ShapeDtypeStruct((nblk, rows, nc), F32)],
        scratch_shapes=[pltpu.VMEM((rows, wdown.shape[0]), BF16)],
        compiler_params=_cparams("arbitrary"),
        name="ffn_sample",
    )(x, hist, *consts)


def _pack_layer(w_in, w_alpha2, w_up, w_conv, b_conv, w_down, nc):
    aw, kw, vw = A_HEADS * A_DIM, B_HEADS * B_DK, B_HEADS * B_DV
    d = w_in.shape[0]
    offs = np.cumsum([aw, aw, aw, A_HEADS, kw, kw, vw, GATE_RANK, vw, d, d])[:-1].tolist()
    q_a, k_a, v_a, f_a, q_b, k_b, v_b, a_low, r_b, g_a, g_b = jnp.split(w_in, offs, axis=1)
    wbig = jnp.concatenate([q_a, k_a, v_a, q_b, k_b, v_b], axis=1).astype(BF16)
    pad = lambda a: jnp.pad(a, ((0, 0), (0, LANES - a.shape[1])))
    wsm = jnp.concatenate([pad(f_a), pad(a_low)], axis=1).astype(BF16)
    wa2 = jnp.pad(w_alpha2, ((0, LANES - GATE_RANK), (0, 0))).astype(BF16)
    dff = w_down.shape[0]
    n_chunks = dff // nc
    wup = w_up.reshape(d, 2 * n_chunks, nc).transpose(1, 0, 2).astype(BF16)
    wconv = w_conv.reshape(w_conv.shape[0], 2 * n_chunks, nc).transpose(1, 0, 2)
    bconv = b_conv.reshape(2 * n_chunks, 1, nc)
    wdown = w_down.astype(BF16)
    return dict(wbig=wbig, wsm=wsm, wa2=wa2, w_r=r_b.astype(BF16), w_ga=g_a.astype(BF16),
                w_gb=g_b.astype(BF16), wup=wup, wconv=wconv, bconv=bconv, wdown=wdown)


def kernel(x_prompt, x_sample, cache_k, cache_v, cache_logf, state_gla, state_conv, page_table,
           g_pre_mix, w_in, b_f, w_alpha2, b_alpha, g_gla, w_proj_a, w_proj_b, w_out, g_post_mix,
           g_pre_ffn, w_up, w_conv, b_conv, w_down, g_post_ffn):
    batch, seq, d = x_prompt.shape
    db, dt, _ = x_sample.shape
    depth = w_in.shape[0]
    n_pool = cache_k.shape[1]
    aw = A_HEADS * A_DIM
    dff2 = w_up.shape[2]
    nc = 256
    assert dt == 8 and depth >= 1

    yp = x_prompt.reshape(batch * seq, d)
    ys = x_sample.reshape(db * dt, d)
    outs = [[] for _ in range(10)]
    for l in range(depth):
        w = _pack_layer(w_in[l], w_alpha2[l], w_up[l], w_conv[l], b_conv[l], w_down[l], nc)
        row = lambda a: a[l].reshape(1, -1)
        inproj_w = (row(g_pre_mix), w["wbig"], w["wsm"], row(b_f), w["wa2"], row(b_alpha))
        mix_w = (row(g_pre_mix), w["w_r"], w["w_ga"], w["w_gb"], row(g_gla), w_proj_a[l].astype(BF16),
                 w_proj_b[l].astype(BF16), w_out[l].astype(BF16), row(g_post_mix))
        ffn_w = (row(g_pre_ffn), w["wup"], w["wconv"], w["bconv"], w["wdown"], row(g_post_ffn))

        qa, ka, va, kaf, vaf, logf, qb, kb, vb, la = _inproj(yp, *inproj_w, seq=seq, tm=512)
        oa = _fox_prompt(qa, ka, va, _key_bias_lanes(logf, batch, seq), batch, seq, tq=1024, tk=512)
        seqs = lambda a: a.reshape(batch, seq, a.shape[1])
        ob, gla_state_t = _gla_prompt(seqs(qb), seqs(kb), seqs(vb), seqs(la), batch, seq, rows=512)
        ob = ob.reshape(batch * seq, ob.shape[2])
        gla_state = gla_state_t.transpose(0, 1, 3, 2)
        x1 = _mix(yp, oa, ob, *mix_w, tm=1024)
        ffn_tm = 1024
        yp, tails = _ffn_prompt(x1, *ffn_w, seq=seq, tm=ffn_tm)
        tiles_per_seq = seq // ffn_tm
        tails = tails.reshape(batch, tiles_per_seq, 2 * (dff2 // 2 // nc), 8, nc)[:, -1, :, 6:, :]
        conv_p = tails.transpose(0, 2, 1, 3).reshape(batch, 2, dff2)
        for lst, val in zip(outs[:5], (kaf.transpose(0, 3, 1, 2), vaf.transpose(0, 3, 1, 2),
                                       logf.reshape(batch, seq, A_HEADS), gla_state, conv_p)):
            lst.append(val)

        qa, ka, va, kaf, vaf, logf, qb, kb, vb, la = _inproj(ys, *inproj_w, seq=db * dt, tm=db * dt)
        kaf, vaf = (a.reshape(A_HEADS, A_DIM, db, dt).transpose(2, 0, 1, 3) for a in (kaf, vaf))
        hh = A_HEADS // 2
        eye = jnp.eye(hh, dtype=BF16)
        q_bd = jnp.einsum("btgld,lm->bgltmd", qa.reshape(db, dt, 2, hh, A_DIM), eye).reshape(
            db, 2, hh * dt, hh * A_DIM)
        padk = lambda a: jnp.pad(a[None], ((0, 0), (0, 0), (0, 0), (0, 0), (0, PAGE - dt)))
        lfn = jnp.pad(logf.reshape(1, db, dt, A_HEADS).transpose(0, 1, 3, 2),
                      ((0, 0), (0, 0), (0, 0), (0, PAGE - dt)))
        o_bd = _fox_sample(page_table, q_bd, padk(kaf), padk(vaf), lfn,
                           cache_k.transpose(0, 1, 3, 4, 2), cache_v.transpose(0, 1, 3, 4, 2),
                           cache_logf.transpose(0, 1, 3, 2), layer=l)
        oa = jnp.einsum("bgltmd,lm->btgld", o_bd.reshape(db, 2, hh, dt, hh, A_DIM),
                        jnp.eye(hh, dtype=F32)).reshape(db * dt, aw).astype(BF16)
        ob, gla_state = _gla_sample(qb, kb, vb, la, state_gla[l], db, dt)
        x1 = _mix(ys, oa, ob, *mix_w, tm=db * dt)
        hist = state_conv[l].astype(F32).reshape(db, state_conv.shape[2], dff2 // nc, nc).transpose(2, 0, 1, 3)
        ys, u = _ffn_sample(x1, hist, *ffn_w, t=dt)
        conv_s = u.transpose(1, 0, 2).reshape(db, dt, dff2)[:, dt - 2:, :]
        for lst, val in zip(outs[5:], (kaf.transpose(0, 3, 1, 2), vaf.transpose(0, 3, 1, 2),
                                       logf.reshape(db, dt, A_HEADS), gla_state, conv_s)):
            lst.append(val)

    stacked = [jnp.stack(o) for o in outs]
    return (yp.reshape(batch, seq, d), ys.reshape(db, dt, d), *stacked)
```

```python
import functools

import jax
import jax.numpy as jnp
import numpy as np
from jax import lax
from jax.experimental import pallas as pl
from jax.experimental.pallas import tpu as pltpu

F32 = jnp.float32
BF16 = jnp.bfloat16
EPS = 1e-6
GATE_TEMP = 16.0
PAGE = 128
A_HEADS, A_DIM = 8, 64
B_HEADS, B_DK, B_DV = 4, 128, 256
GATE_RANK = 16
LANES = 128
VMEM_LIMIT = 56 * 1024 * 1024
HIGHEST = lax.Precision.HIGHEST
LOG2E = float(np.log2(np.e))
NT = (((1,), (1,)), ((), ()))
TN = (((0,), (0,)), ((), ()))


def _cparams(*sem, fuse=None):
    return pltpu.CompilerParams(dimension_semantics=sem, vmem_limit_bytes=VMEM_LIMIT, allow_input_fusion=fuse)


def _resident(shape):
    nd = len(shape)
    return pl.BlockSpec(shape, lambda *_: (0,) * nd, pipeline_mode=pl.Buffered(1))


def _rms(x, g):
    return x * lax.rsqrt(jnp.mean(x * x, axis=-1, keepdims=True) + EPS) * g


def _log_sigmoid(x):
    return jnp.minimum(x, 0.0) - jnp.log1p(jnp.exp(-jnp.abs(x)))


def _dot(a, b):
    return jnp.dot(a, b, preferred_element_type=F32)


def _trunc_to_bf16_bits(x):
    bits = lax.bitcast_convert_type(x, jnp.int32) & jnp.int32(-65536)
    return lax.bitcast_convert_type(bits, F32)


def _split3(x):
    hi = _trunc_to_bf16_bits(x)
    r1 = x - hi
    mid = _trunc_to_bf16_bits(r1)
    lo = r1 - mid
    return hi.astype(BF16), mid.astype(BF16), lo.astype(BF16)


def _dot_sel(x, sel):
    return sum(_dot(part, sel) for part in _split3(x))


def _sel_dot(sel, x):
    return sum(_dot(sel, part) for part in _split3(x))


def _inproj_kernel(x_ref, g_ref, wbig_ref, wsm_ref, bf_ref, wa2_ref, ba_ref,
                   qa_ref, ka_ref, va_ref, kaf_ref, vaf_ref, logf_ref,
                   qb_ref, kb_ref, vb_ref, la_ref):
    hn = _rms(x_ref[...], g_ref[...]).astype(BF16)
    aw = A_HEADS * A_DIM
    kw = B_HEADS * B_DK

    def mm(lo, hi):
        return _dot(hn, wbig_ref[:, lo:hi])

    qa_ref[...] = (mm(0, aw) * (A_DIM ** -0.5 * LOG2E)).astype(BF16)
    k_a = mm(aw, 2 * aw)
    kaf_ref[0] = k_a.T.reshape(A_HEADS, A_DIM, k_a.shape[0])
    ka_ref[...] = k_a.astype(BF16)
    v_a = mm(2 * aw, 3 * aw)
    vaf_ref[0] = v_a.T.reshape(A_HEADS, A_DIM, v_a.shape[0])
    va_ref[...] = v_a.astype(BF16)
    o = 3 * aw
    qb_ref[...] = mm(o, o + kw) * B_DK ** -0.5
    kb_ref[...] = mm(o + kw, o + 2 * kw)
    vb_ref[...] = mm(o + 2 * kw, o + 2 * kw + B_HEADS * B_DV)
    us = _dot(hn, wsm_ref[...])
    logf_ref[...] = _log_sigmoid(us[:, :A_HEADS] + bf_ref[...])
    z = _dot(us[:, LANES:].astype(BF16), wa2_ref[...]) + ba_ref[...]
    la_ref[...] = _log_sigmoid(z) / GATE_TEMP


def _inproj(x, g_pre, wbig, wsm, b_f, wa2, b_alpha, seq, tm):
    t, d = x.shape
    aw, kw, vw = A_HEADS * A_DIM, B_HEADS * B_DK, B_HEADS * B_DV
    row = lambda n: pl.BlockSpec((tm, n), lambda i: (i, 0))
    outs = [(aw, BF16)] * 3 + [(aw, F32)] * 2 + [(A_HEADS, F32), (kw, F32), (kw, F32), (vw, F32), (kw, F32)]
    out_specs = [row(n) for n, _ in outs]
    out_shape = [jax.ShapeDtypeStruct((t, n), dt) for n, dt in outs]
    tiles_per_seq = seq // tm
    for i in (3, 4):
        out_specs[i] = pl.BlockSpec((1, A_HEADS, A_DIM, tm),
                                    lambda i: (i // tiles_per_seq, 0, 0, i % tiles_per_seq))
        out_shape[i] = jax.ShapeDtypeStruct((t // seq, A_HEADS, A_DIM, seq), F32)
    return pl.pallas_call(
        _inproj_kernel,
        grid=(t // tm,),
        in_specs=[row(d), _resident(g_pre.shape), _resident(wbig.shape), _resident(wsm.shape),
                  _resident(b_f.shape), _resident(wa2.shape), _resident(b_alpha.shape)],
        out_specs=out_specs,
        out_shape=out_shape,
        compiler_params=_cparams("arbitrary", fuse=[False, False, True, True, False, True, False]),
        name="inproj",
    )(x, g_pre, wbig, wsm, b_f, wa2, b_alpha)


def _key_bias_kernel(lf_ref, tril_ref, place_ref, o_ref, carry_sc, *, seg):
    @pl.when(pl.program_id(1) == 0)
    def _():
        carry_sc[...] = jnp.zeros(carry_sc.shape, F32)

    carry = carry_sc[...]
    for s in range(lf_ref.shape[0] // seg):
        rows = slice(s * seg, (s + 1) * seg)
        c = _sel_dot(tril_ref[...], lf_ref[rows, :]) + carry
        carry = c[seg - 1:seg, :]
        x = -c * LOG2E
        hi = _trunc_to_bf16_bits(x)
        r1 = x - hi
        mid = _trunc_to_bf16_bits(r1)
        parts = (hi, mid, r1 - mid)
        o_ref[rows, :] = sum(_dot(p, place_ref[j]) for j, p in enumerate(parts)).astype(o_ref.dtype)
    carry_sc[...] = carry


def _key_bias_lanes(logf, batch, seq, rows=2048, seg=256):
    heads = logf.shape[1]
    tril = jnp.tril(jnp.ones((seg, seg), BF16))
    place = np.zeros((3, heads, heads // 2 * LANES), np.float32)
    for j in range(3):
        for h in range(heads):
            place[j, h, (h // 2) * LANES + (h % 2) * 3 + j] = 1.0
    place = jnp.asarray(place)
    steps = seq // rows
    return pl.pallas_call(
        functools.partial(_key_bias_kernel, seg=seg),
        grid=(batch, steps),
        in_specs=[pl.BlockSpec((rows, heads), lambda b, i: (b * steps + i, 0)),
                  _resident(tril.shape), _resident(place.shape)],
        out_specs=pl.BlockSpec((rows, place.shape[2]), lambda b, i: (b * steps + i, 0)),
        out_shape=jax.ShapeDtypeStruct((batch * seq, place.shape[2]), BF16),
        scratch_shapes=[pltpu.VMEM((1, heads), F32)],
        compiler_params=_cparams("arbitrary", "arbitrary"),
        name="key_bias",
    )(logf, tril, place)


def _fox_prompt_kernel(q_ref, k_ref, v_ref, kb_ref, o_ref, m_sc, l_sc, acc_sc, *, tq, tk):
    i = pl.program_id(2)
    q = q_ref[...]
    lane = lax.broadcasted_iota(jnp.int32, q.shape, 1)
    zero = jnp.zeros_like(q)
    sel_even = jnp.where(lane < 3, 1.0, 0.0).astype(BF16)
    sel_odd = jnp.where(lane < 3, 0.0, jnp.where(lane < 6, 1.0, 0.0)).astype(BF16)
    q_even = jnp.concatenate([jnp.where(lane < A_DIM, q, zero), sel_even], axis=1)
    q_odd = jnp.concatenate([jnp.where(lane >= A_DIM, q, zero), sel_odd], axis=1)
    qs = jnp.concatenate([q_even, q_odd], axis=0)
    m_sc[...] = jnp.full(m_sc.shape, -jnp.inf, F32)
    l_sc[...] = jnp.zeros(l_sc.shape, F32)
    acc_sc[...] = jnp.zeros(acc_sc.shape, F32)
    reps = tk // LANES

    def block(j, masked, r0=0, nrows=2 * tq):
        rs = slice(r0, r0 + nrows)
        start = pl.multiple_of(j * tk, tk)
        ka = jnp.concatenate([k_ref[pl.ds(start, tk), :], kb_ref[pl.ds(start, tk), :]], axis=1)
        s = lax.dot_general(qs[rs], ka, NT, preferred_element_type=F32)
        if masked:
            r = lax.broadcasted_iota(jnp.int32, s.shape, 0) + r0
            c = lax.broadcasted_iota(jnp.int32, s.shape, 1)
            qpos = i * tq + jnp.where(r >= tq, r - tq, r)
            s = jnp.where(start + c <= qpos, s, -jnp.inf)
        m_old = m_sc[rs]
        m_new = jnp.maximum(m_old, jnp.max(s, axis=-1, keepdims=True))
        alpha = jnp.exp2(m_old - m_new)
        p = jnp.exp2(s - jnp.tile(m_new, (1, reps)))
        l_sc[rs] = alpha * l_sc[rs] + jnp.sum(p, axis=-1, keepdims=True)
        acc_sc[rs] = alpha * acc_sc[rs] + _dot(p.astype(BF16), v_ref[pl.ds(start, tk), :])
        m_sc[rs] = m_new

    n_full = i * (tq // tk)

    per_body = 4

    def body(jj, carry):
        for u in range(per_body):
            block(per_body * jj + u, False)
        return carry

    lax.fori_loop(0, n_full // per_body, body, 0)
    left = n_full % per_body
    for bit in (2, 1):
        @pl.when((left & bit) != 0)
        def _(bit=bit):
            first = n_full - (left & (2 * bit - 1))
            for u in range(bit):
                block(first + u, False)

    block(n_full, True)
    for d in range(1, tq // tk):
        for e in range(2):
            block(n_full + d, True, r0=e * tq + d * tk, nrows=tq - d * tk)
    o = acc_sc[...] / l_sc[...]
    o_ref[...] = jnp.where(lane < A_DIM, o[:tq], o[tq:]).astype(o_ref.dtype)


def _fox_prompt(qa, ka, va, kbias, batch, seq, tq, tk):
    assert tq % tk == 0
    nq = seq // tq
    pairs = A_HEADS // 2
    kv_spec = pl.BlockSpec((seq, LANES), lambda b, p, i: (b, p))
    q_spec = pl.BlockSpec((tq, LANES), lambda b, p, i: (b * nq + i, p))
    return pl.pallas_call(
        functools.partial(_fox_prompt_kernel, tq=tq, tk=tk),
        grid=(batch, pairs, nq),
        in_specs=[q_spec, kv_spec, kv_spec, kv_spec],
        out_specs=q_spec,
        out_shape=jax.ShapeDtypeStruct(qa.shape, BF16),
        scratch_shapes=[pltpu.VMEM((2 * tq, LANES), F32), pltpu.VMEM((2 * tq, LANES), F32),
                        pltpu.VMEM((2 * tq, LANES), F32)],
        compiler_params=_cparams("arbitrary", "arbitrary", "arbitrary"),
        name="fox_prompt",
    )(qa, ka, va, kbias)


def _fox_sample_kernel(pt_ref, q_ref, kn_ref, vn_ref, lfn_ref, tri_ref, sfx_ref, *rest, pages):
    page_refs = rest[:3 * pages]
    o_ref, m_sc, l_sc, acc_sc, carry_sc = rest[3 * pages:]
    g = pl.program_id(1)
    halves = 2
    hh = A_HEADS // halves
    nq = q_ref.shape[2] // hh

    def tiles(refs, half):
        return jnp.concatenate(
            [r[0, 0, half * hh:(half + 1) * hh].reshape(hh * A_DIM, PAGE).astype(BF16) for r in refs], axis=1)

    def rows_of(bias, half):
        return jnp.concatenate(
            [jnp.broadcast_to(bias[h:h + 1, :], (nq, bias.shape[1])) for h in range(half * hh, (half + 1) * hh)],
            axis=0) * LOG2E

    def update(half, s_list, v_groups):
        m_old = m_sc[half]
        m_row = jnp.max(s_list[0], axis=-1, keepdims=True)
        for s in s_list[1:]:
            m_row = jnp.maximum(m_row, jnp.max(s, axis=-1, keepdims=True))
        m_new = jnp.maximum(m_old, m_row)
        alpha = jnp.exp2(m_old - m_new)
        l_row = None
        pv = None
        for s, v_refs in zip(s_list, v_groups):
            p = jnp.exp2(s - jnp.tile(m_new, (1, s.shape[1] // LANES)))
            ls = jnp.sum(p, axis=-1, keepdims=True)
            l_row = ls if l_row is None else l_row + ls
            d = lax.dot_general(p.astype(BF16), tiles(v_refs, half), NT, preferred_element_type=F32)
            pv = d if pv is None else pv + d
        l_sc[half] = alpha * l_sc[half] + l_row
        acc_sc[half] = jnp.tile(alpha, (1, hh * A_DIM // LANES)) * acc_sc[half] + pv
        m_sc[half] = m_new

    @pl.when(g == 0)
    def _():
        m_sc[...] = jnp.full(m_sc.shape, -jnp.inf, F32)
        l_sc[...] = jnp.zeros(l_sc.shape, F32)
        acc_sc[...] = jnp.zeros(acc_sc.shape, F32)
        carry_sc[...] = jnp.zeros(carry_sc.shape, F32)
        lfn = lfn_ref[0, 0]
        cn = _dot_sel(jnp.concatenate([lfn, jnp.zeros_like(lfn)], axis=0), tri_ref[...])[:A_HEADS]
        for half in range(halves):
            s = _dot(q_ref[0, half], tiles([kn_ref], half)) - rows_of(cn, half)
            r = lax.broadcasted_iota(jnp.int32, s.shape, 0)
            c = lax.broadcasted_iota(jnp.int32, s.shape, 1)
            s = jnp.where(c <= (r & (nq - 1)), s, -jnp.inf)
            update(half, [s], [[vn_ref]])

    order = list(reversed(range(pages)))
    lf_all = jnp.concatenate([page_refs[2 * pages + p][0, 0] for p in order], axis=0)
    sfx_all = _dot_sel(lf_all, sfx_ref[...])
    carry = carry_sc[...]
    biases = []
    for i in range(pages):
        blk = sfx_all[i * A_HEADS:(i + 1) * A_HEADS]
        biases.append(blk[:, :PAGE] + carry)
        carry = carry + blk[:, PAGE:]
    carry_sc[...] = carry
    pairs = [(order[i], order[i + 1], jnp.concatenate(biases[i:i + 2], axis=1)) for i in range(0, pages, 2)]
    for half in range(halves):
        s_list = [_dot(q_ref[0, half], tiles([page_refs[a], page_refs[b]], half)) + rows_of(bias, half)
                  for a, b, bias in pairs]
        update(half, s_list, [[page_refs[pages + a], page_refs[pages + b]] for a, b, _ in pairs])

    @pl.when(g == pl.num_programs(1) - 1)
    def _():
        for half in range(halves):
            o_ref[0, half] = acc_sc[half] / jnp.tile(l_sc[half], (1, hh * A_DIM // LANES))


def _fox_sample(page_table, q, kn, vn, lfn, cache_kt, cache_vt, cache_lft, layer, pages=16):
    db, n_pages = page_table.shape
    groups = n_pages // pages
    assert pages % 2 == 0
    state = q.shape[1:3] + (LANES,)
    idx = np.arange(PAGE)
    tri = jnp.asarray(idx[:, None] <= idx[None, :], BF16)
    sfx = jnp.asarray(np.concatenate([idx[:, None] > idx[None, :], np.ones((PAGE, PAGE), bool)], 1), BF16)
    kv_page = (A_HEADS, A_DIM, PAGE)
    lf_page = (A_HEADS, PAGE)

    def per_b(shape):
        nd = len(shape)
        return pl.BlockSpec((1,) + shape, lambda b, g, pt: (b,) + (0,) * nd)

    def new_page(shape):
        nd = len(shape)
        return pl.BlockSpec((1, 1) + shape, lambda b, g, pt: (0, b) + (0,) * nd)

    def const(shape):
        nd = len(shape)
        return pl.BlockSpec(shape, lambda b, g, pt: (0,) * nd)

    def page_spec(shape, p):
        nd = len(shape)
        return pl.BlockSpec(
            (1, 1) + shape,
            lambda b, g, pt: (layer, pt[b * n_pages + (groups - 1 - g) * pages + p]) + (0,) * nd)

    in_specs = [per_b(q.shape[1:]), new_page(kv_page), new_page(kv_page), new_page(lf_page),
                const(tri.shape), const(sfx.shape)]
    in_specs += [page_spec(kv_page, p) for p in range(pages)]
    in_specs += [page_spec(kv_page, p) for p in range(pages)]
    in_specs += [page_spec(lf_page, p) for p in range(pages)]
    grid_spec = pltpu.PrefetchScalarGridSpec(
        num_scalar_prefetch=1,
        grid=(db, groups),
        in_specs=in_specs,
        out_specs=per_b(q.shape[1:]),
        scratch_shapes=[pltpu.VMEM(state, F32), pltpu.VMEM(state, F32),
                        pltpu.VMEM(q.shape[1:], F32), pltpu.VMEM((A_HEADS, LANES), F32)],
    )
    return pl.pallas_call(
        functools.partial(_fox_sample_kernel, pages=pages),
        grid_spec=grid_spec,
        out_shape=jax.ShapeDtypeStruct(q.shape, F32),
        compiler_params=_cparams("arbitrary", "arbitrary"),
        name="fox_sample",
    )(page_table.reshape(-1), q, kn, vn, lfn, tri, sfx,
      *([cache_kt] * pages), *([cache_vt] * pages), *([cache_lft] * pages))


def _gla_chunk(q, k, v, la, s_old, tril, row0, ones, cast):
    c = q.shape[0]
    bc = jnp.dot(tril, la, precision=HIGHEST, preferred_element_type=F32)
    ref = bc[c // 2 - 1:c // 2, :]
    last = bc[c - 1:c, :]
    qt = q * jnp.exp(bc - ref)
    kt = k * jnp.exp(ref - bc)
    o = _dot(cast(qt * jnp.exp(ref)), cast(s_old))
    a = lax.dot_general(cast(qt), cast(kt), NT, preferred_element_type=F32)
    a = jnp.where(tril > 0, a, 0.0)
    vc = cast(v)
    o = o + _dot(cast(a), vc)
    kd = cast(kt * jnp.exp(last - ref))
    scale = lax.dot_general(row0 * jnp.exp(last), ones, TN, precision=HIGHEST, preferred_element_type=F32)
    s_new = scale * s_old + lax.dot_general(kd, vc, TN, preferred_element_type=F32)
    return o, s_new


def _gla_prompt_kernel(q_ref, k_ref, v_ref, la_ref, tril_ref, o_ref, st_ref, *, chunk):
    @pl.when(pl.program_id(0) == 0)
    def _():
        st_ref[...] = jnp.zeros(st_ref.shape, F32)

    tril = tril_ref[...]
    r = lax.broadcasted_iota(jnp.int32, tril.shape, 0)
    c = lax.broadcasted_iota(jnp.int32, tril.shape, 1)
    causal = c <= r

    nb = q_ref.shape[0]
    chains = [(b, h) for b in range(nb) for h in range(B_HEADS)]
    kcols = lambda h: slice(h * B_DK, (h + 1) * B_DK)
    vcols = lambda h: slice(h * B_DV, (h + 1) * B_DV)

    def body(ci, states):
        rows = pl.ds(pl.multiple_of(ci * chunk, chunk), chunk)
        qs, kd, decay, qtb, ktb = [], [], [], [], []
        for b in range(nb):
            bc = _sel_dot(tril, la_ref[b, rows, :])
            ref = bc[chunk // 2 - 1:chunk // 2, :]
            last = bc[chunk - 1:chunk, :]
            qt = q_ref[b, rows, :] * jnp.exp(bc - ref)
            kt = k_ref[b, rows, :] * jnp.exp(ref - bc)
            qs.append((qt * jnp.exp(ref)).astype(BF16))
            kd.append((kt * jnp.exp(last - ref)).astype(BF16))
            decay.append(jnp.exp(last))
            qtb.append(qt.astype(BF16))
            ktb.append(kt.astype(BF16))
        vbs = [v_ref[b, rows, vcols(h)].astype(BF16) for b, h in chains]
        intra = []
        for (b, h), vb in zip(chains, vbs):
            a = lax.dot_general(qtb[b][:, kcols(h)], ktb[b][:, kcols(h)], NT, preferred_element_type=F32)
            intra.append(_dot(jnp.where(causal, a, 0.0).astype(BF16), vb))
        new_states = []
        for (b, h), vb, st, o_intra in zip(chains, vbs, states, intra):
            o = lax.dot_general(qs[b][:, kcols(h)], st.astype(BF16), NT, preferred_element_type=F32)
            o_ref[b, rows, vcols(h)] = o + o_intra
            new_states.append(st * decay[b][:, kcols(h)]
                              + lax.dot_general(vb, kd[b][:, kcols(h)], TN, preferred_element_type=F32))
        return tuple(new_states)

    states = lax.fori_loop(0, q_ref.shape[1] // chunk, body, tuple(st_ref[b, h] for b, h in chains),
                           unroll=2)
    for (b, h), st in zip(chains, states):
        st_ref[b, h] = st


def _gla_consts(chunk):
    tril = jnp.tril(jnp.ones((chunk, chunk), F32))
    row0 = jnp.zeros((8, B_DK), F32).at[0].set(1.0)
    ones = jnp.ones((8, B_DV), F32)
    return tril, row0, ones


def _gla_prompt(qb, kb, vb, la, batch, seq, rows, chunk=64):
    kw, vw = B_HEADS * B_DK, B_HEADS * B_DV
    tril = jnp.tril(jnp.ones((chunk, chunk), BF16))
    rk = pl.BlockSpec((batch, rows, kw), lambda i: (0, i, 0))
    rv = pl.BlockSpec((batch, rows, vw), lambda i: (0, i, 0))
    return pl.pallas_call(
        functools.partial(_gla_prompt_kernel, chunk=chunk),
        grid=(seq // rows,),
        in_specs=[rk, rk, rv, rk, _resident(tril.shape)],
        out_specs=[rv, pl.BlockSpec((batch, B_HEADS, B_DV, B_DK), lambda i: (0, 0, 0, 0))],
        out_shape=[jax.ShapeDtypeStruct(vb.shape, F32),
                   jax.ShapeDtypeStruct((batch, B_HEADS, B_DV, B_DK), F32)],
        compiler_params=_cparams("arbitrary"),
        name="gla_prompt",
    )(qb, kb, vb, la, tril)


def _gla_sample_kernel(q_ref, k_ref, v_ref, la_ref, s_in_ref, tril_ref, row0_ref, ones_ref, o_ref, s_ref, *, t):
    cast = lambda a: a
    tril, row0, ones = tril_ref[...], row0_ref[...], ones_ref[...]
    results = []
    for b in range(q_ref.shape[0] // t):
        rows = slice(b * t, (b + 1) * t)
        for h in range(B_HEADS):
            kc = slice(h * B_DK, (h + 1) * B_DK)
            vc = slice(h * B_DV, (h + 1) * B_DV)
            results.append((b, h, rows, vc,
                            _gla_chunk(q_ref[rows, kc], k_ref[rows, kc], v_ref[rows, vc], la_ref[rows, kc],
                                       s_in_ref[b, h], tril, row0, ones, cast)))
    for b, h, rows, vc, (o, s_new) in results:
        o_ref[rows, vc] = o
        s_ref[b, h] = s_new


def _gla_sample(qb, kb, vb, la, state, db, t, per_step=4):
    kw, vw = B_HEADS * B_DK, B_HEADS * B_DV
    consts = _gla_consts(t)
    rk = pl.BlockSpec((per_step * t, kw), lambda b: (b, 0))
    rv = pl.BlockSpec((per_step * t, vw), lambda b: (b, 0))
    st = pl.BlockSpec((per_step, B_HEADS, B_DK, B_DV), lambda b: (b, 0, 0, 0))
    return pl.pallas_call(
        functools.partial(_gla_sample_kernel, t=t),
        grid=(db // per_step,),
        in_specs=[rk, rk, rv, rk, st] + [_resident(c.shape) for c in consts],
        out_specs=[rv, st],
        out_shape=[jax.ShapeDtypeStruct(vb.shape, F32), jax.ShapeDtypeStruct(state.shape, F32)],
        compiler_params=_cparams("arbitrary"),
        name="gla_sample",
    )(qb, kb, vb, la, state, *consts)


def _mix_kernel(x_ref, oa_ref, ob_ref, gpre_ref, wr_ref, wga_ref, wgb_ref, ggla_ref,
                wpa_ref, wpb_ref, wout_ref, gpost_ref, y_ref):
    x = x_ref[...]
    hn = _rms(x, gpre_ref[...]).astype(BF16)
    r_b = _dot(hn, wr_ref[...])
    ob = ob_ref[...]
    parts = []
    for h in range(B_HEADS):
        seg = ob[:, h * B_DV:(h + 1) * B_DV]
        parts.append(_rms(seg, ggla_ref[...]))
    obn = jnp.concatenate(parts, axis=-1) * (r_b * jax.nn.sigmoid(r_b))
    pa = _dot(oa_ref[...], wpa_ref[...])
    pb = _dot(obn.astype(BF16), wpb_ref[...])
    merged = jax.nn.sigmoid(_dot(hn, wga_ref[...])) * pa + jax.nn.sigmoid(_dot(hn, wgb_ref[...])) * pb
    z = _dot(merged.astype(BF16), wout_ref[...])
    y_ref[...] = x + _rms(z, gpost_ref[...])


def _mix(x, oa, ob, g_pre, w_r, w_ga, w_gb, g_gla, w_pa, w_pb, w_out, g_post, tm):
    t, d = x.shape
    row = lambda n: pl.BlockSpec((tm, n), lambda i: (i, 0))
    consts = (g_pre, w_r, w_ga, w_gb, g_gla, w_pa, w_pb, w_out, g_post)
    return pl.pallas_call(
        _mix_kernel,
        grid=(t // tm,),
        in_specs=[row(d), row(oa.shape[1]), row(ob.shape[1])] + [_resident(c.shape) for c in consts],
        out_specs=row(d),
        out_shape=jax.ShapeDtypeStruct(x.shape, F32),
        compiler_params=_cparams("arbitrary", fuse=[False] * 3 + [c.dtype == BF16 for c in consts]),
        name="mix",
    )(x, oa, ob, *consts)


def _gelu_tanh(g):
    return 0.5 * g * (1.0 + jnp.tanh(np.sqrt(2.0 / np.pi) * (g + 0.044715 * (g * g * g))))


def _ffn_body(x_ref, gpre_ref, wup_ref, wconv_ref, bconv_ref, wdown_ref, gpost_ref, y_ref, tail_ref,
              h_sc, shifted):
    x = x_ref[...]
    hn = _rms(x, gpre_ref[...]).astype(BF16)
    n_chunks = wup_ref.shape[0] // 2
    nc = wup_ref.shape[2]
    for j in range(n_chunks):
        halves = []
        for half in range(2):
            idx = half * n_chunks + j
            u = _dot(hn, wup_ref[idx])
            u1, u2 = shifted(idx, u)
            w = wconv_ref[idx]
            halves.append(bconv_ref[idx] + u2 * w[0:1] + u1 * w[1:2] + u * w[2:3])
            tail_ref[idx] = u[u.shape[0] - 8:, :]
        h_sc[:, j * nc:(j + 1) * nc] = (_gelu_tanh(halves[1]) * halves[0]).astype(BF16)
    y_ref[...] = x + _rms(_dot(h_sc[...], wdown_ref[...]), gpost_ref[...])


def _ffn_prompt_kernel(x_ref, gpre_ref, wup_ref, wconv_ref, bconv_ref, wdown_ref, gpost_ref,
                       y_ref, tail_ref, h_sc, hist_sc, *, tiles_per_seq):
    first = pl.program_id(0) % tiles_per_seq == 0

    @pl.when(first)
    def _():
        hist_sc[...] = jnp.zeros(hist_sc.shape, F32)

    def shifted(idx, u):
        rows = lax.broadcasted_iota(jnp.int32, u.shape, 0)
        p1 = hist_sc[idx, 7:8, :]
        p2 = hist_sc[idx, 6:7, :]
        u1 = jnp.where(rows == 0, p1, pltpu.roll(u, 1, axis=0))
        u2 = jnp.where(rows == 0, p2, jnp.where(rows == 1, p1, pltpu.roll(u, 2, axis=0)))
        hist_sc[idx] = u[u.shape[0] - 8:, :]
        return u1, u2

    _ffn_body(x_ref, gpre_ref, wup_ref, wconv_ref, bconv_ref, wdown_ref, gpost_ref, y_ref,
              tail_ref.at[0], h_sc, shifted)


def _ffn_sample_kernel(x_ref, st_ref, gpre_ref, wup_ref, wconv_ref, bconv_ref, wdown_ref,
                       gpost_ref, y_ref, u_ref, h_sc, *, t):
    x = x_ref[...]
    hn = _rms(x, gpre_ref[...]).astype(BF16)
    n_chunks = wup_ref.shape[0] // 2
    nc = wup_ref.shape[2]
    nseq = x.shape[0] // t
    pos = lax.broadcasted_iota(jnp.int32, (nseq, t, nc), 1)
    for j in range(n_chunks):
        halves = []
        for half in range(2):
            idx = half * n_chunks + j
            u = _dot(hn, wup_ref[idx])
            u_ref[idx] = u
            u = u.reshape(nseq, t, nc)
            st = st_ref[idx]
            old2 = jnp.broadcast_to(st[:, 0:1, :], u.shape)
            old1 = jnp.broadcast_to(st[:, 1:2, :], u.shape)
            u1 = jnp.where(pos < 1, old1, pltpu.roll(u, 1, axis=1))
            u2 = jnp.where(pos < 1, old2, jnp.where(pos < 2, old1, pltpu.roll(u, 2, axis=1)))
            w = wconv_ref[idx]
            halves.append(bconv_ref[idx] + u2 * w[0:1] + u1 * w[1:2] + u * w[2:3])
        h = (_gelu_tanh(halves[1]) * halves[0]).reshape(x.shape[0], nc)
        h_sc[:, j * nc:(j + 1) * nc] = h.astype(BF16)
    y_ref[...] = x + _rms(_dot(h_sc[...], wdown_ref[...]), gpost_ref[...])


def _ffn_prompt(x, g_pre, wup, wconv, bconv, wdown, g_post, seq, tm):
    t, d = x.shape
    nblk, _, nc = wup.shape
    tiles = t // tm
    row = pl.BlockSpec((tm, d), lambda i: (i, 0))
    consts = (g_pre, wup, wconv, bconv, wdown, g_post)
    return pl.pallas_call(
        functools.partial(_ffn_prompt_kernel, tiles_per_seq=seq // tm),
        grid=(tiles,),
        in_specs=[row] + [_resident(c.shape) for c in consts],
        out_specs=[row, pl.BlockSpec((1, nblk, 8, nc), lambda i: (i, 0, 0, 0))],
        out_shape=[jax.ShapeDtypeStruct(x.shape, F32), jax.ShapeDtypeStruct((tiles, nblk, 8, nc), F32)],
        scratch_shapes=[pltpu.VMEM((tm, wdown.shape[0]), BF16), pltpu.VMEM((nblk, 8, nc), F32)],
        compiler_params=_cparams("arbitrary", fuse=[False] + [c.dtype == BF16 for c in consts]),
        name="ffn_prompt",
    )(x, *consts)


def _ffn_sample(x, hist, g_pre, wup, wconv, bconv, wdown, g_post, t):
    rows, d = x.shape
    nblk, _, nc = wup.shape
    consts = (g_pre, wup, wconv, bconv, wdown, g_post)
    return pl.pallas_call(
        functools.partial(_ffn_sample_kernel, t=t),
        grid=(1,),
        in_specs=[_resident(a.shape) for a in (x, hist) + consts],
        out_specs=[pl.BlockSpec(x.shape, lambda i: (0, 0)),
                   pl.BlockSpec((nblk, rows, nc), lambda i: (0, 0, 0))],
        out_shape=[jax.ShapeDtypeStruct(x.shape, F32), jax.ShapeDtypeStruct((nblk, rows, nc), F32)],
        scratch_shapes=[pltpu.VMEM((rows, wdown.shape[0]), BF16)],
        compiler_params=_cparams("arbitrary"),
        name="ffn_sample",
    )(x, hist, *consts)


def _pack_layer(w_in, w_alpha2, w_up, w_conv, b_conv, w_down, nc):
    aw, kw, vw = A_HEADS * A_DIM, B_HEADS * B_DK, B_HEADS * B_DV
    d = w_in.shape[0]
    offs = np.cumsum([aw, aw, aw, A_HEADS, kw, kw, vw, GATE_RANK, vw, d, d])[:-1].tolist()
    q_a, k_a, v_a, f_a, q_b, k_b, v_b, a_low, r_b, g_a, g_b = jnp.split(w_in, offs, axis=1)
    wbig = jnp.concatenate([q_a, k_a, v_a, q_b, k_b, v_b], axis=1).astype(BF16)
    pad = lambda a: jnp.pad(a, ((0, 0), (0, LANES - a.shape[1])))
    wsm = jnp.concatenate([pad(f_a), pad(a_low)], axis=1).astype(BF16)
    wa2 = jnp.pad(w_alpha2, ((0, LANES - GATE_RANK), (0, 0))).astype(BF16)
    dff = w_down.shape[0]
    n_chunks = dff // nc
    wup = w_up.reshape(d, 2 * n_chunks, nc).transpose(1, 0, 2).astype(BF16)
    wconv = w_conv.reshape(w_conv.shape[0], 2 * n_chunks, nc).transpose(1, 0, 2)
    bconv = b_conv.reshape(2 * n_chunks, 1, nc)
    wdown = w_down.astype(BF16)
    return dict(wbig=wbig, wsm=wsm, wa2=wa2, w_r=r_b.astype(BF16), w_ga=g_a.astype(BF16),
                w_gb=g_b.astype(BF16), wup=wup, wconv=wconv, bconv=bconv, wdown=wdown)


def kernel(x_prompt, x_sample, cache_k, cache_v, cache_logf, state_gla, state_conv, page_table,
           g_pre_mix, w_in, b_f, w_alpha2, b_alpha, g_gla, w_proj_a, w_proj_b, w_out, g_post_mix,
           g_pre_ffn, w_up, w_conv, b_conv, w_down, g_post_ffn):
    batch, seq, d = x_prompt.shape
    db, dt, _ = x_sample.shape
    depth = w_in.shape[0]
    n_pool = cache_k.shape[1]
    aw = A_HEADS * A_DIM
    dff2 = w_up.shape[2]
    nc = 256
    assert dt == 8 and depth >= 1

    yp = x_prompt.reshape(batch * seq, d)
    ys = x_sample.reshape(db * dt, d)
    outs = [[] for _ in range(10)]
    for l in range(depth):
        w = _pack_layer(w_in[l], w_alpha2[l], w_up[l], w_conv[l], b_conv[l], w_down[l], nc)
        row = lambda a: a[l].reshape(1, -1)
        inproj_w = (row(g_pre_mix), w["wbig"], w["wsm"], row(b_f), w["wa2"], row(b_alpha))
        mix_w = (row(g_pre_mix), w["w_r"], w["w_ga"], w["w_gb"], row(g_gla), w_proj_a[l].astype(BF16),
                 w_proj_b[l].astype(BF16), w_out[l].astype(BF16), row(g_post_mix))
        ffn_w = (row(g_pre_ffn), w["wup"], w["wconv"], w["bconv"], w["wdown"], row(g_post_ffn))

        qa, ka, va, kaf, vaf, logf, qb, kb, vb, la = _inproj(yp, *inproj_w, seq=seq, tm=512)
        oa = _fox_prompt(qa, ka, va, _key_bias_lanes(logf, batch, seq), batch, seq, tq=1024, tk=512)
        seqs = lambda a: a.reshape(batch, seq, a.shape[1])
        ob, gla_state_t = _gla_prompt(seqs(qb), seqs(kb), seqs(vb), seqs(la), batch, seq, rows=512)
        ob = ob.reshape(batch * seq, ob.shape[2])
        gla_state = gla_state_t.transpose(0, 1, 3, 2)
        x1 = _mix(yp, oa, ob, *mix_w, tm=1024)
        ffn_tm = 1024
        yp, tails = _ffn_prompt(x1, *ffn_w, seq=seq, tm=ffn_tm)
        tiles_per_seq = seq // ffn_tm
        tails = tails.reshape(batch, tiles_per_seq, 2 * (dff2 // 2 // nc), 8, nc)[:, -1, :, 6:, :]
        conv_p = tails.transpose(0, 2, 1, 3).reshape(batch, 2, dff2)
        for lst, val in zip(outs[:5], (kaf.transpose(0, 3, 1, 2), vaf.transpose(0, 3, 1, 2),
                                       logf.reshape(batch, seq, A_HEADS), gla_state, conv_p)):
            lst.append(val)

        qa, ka, va, kaf, vaf, logf, qb, kb, vb, la = _inproj(ys, *inproj_w, seq=db * dt, tm=db * dt)
        kaf, vaf = (a.reshape(A_HEADS, A_DIM, db, dt).transpose(2, 0, 1, 3) for a in (kaf, vaf))
        hh = A_HEADS // 2
        eye = jnp.eye(hh, dtype=BF16)
        q_bd = jnp.einsum("btgld,lm->bgltmd", qa.reshape(db, dt, 2, hh, A_DIM), eye).reshape(
            db, 2, hh * dt, hh * A_DIM)
        padk = lambda a: jnp.pad(a[None], ((0, 0), (0, 0), (0, 0), (0, 0), (0, PAGE - dt)))
        lfn = jnp.pad(logf.reshape(1, db, dt, A_HEADS).transpose(0, 1, 3, 2),
                      ((0, 0), (0, 0), (0, 0), (0, PAGE - dt)))
        o_bd = _fox_sample(page_table, q_bd, padk(kaf), padk(vaf), lfn,
                           cache_k.transpose(0, 1, 3, 4, 2), cache_v.transpose(0, 1, 3, 4, 2),
                           cache_logf.transpose(0, 1, 3, 2), layer=l)
        oa = jnp.einsum("bgltmd,lm->btgld", o_bd.reshape(db, 2, hh, dt, hh, A_DIM),
                        jnp.eye(hh, dtype=F32)).reshape(db * dt, aw).astype(BF16)
        ob, gla_state = _gla_sample(qb, kb, vb, la, state_gla[l], db, dt)
        x1 = _mix(ys, oa, ob, *mix_w, tm=db * dt)
        hist = state_conv[l].astype(F32).reshape(db, state_conv.shape[2], dff2 // nc, nc).transpose(2, 0, 1, 3)
        ys, u = _ffn_sample(x1, hist, *ffn_w, t=dt)
        conv_s = u.transpose(1, 0, 2).reshape(db, dt, dff2)[:, dt - 2:, :]
        for lst, val in zip(outs[5:], (kaf.transpose(0, 3, 1, 2), vaf.transpose(0, 3, 1, 2),
                                       logf.reshape(db, dt, A_HEADS), gla_state, conv_s)):
            lst.append(val)

    stacked = [jnp.stack(o) for o in outs]
    return (yp.reshape(batch, seq, d), ys.reshape(db, dt, d), *stacked)
```
